```python
import jax, jax.numpy as jnp
from jax import lax
import numpy as np

D_MODEL = 1024
BATCH = 32
SEQ = 2048
DEPTH = 1

D_MIX = D_MODEL
D_SGU = D_MIX // 2
SGU_GROUPS = 4
SGU_GROUP_DIM = D_SGU // SGU_GROUPS
CHUNK = 128
N_HEADS = 8
HEAD_DIM = (D_MIX - D_SGU) // N_HEADS
N_KV_HEADS = 2
GQA_GROUP = N_HEADS // N_KV_HEADS
ROT_DIM = HEAD_DIM // 4
ROPE_THETA = 500000.0
IDX_HEADS = 8
IDX_DIM = 64
TOPK_MAX = 256
Q_BLOCK = 128
D_FF = 4 * D_MODEL
EPS = 1e-6
COL_SIZES = (D_SGU, D_SGU,
             N_HEADS * HEAD_DIM,
             N_KV_HEADS * HEAD_DIM,
             N_KV_HEADS * HEAD_DIM,
             IDX_HEADS * IDX_DIM,
             IDX_DIM,
             IDX_HEADS)
D_IN = sum(COL_SIZES)

kernel_name = "hybrid_sgu_dsa_block"


def rms_norm(x, g):
    x32 = x.astype(jnp.float32)
    y = x32 * lax.rsqrt(jnp.mean(x32 * x32, axis=-1, keepdims=True) + EPS)
    return (y * g.astype(jnp.float32)).astype(x.dtype)


def rope_tables(positions):
    inv_freq = ROPE_THETA ** (-jnp.arange(0, ROT_DIM, 2, dtype=jnp.float32) / ROT_DIM)
    ang = positions.astype(jnp.float32)[..., None] * inv_freq
    return jnp.cos(ang), jnp.sin(ang)


def partial_rope(x, cos, sin):
    half = ROT_DIM // 2
    c = cos[:, :, None, :].astype(x.dtype)
    s = sin[:, :, None, :].astype(x.dtype)
    x1, x2, xp = x[..., :half], x[..., half:ROT_DIM], x[..., ROT_DIM:]
    return jnp.concatenate([x1 * c - x2 * s, x2 * c + x1 * s, xp], axis=-1)


def split_cols(p):
    outs, start = [], 0
    for n in COL_SIZES:
        outs.append(p[..., start:start + n])
        start += n
    return outs


def spatial_gating(u, v, g_v, w_s, b_s):
    B, S, _ = u.shape
    v = rms_norm(v.reshape(B, S, SGU_GROUPS, SGU_GROUP_DIM), g_v.reshape(SGU_GROUPS, SGU_GROUP_DIM))
    v = v.reshape(B, S // CHUNK, CHUNK, SGU_GROUPS, SGU_GROUP_DIM)
    causal = jnp.tril(jnp.ones((CHUNK, CHUNK), dtype=w_s.dtype))
    w_m = w_s * causal[None]
    mixed = jnp.einsum('gts,bnsgc->bntgc', w_m, v) + b_s.T[None, None, :, :, None]
    return u * mixed.reshape(B, S, D_SGU)


def dsa_attention(q, k, v, q_idx, k_idx, w_idx):
    B, S = q.shape[0], q.shape[1]
    topk = min(TOPK_MAX, S // 4)
    nb = S // Q_BLOCK
    key_pos = jnp.arange(S)

    def blocks(a):
        return a.reshape((B, nb, Q_BLOCK) + a.shape[2:]).swapaxes(0, 1)

    def one_block(args):
        qb, qib, wb, start = args
        t_pos = start + jnp.arange(Q_BLOCK)
        causal = key_pos[None, :] <= t_pos[:, None]
        dots = jnp.einsum('bthd,bsd->bths', qib, k_idx,
                          preferred_element_type=jnp.float32) * (IDX_DIM ** -0.5)
        score = jnp.einsum('bth,bths->bts', wb.astype(jnp.float32) * (IDX_HEADS ** -0.5),
                           jax.nn.relu(dots))
        score = jnp.where(causal[None], score, -jnp.inf)
        _, idx = lax.top_k(score, topk)
        sel_ok = idx <= t_pos[None, :, None]
        kg = jax.vmap(lambda kb, ib: kb[ib])(k, idx)
        vg = jax.vmap(lambda vb, ib: vb[ib])(v, idx)
        qg = qb.reshape(B, Q_BLOCK, N_KV_HEADS, GQA_GROUP, HEAD_DIM)
        logits = jnp.einsum('btkgd,btjkd->btkgj', qg, kg,
                            preferred_element_type=jnp.float32) * (HEAD_DIM ** -0.5)
        logits = jnp.where(sel_ok[:, :, None, None, :], logits, -jnp.inf)
        p = jax.nn.softmax(logits, axis=-1).astype(v.dtype)
        o = jnp.einsum('btkgj,btjkd->btkgd', p, vg)
        return o.reshape(B, Q_BLOCK, N_HEADS * HEAD_DIM)

    starts = jnp.arange(nb, dtype=jnp.int32) * Q_BLOCK
    out = lax.map(one_block, (blocks(q), blocks(q_idx), blocks(w_idx), starts))
    return out.swapaxes(0, 1).reshape(B, S, N_HEADS * HEAD_DIM)


def setup_inputs(seed: int = 0) -> dict:
    key = jax.random.key(seed)
    ks = jax.random.split(key, 24)
    f32 = jnp.float32
    L, D = DEPTH, D_MODEL

    def nrm(k, shape, scale):
        return jax.random.normal(k, shape, f32) * scale

    def gain(k, shape):
        return 1.0 + 0.02 * jax.random.normal(k, shape, f32)

    x = jax.random.normal(ks[0], (BATCH, SEQ, D), f32)
    c = jax.random.normal(ks[1], (BATCH, D), f32)
    offset = jax.random.randint(ks[2], (BATCH, 1), 0, 1024, dtype=jnp.int32)
    positions = offset + jnp.arange(SEQ, dtype=jnp.int32)[None, :]
    return {
        "x": x,
        "c": c,
        "positions": positions,
        "w_ada": nrm(ks[3], (L, D, 6 * D), D ** -0.5),
        "b_ada": nrm(ks[4], (L, 6 * D), 0.02),
        "g_pre_mix": gain(ks[5], (L, D)),
        "w_in": nrm(ks[6], (L, D, D_IN), D ** -0.5),
        "g_sgu_v": gain(ks[7], (L, D_SGU)),
        "w_spatial": nrm(ks[8], (L, SGU_GROUPS, CHUNK, CHUNK), CHUNK ** -0.5),
        "b_spatial": gain(ks[9], (L, SGU_GROUPS, CHUNK)),
        "g_out_sgu": gain(ks[10], (L, D_SGU)),
        "g_out_attn": gain(ks[11], (L, N_HEADS * HEAD_DIM)),
        "w_out": nrm(ks[12], (L, D_MIX, D), D_MIX ** -0.5),
        "g_post_mix": gain(ks[13], (L, D)),
        "g_pre_ffn": gain(ks[14], (L, D)),
        "w_ff1": nrm(ks[15], (L, D, D_FF), D ** -0.5),
        "w_ff2": nrm(ks[16], (L, D_FF, D), D_FF ** -0.5),
        "g_post_ffn": gain(ks[17], (L, D)),
    }


def reference(x, c, positions, w_ada, b_ada, g_pre_mix, w_in, g_sgu_v, w_spatial, b_spatial,
              g_out_sgu, g_out_attn, w_out, g_post_mix, g_pre_ffn, w_ff1, w_ff2, g_post_ffn):
    B, S, _ = x.shape
    cos, sin = rope_tables(positions)
    c_act = jax.nn.silu(c)
    for l in range(DEPTH):
        mod = jnp.einsum('bd,de->be', c_act, w_ada[l]) + b_ada[l]
        sh1, sc1, g1, sh2, sc2, g2 = jnp.split(mod[:, None, :], 6, axis=-1)

        h = rms_norm(x, g_pre_mix[l]) * (1.0 + sc1) + sh1
        proj = jnp.einsum('bsd,de->bse', h, w_in[l])
        p_u, p_v, p_q, p_k, p_val, p_qi, p_ki, p_wi = split_cols(proj)

        z_u, z_v = jax.nn.gelu(p_u), jax.nn.gelu(p_v)
        y_a = spatial_gating(z_u, z_v, g_sgu_v[l], w_spatial[l], b_spatial[l])

        q = partial_rope(p_q.reshape(B, S, N_HEADS, HEAD_DIM), cos, sin)
        k = partial_rope(p_k.reshape(B, S, N_KV_HEADS, HEAD_DIM), cos, sin)
        v = p_val.reshape(B, S, N_KV_HEADS, HEAD_DIM)
        q_idx = partial_rope(p_qi.reshape(B, S, IDX_HEADS, IDX_DIM), cos, sin)
        k_idx = partial_rope(p_ki.reshape(B, S, 1, IDX_DIM), cos, sin)[:, :, 0, :]
        y_b = dsa_attention(q, k, v, q_idx, k_idx, p_wi)

        merged = jnp.concatenate([rms_norm(y_a, g_out_sgu[l]), rms_norm(y_b, g_out_attn[l])], axis=-1)
        o = jnp.einsum('bse,ed->bsd', merged, w_out[l])
        x = x + g1 * rms_norm(o, g_post_mix[l])

        h2 = rms_norm(x, g_pre_ffn[l]) * (1.0 + sc2) + sh2
        f = jnp.square(jax.nn.relu(jnp.einsum('bsd,df->bsf', h2, w_ff1[l])))
        f = jnp.einsum('bsf,fd->bsd', f, w_ff2[l])
        x = x + g2 * rms_norm(f, g_post_ffn[l])
    return x
```

```python
import functools

import jax
import jax.numpy as jnp
from jax import lax
from jax.experimental import pallas as pl
from jax.experimental.pallas import tpu as pltpu

F32 = jnp.float32
BF16 = jnp.bfloat16

SGU_GROUPS = 4
SGU_GROUP_DIM = 128
CHUNK = 128
N_HEADS = 8
HEAD_DIM = 64
N_KV_HEADS = 2
GQA_GROUP = N_HEADS // N_KV_HEADS
ROT_DIM = 16
ROT_HALF = ROT_DIM // 2
ROPE_THETA = 500000.0
IDX_HEADS = 8
IDX_DIM = 64
TOPK_MAX = 256
EPS = 1e-6

ADA_TN = 1024
PROJ_TM = 512
ATT_TQ = 256
ATT_CH = 256
FFN_TM = 512
FFN_TF = 1024
BISECT_UNROLL = 4
IDX_K = 4 * IDX_DIM
ONES_ROWS = 16
NEG = -1e30
VMEM_LIMIT_BYTES = 56 * 1024 * 1024


def _split_bf16(a):
    hi = a.astype(BF16).astype(F32)
    lo = (a - hi).astype(BF16).astype(F32)
    return hi, lo


def _rms(a, g):
    return a * lax.rsqrt(jnp.mean(a * a, axis=-1, keepdims=True) + EPS) * g


def _ada_kernel(c_ref, w_ref, b_ref, o_ref):
    c = c_ref[...]
    ca = c * (1.0 / (1.0 + jnp.exp(-c)))
    c_hi, c_lo = _split_bf16(ca)
    w_hi, w_lo = _split_bf16(w_ref[...])
    c_hi, c_lo, w_hi, w_lo = (t.astype(BF16) for t in (c_hi, c_lo, w_hi, w_lo))
    acc = jnp.dot(c_hi, w_hi, preferred_element_type=F32)
    acc += jnp.dot(c_hi, w_lo, preferred_element_type=F32)
    acc += jnp.dot(c_lo, w_hi, preferred_element_type=F32)
    o_ref[...] = acc + b_ref[...]


def _ada_mod(c, w_ada, b_ada):
    bsz, d = c.shape
    n = w_ada.shape[1]
    return pl.pallas_call(
        _ada_kernel,
        grid=(n // ADA_TN,),
        in_specs=[
            pl.BlockSpec((bsz, d), lambda i: (0, 0)),
            pl.BlockSpec((d, ADA_TN), lambda i: (0, i)),
            pl.BlockSpec((1, ADA_TN), lambda i: (0, i)),
        ],
        out_specs=pl.BlockSpec((bsz, ADA_TN), lambda i: (0, i)),
        out_shape=jax.ShapeDtypeStruct((bsz, n), F32),
        compiler_params=pltpu.CompilerParams(dimension_semantics=("arbitrary",)),
        name="ada_mod",
    )(c, w_ada, b_ada.reshape(1, n))


def _rope_rows(p, cos, sin):
    x1, x2 = p[0:ROT_HALF], p[ROT_HALF:ROT_DIM]
    return jnp.concatenate([x1 * cos - x2 * sin, x2 * cos + x1 * sin, p[ROT_DIM:]], axis=0)


def _dot_nt(a, b):
    return lax.dot_general(a, b, (((1,), (1,)), ((), ())), preferred_element_type=F32)


def _in_proj_kernel(x_ref, mod_ref, gpre_ref, wuv_ref, wtm_ref, wti_hi_ref, wti_lo_ref, cos_ref, sin_ref,
                    gv_ref, wsp_ref, bsp_ref, gosgu_ref,
                    ya_ref, qt_ref, k_ref, vt_ref, qi_ref, ki_ref, wt_ref):
    tm = x_ref.shape[1]
    d_sgu = SGU_GROUPS * SGU_GROUP_DIM
    x = x_ref[0]
    sh1 = mod_ref[0, 0:1, :]
    sc1 = mod_ref[0, 1:2, :]
    h = _rms(x, gpre_ref[...]) * (1.0 + sc1) + sh1
    h_hi32, h_lo32 = _split_bf16(h)
    h_hi, h_lo = h_hi32.astype(BF16), h_lo32.astype(BF16)
    cos, sin = cos_ref[0], sin_ref[0]

    puv = jnp.dot(h_hi, wuv_ref[...], preferred_element_type=F32)
    zu = jax.nn.gelu(puv[:, :d_sgu])
    zv = jax.nn.gelu(puv[:, d_sgu:])
    row = lax.broadcasted_iota(jnp.int32, (CHUNK, CHUNK), 0)
    col = lax.broadcasted_iota(jnp.int32, (CHUNK, CHUNK), 1)
    ya_groups = []
    for g in range(SGU_GROUPS):
        sl = slice(g * SGU_GROUP_DIM, (g + 1) * SGU_GROUP_DIM)
        vn = _rms(zv[:, sl], gv_ref[:, sl]).astype(BF16)
        wm = jnp.where(col <= row, wsp_ref[g], 0.0).astype(BF16)
        parts = []
        for ci in range(tm // CHUNK):
            rs = slice(ci * CHUNK, (ci + 1) * CHUNK)
            mixed = jnp.dot(wm, vn[rs], preferred_element_type=F32) + bsp_ref[g]
            parts.append(zu[rs, sl] * mixed)
        ya_groups.append(jnp.concatenate(parts, axis=0))
    ya = jnp.concatenate(ya_groups, axis=1)
    ya_ref[0] = _rms(ya, gosgu_ref[...]).astype(BF16)

    ptm = _dot_nt(wtm_ref[...], h_hi)
    nq = N_HEADS * HEAD_DIM
    nkv = N_KV_HEADS * HEAD_DIM
    for hh in range(N_HEADS):
        rs = slice(hh * HEAD_DIM, (hh + 1) * HEAD_DIM)
        qt_ref[0, rs, :] = (_rope_rows(ptm[rs], cos, sin) * (HEAD_DIM ** -0.5)).astype(BF16)
    kt = jnp.concatenate(
        [_rope_rows(ptm[nq + g * HEAD_DIM:nq + (g + 1) * HEAD_DIM], cos, sin) for g in range(N_KV_HEADS)], axis=0)
    k_ref[0] = kt.T.astype(BF16)
    vt = ptm[nq + nkv:nq + 2 * nkv].astype(BF16)
    for ci in range(tm // ATT_CH):
        vt_ref[0, ci] = vt[:, ci * ATT_CH:(ci + 1) * ATT_CH]

    pti = _dot_nt(wti_hi_ref[...], h_hi) + _dot_nt(wti_lo_ref[...], h_hi) + _dot_nt(wti_hi_ref[...], h_lo)
    nqi = IDX_HEADS * IDX_DIM
    for hh in range(IDX_HEADS):
        r = _rope_rows(pti[hh * IDX_DIM:(hh + 1) * IDX_DIM], cos, sin)
        r_hi, r_lo = _split_bf16(r)
        qi_ref[0, hh] = jnp.concatenate([r_hi, r_hi, r_lo, r_lo], axis=0).astype(BF16)
    kr = _rope_rows(pti[nqi:nqi + IDX_DIM], cos, sin)
    k_hi, k_lo = _split_bf16(kr)
    ki_ref[0] = jnp.concatenate([k_hi, k_lo, k_hi, k_lo], axis=0).T.astype(BF16)
    wt_ref[0] = pti[nqi + IDX_DIM:nqi + IDX_DIM + IDX_HEADS] * ((IDX_HEADS ** -0.5) * (IDX_DIM ** -0.5))


def _in_proj(x, mod3, g_pre, w_uv, wt_main, wt_idx_hi, wt_idx_lo, cos_t, sin_t, g_v, w_sp, b_sp, g_osgu):
    bsz, s, d = x.shape
    tm = PROJ_TM
    d_sgu = SGU_GROUPS * SGU_GROUP_DIM
    nq = N_HEADS * HEAD_DIM
    nkv = N_KV_HEADS * HEAD_DIM
    const2 = lambda b, i: (0, 0)
    const3 = lambda b, i: (0, 0, 0)
    return pl.pallas_call(
        _in_proj_kernel,
        grid=(bsz, s // tm),
        in_specs=[
            pl.BlockSpec((1, tm, d), lambda b, i: (b, i, 0)),
            pl.BlockSpec((1, 6, d), lambda b, i: (b, 0, 0)),
            pl.BlockSpec((1, d), const2),
            pl.BlockSpec(w_uv.shape, const2),
            pl.BlockSpec(wt_main.shape, const2),
            pl.BlockSpec(wt_idx_hi.shape, const2),
            pl.BlockSpec(wt_idx_lo.shape, const2),
            pl.BlockSpec((1, ROT_HALF, tm), lambda b, i: (b, 0, i)),
            pl.BlockSpec((1, ROT_HALF, tm), lambda b, i: (b, 0, i)),
            pl.BlockSpec((1, d_sgu), const2),
            pl.BlockSpec(w_sp.shape, const3),
            pl.BlockSpec(b_sp.shape, const3),
            pl.BlockSpec((1, d_sgu), const2),
        ],
        out_specs=[
            pl.BlockSpec((1, tm, d_sgu), lambda b, i: (b, i, 0)),
            pl.BlockSpec((1, nq, tm), lambda b, i: (b, 0, i)),
            pl.BlockSpec((1, tm, nkv), lambda b, i: (b, i, 0)),
            pl.BlockSpec((1, tm // ATT_CH, nkv, ATT_CH), lambda b, i: (b, i, 0, 0)),
            pl.BlockSpec((1, IDX_HEADS, IDX_K, tm), lambda b, i: (b, 0, 0, i)),
            pl.BlockSpec((1, tm, IDX_K), lambda b, i: (b, i, 0)),
            pl.BlockSpec((1, IDX_HEADS, tm), lambda b, i: (b, 0, i)),
        ],
        out_shape=[
            jax.ShapeDtypeStruct((bsz, s, d_sgu), BF16),
            jax.ShapeDtypeStruct((bsz, nq, s), BF16),
            jax.ShapeDtypeStruct((bsz, s, nkv), BF16),
            jax.ShapeDtypeStruct((bsz, s // ATT_CH, nkv, ATT_CH), BF16),
            jax.ShapeDtypeStruct((bsz, IDX_HEADS, IDX_K, s), BF16),
            jax.ShapeDtypeStruct((bsz, s, IDX_K), BF16),
            jax.ShapeDtypeStruct((bsz, IDX_HEADS, s), F32),
        ],
        compiler_params=pltpu.CompilerParams(
            dimension_semantics=("arbitrary", "arbitrary"), vmem_limit_bytes=VMEM_LIMIT_BYTES),
        name="in_proj",
    )(x, mod3, g_pre, w_uv, wt_main, wt_idx_hi, wt_idx_lo, cos_t, sin_t, g_v, w_sp, b_sp, g_osgu)


def _attn_kernel(qi_ref, w_ref, ki_ref, q_ref, k_ref, vt_ref, g_ref, o_ref,
                 sc_ref, qpad_ref, m_ref, acc_ref, *, topk):
    tq, ch = ATT_TQ, ATT_CH
    j = pl.program_id(1)
    nch = j + 1
    q_pos = j * tq + lax.broadcasted_iota(jnp.int32, (1, tq), 1)
    row_iota = lax.broadcasted_iota(jnp.int32, (ch, tq), 0)

    def score_chunk(kc, carry):
        mn, mx = carry
        kic = ki_ref[0, pl.ds(pl.multiple_of(kc * ch, ch), ch), :]
        acc = jnp.zeros((ch, tq), F32)
        for hh in range(IDX_HEADS):
            dd = jnp.dot(kic, qi_ref[0, hh], preferred_element_type=F32)
            acc = acc + w_ref[0, hh:hh + 1, :] * jnp.maximum(dd, 0.0)
        causal = (row_iota + kc * ch) <= q_pos
        sc_ref[kc] = jnp.where(causal, acc, -jnp.inf)
        mn = jnp.minimum(mn, jnp.min(jnp.where(causal, acc, jnp.inf), axis=0, keepdims=True))
        mx = jnp.maximum(mx, jnp.max(jnp.where(causal, acc, -jnp.inf), axis=0, keepdims=True))
        return mn, mx

    mn, mx = lax.fori_loop(0, nch, score_chunk,
                           (jnp.full((1, tq), jnp.inf, F32), jnp.full((1, tq), -jnp.inf, F32)))

    def count(pred):
        def body(kc, c):
            ind = jnp.where(pred(sc_ref[kc]), 1.0, 0.0)
            return c + jnp.sum(ind.reshape(ch // 8, 8, tq), axis=0)
        c8 = lax.fori_loop(0, nch, body, jnp.zeros((8, tq), F32))
        return jnp.sum(c8, axis=0, keepdims=True)

    kk = jnp.minimum(float(topk), (q_pos + 1).astype(F32))
    done0 = count(lambda s: s >= mx) >= kk

    def bisect_step(state):
        lo, hi, thr, done = state
        mid = 0.5 * lo + 0.5 * hi
        valid = (mid > lo) & (mid < hi)
        c = count(lambda s: s >= mid)
        ge = c >= kk
        thr = jnp.where(done, thr, jnp.where(valid, mid, lo))
        lo = jnp.where(valid & ge, mid, lo)
        hi = jnp.where(valid & jnp.logical_not(ge), mid, hi)
        done = done | jnp.logical_not(valid) | (c == kk)
        return lo, hi, thr, done

    def bisect_cond(carry):
        return carry[1] > 0.0

    def bisect_body(carry):
        state, _ = carry
        lo, hi, thr, done_f = state
        st = (lo, hi, thr, done_f > 0.5)
        for _ in range(BISECT_UNROLL):
            st = bisect_step(st)
        lo, hi, thr, done = st
        done_f = jnp.where(done, 1.0, 0.0)
        return (lo, hi, thr, done_f), jnp.max(1.0 - done_f)

    done0_f = jnp.where(done0, 1.0, 0.0)
    (_, _, thr, _), _ = lax.while_loop(
        bisect_cond, bisect_body, ((mn, mx, mx, done0_f), jnp.max(1.0 - done0_f)))

    n_gt = count(lambda s: s > thr)
    n_ge = count(lambda s: s >= thr)
    need = kk - n_gt
    tie = jnp.max(jnp.where(n_ge > kk, 1.0, 0.0)) > 0.0

    @pl.when(jnp.logical_not(tie))
    def _():
        def body(kc, _):
            sc_ref[kc] = jnp.where(sc_ref[kc] >= thr, 0.0, NEG)
            return 0
        lax.fori_loop(0, nch, body, 0)

    @pl.when(tie)
    def _():
        r = lax.broadcasted_iota(jnp.int32, (ch, ch), 0)
        c = lax.broadcasted_iota(jnp.int32, (ch, ch), 1)
        below = jnp.where(c < r, 1.0, 0.0).astype(BF16)

        def body(kc, seen):
            s = sc_ref[kc]
            eq = jnp.where(s == thr, 1.0, 0.0)
            rank = jnp.dot(below, eq.astype(BF16), preferred_element_type=F32) + seen
            sel = (s > thr) | ((s == thr) & (rank < need))
            sc_ref[kc] = jnp.where(sel, 0.0, NEG)
            return seen + jnp.sum(eq, axis=0, keepdims=True)
        lax.fori_loop(0, nch, body, jnp.zeros((1, tq), F32))

    zq = jnp.zeros((HEAD_DIM, tq), BF16)
    for hh in range(N_HEADS):
        qh = q_ref[0, hh * HEAD_DIM:(hh + 1) * HEAD_DIM, :]
        qpad_ref[hh] = jnp.concatenate([qh, zq] if hh < GQA_GROUP else [zq, qh], axis=0)
    m_ref[...] = jnp.full(m_ref.shape, NEG, F32)
    acc_ref[...] = jnp.zeros(acc_ref.shape, F32)
    ones = jnp.ones((ONES_ROWS, ch), BF16)

    def attn_chunk(kc, _):
        kcs = k_ref[0, pl.ds(pl.multiple_of(kc * ch, ch), ch), :]
        vt = vt_ref[0, kc]
        bias = sc_ref[kc]
        for g in range(N_KV_HEADS):
            vaug = jnp.concatenate([vt[g * HEAD_DIM:(g + 1) * HEAD_DIM], ones], axis=0)
            for hh in range(g * GQA_GROUP, (g + 1) * GQA_GROUP):
                s = jnp.dot(kcs, qpad_ref[hh], preferred_element_type=F32) + bias
                m_old = m_ref[hh]
                m_new = jnp.maximum(m_old, jnp.max(s, axis=0, keepdims=True))
                alpha = jnp.exp(m_old - m_new)
                p = jnp.exp(s - m_new).astype(BF16)
                acc_ref[hh] = alpha * acc_ref[hh] + jnp.dot(vaug, p, preferred_element_type=F32)
                m_ref[hh] = m_new
        return 0

    lax.fori_loop(0, nch, attn_chunk, 0)

    outs = []
    for hh in range(N_HEADS):
        a = acc_ref[hh]
        outs.append(a[0:HEAD_DIM] / a[HEAD_DIM:HEAD_DIM + 1])
    y = jnp.concatenate(outs, axis=0)
    yn = y * lax.rsqrt(jnp.mean(y * y, axis=0, keepdims=True) + EPS) * g_ref[...]
    o_ref[0] = yn.T.astype(BF16)


def _dsa_attn(qi3t, wt, ki3, qt, kk, vt, g_attn_b, topk):
    bsz, s, _ = ki3.shape
    tq, ch = ATT_TQ, ATT_CH
    nq = N_HEADS * HEAD_DIM
    nkv = N_KV_HEADS * HEAD_DIM
    return pl.pallas_call(
        functools.partial(_attn_kernel, topk=topk),
        grid=(bsz, s // tq),
        in_specs=[
            pl.BlockSpec((1, IDX_HEADS, IDX_K, tq), lambda b, j: (b, 0, 0, j)),
            pl.BlockSpec((1, IDX_HEADS, tq), lambda b, j: (b, 0, j)),
            pl.BlockSpec((1, s, IDX_K), lambda b, j: (b, 0, 0)),
            pl.BlockSpec((1, nq, tq), lambda b, j: (b, 0, j)),
            pl.BlockSpec((1, s, nkv), lambda b, j: (b, 0, 0)),
            pl.BlockSpec((1, s // ch, nkv, ch), lambda b, j: (b, 0, 0, 0)),
            pl.BlockSpec((nq, tq), lambda b, j: (0, 0)),
        ],
        out_specs=pl.BlockSpec((1, tq, nq), lambda b, j: (b, j, 0)),
        out_shape=jax.ShapeDtypeStruct((bsz, s, nq), BF16),
        scratch_shapes=[
            pltpu.VMEM((s // ch, ch, tq), F32),
            pltpu.VMEM((N_HEADS, 2 * HEAD_DIM, tq), BF16),
            pltpu.VMEM((N_HEADS, 1, tq), F32),
            pltpu.VMEM((N_HEADS, HEAD_DIM + ONES_ROWS, tq), F32),
        ],
        compiler_params=pltpu.CompilerParams(
            dimension_semantics=("arbitrary", "arbitrary"), vmem_limit_bytes=VMEM_LIMIT_BYTES),
        name="dsa_attn",
    )(qi3t, wt, ki3, qt, kk, vt, g_attn_b)


def _out_ffn_kernel(x_ref, ya_ref, yb_ref, mod_ref, wo_ref, gpm_ref, gpf_ref, w1_ref, w2_ref, gpo_ref, o_ref):
    d_a = ya_ref.shape[2]
    x = x_ref[0]
    g1 = mod_ref[0, 2:3, :]
    sh2 = mod_ref[0, 3:4, :]
    sc2 = mod_ref[0, 4:5, :]
    g2 = mod_ref[0, 5:6, :]
    o = jnp.dot(ya_ref[0], wo_ref[0:d_a, :], preferred_element_type=F32)
    o += jnp.dot(yb_ref[0], wo_ref[d_a:, :], preferred_element_type=F32)
    x1 = x + g1 * _rms(o, gpm_ref[...])
    h2 = (_rms(x1, gpf_ref[...]) * (1.0 + sc2) + sh2).astype(BF16)
    f = jnp.zeros(x.shape, F32)
    for c in range(w1_ref.shape[1] // FFN_TF):
        a = jnp.dot(h2, w1_ref[:, c * FFN_TF:(c + 1) * FFN_TF], preferred_element_type=F32)
        a = jnp.square(jnp.maximum(a, 0.0)).astype(BF16)
        f += jnp.dot(a, w2_ref[c * FFN_TF:(c + 1) * FFN_TF, :], preferred_element_type=F32)
    o_ref[0] = x1 + g2 * _rms(f, gpo_ref[...])


def _out_ffn(x, ya, yb, mod3, w_out, g_pm, g_pf, w1, w2, g_po):
    bsz, s, d = x.shape
    tm = FFN_TM
    d_a = ya.shape[2]
    const2 = lambda b, i: (0, 0)
    resident = functools.partial(pl.BlockSpec, index_map=const2, pipeline_mode=pl.Buffered(1))
    return pl.pallas_call(
        _out_ffn_kernel,
        grid=(bsz, s // tm),
        in_specs=[
            pl.BlockSpec((1, tm, d), lambda b, i: (b, i, 0)),
            pl.BlockSpec((1, tm, d_a), lambda b, i: (b, i, 0)),
            pl.BlockSpec((1, tm, yb.shape[2]), lambda b, i: (b, i, 0)),
            pl.BlockSpec((1, 6, d), lambda b, i: (b, 0, 0)),
            resident(w_out.shape),
            pl.BlockSpec((1, d), const2),
            pl.BlockSpec((1, d), const2),
            resident(w1.shape),
            resident(w2.shape),
            pl.BlockSpec((1, d), const2),
        ],
        out_specs=pl.BlockSpec((1, tm, d), lambda b, i: (b, i, 0)),
        out_shape=jax.ShapeDtypeStruct((bsz, s, d), F32),
        compiler_params=pltpu.CompilerParams(
            dimension_semantics=("arbitrary", "arbitrary"), vmem_limit_bytes=VMEM_LIMIT_BYTES),
        name="out_ffn",
    )(x, ya, yb, mod3, w_out, g_pm, g_pf, w1, w2, g_po)


def _layer(x, mod3, cos_t, sin_t, g_pre_mix, w_in, g_sgu_v, w_spatial, b_spatial, g_out_sgu, g_out_attn,
           w_out, g_post_mix, g_pre_ffn, w_ff1, w_ff2, g_post_ffn):
    bsz, s, d = x.shape
    d_sgu = SGU_GROUPS * SGU_GROUP_DIM
    nq = N_HEADS * HEAD_DIM
    nkv = N_KV_HEADS * HEAD_DIM
    nqi = IDX_HEADS * IDX_DIM
    topk = min(TOPK_MAX, s // 4)

    o_q = 2 * d_sgu
    o_qi = o_q + nq + 2 * nkv
    w_uv = w_in[:, :o_q].astype(BF16)
    wt_main = w_in[:, o_q:o_qi].T.astype(BF16)
    n_idx = nqi + IDX_DIM + IDX_HEADS
    wt_idx = jnp.pad(w_in[:, o_qi:o_qi + n_idx].T, ((0, (-n_idx) % 16), (0, 0)))
    wt_idx_hi = wt_idx.astype(BF16)
    wt_idx_lo = (wt_idx - wt_idx_hi.astype(F32)).astype(BF16)
    b_sp = jnp.broadcast_to(b_spatial[:, :, None], (SGU_GROUPS, CHUNK, SGU_GROUP_DIM))

    ya, qt, kk, vt, qi3t, ki3, wt = _in_proj(
        x, mod3, g_pre_mix.reshape(1, d), w_uv, wt_main, wt_idx_hi, wt_idx_lo, cos_t, sin_t,
        g_sgu_v.reshape(1, d_sgu), w_spatial, b_sp, g_out_sgu.reshape(1, d_sgu))

    g_attn_b = jnp.broadcast_to(g_out_attn[:, None], (nq, ATT_TQ))
    yb = _dsa_attn(qi3t, wt, ki3, qt, kk, vt, g_attn_b, topk)

    return _out_ffn(x, ya, yb, mod3, w_out.astype(BF16), g_post_mix.reshape(1, d), g_pre_ffn.reshape(1, d),
                    w_ff1.astype(BF16), w_ff2.astype(BF16), g_post_ffn.reshape(1, d))


def kernel(x, c, positions, w_ada, b_ada, g_pre_mix, w_in, g_sgu_v, w_spatial, b_spatial, g_out_sgu, g_out_attn,
           w_out, g_post_mix, g_pre_ffn, w_ff1, w_ff2, g_post_ffn):
    bsz, s, d = x.shape
    inv_freq = ROPE_THETA ** (-jnp.arange(0, ROT_DIM, 2, dtype=F32) / ROT_DIM)
    ang = positions.astype(F32)[:, None, :] * inv_freq[None, :, None]
    cos_t, sin_t = jnp.cos(ang), jnp.sin(ang)
    for l in range(w_ada.shape[0]):
        mod3 = _ada_mod(c, w_ada[l], b_ada[l]).reshape(bsz, 6, d)
        x = _layer(x, mod3, cos_t, sin_t, g_pre_mix[l], w_in[l], g_sgu_v[l], w_spatial[l], b_spatial[l],
                   g_out_sgu[l], g_out_attn[l], w_out[l], g_post_mix[l], g_pre_ffn[l], w_ff1[l], w_ff2[l],
                   g_post_ffn[l])
    return x
```

```python
import functools

import jax
import jax.numpy as jnp
from jax import lax
from jax.experimental import pallas as pl
from jax.experimental.pallas import tpu as pltpu

F32 = jnp.float32
BF16 = jnp.bfloat16

SGU_GROUPS = 4
SGU_GROUP_DIM = 128
CHUNK = 128
N_HEADS = 8
HEAD_DIM = 64
N_KV_HEADS = 2
GQA_GROUP = N_HEADS // N_KV_HEADS
ROT_DIM = 16
ROT_HALF = ROT_DIM // 2
ROPE_THETA = 500000.0
IDX_HEADS = 8
IDX_DIM = 64
TOPK_MAX = 256
EPS = 1e-6

ADA_TN = 1024
PROJ_TM = 512
ATT_TQ = 256
ATT_CH = 256
FFN_TM = 512
FFN_TF = 1024
BISECT_UNROLL = 4
IDX_K = 4 * IDX_DIM
ONES_ROWS = 16
NEG = -1e30
VMEM_LIMIT_BYTES = 56 * 1024 * 1024
Q_SCALE = (HEAD_DIM ** -0.5) * 1.4426950408889634


def _split_bf16(a):
    hi = a.astype(BF16).astype(F32)
    lo = (a - hi).astype(BF16).astype(F32)
    return hi, lo


def _rms(a, g):
    return a * lax.rsqrt(jnp.mean(a * a, axis=-1, keepdims=True) + EPS) * g


def _ada_kernel(c_ref, w_ref, b_ref, o_ref):
    c = c_ref[...]
    ca = c * (1.0 / (1.0 + jnp.exp(-c)))
    c_hi, c_lo = _split_bf16(ca)
    w_hi, w_lo = _split_bf16(w_ref[...])
    c_hi, c_lo, w_hi, w_lo = (t.astype(BF16) for t in (c_hi, c_lo, w_hi, w_lo))
    acc = jnp.dot(c_hi, w_hi, preferred_element_type=F32)
    acc += jnp.dot(c_hi, w_lo, preferred_element_type=F32)
    acc += jnp.dot(c_lo, w_hi, preferred_element_type=F32)
    o_ref[...] = acc + b_ref[...]


def _ada_mod(c, w_ada, b_ada):
    bsz, d = c.shape
    n = w_ada.shape[1]
    return pl.pallas_call(
        _ada_kernel,
        grid=(n // ADA_TN,),
        in_specs=[
            pl.BlockSpec((bsz, d), lambda i: (0, 0)),
            pl.BlockSpec((d, ADA_TN), lambda i: (0, i)),
            pl.BlockSpec((1, ADA_TN), lambda i: (0, i)),
        ],
        out_specs=pl.BlockSpec((bsz, ADA_TN), lambda i: (0, i)),
        out_shape=jax.ShapeDtypeStruct((bsz, n), F32),
        compiler_params=pltpu.CompilerParams(dimension_semantics=("arbitrary",)),
        name="ada_mod",
    )(c, w_ada, b_ada.reshape(1, n))


def _rope_rows(p, cos, sin):
    x1, x2 = p[0:ROT_HALF], p[ROT_HALF:ROT_DIM]
    return jnp.concatenate([x1 * cos - x2 * sin, x2 * cos + x1 * sin, p[ROT_DIM:]], axis=0)


def _dot_nt(a, b):
    return lax.dot_general(a, b, (((1,), (1,)), ((), ())), preferred_element_type=F32)


def _in_proj_kernel(x_ref, mod_ref, gpre_ref, wuv_ref, wtm_ref, wti_hi_ref, wti_lo_ref, cos_ref, sin_ref,
                    gv_ref, wsp_ref, bsp_ref, gosgu_ref,
                    ya_ref, qt_ref, k_ref, vt_ref, qi_ref, ki_ref, wt_ref):
    tm = x_ref.shape[1]
    d_sgu = SGU_GROUPS * SGU_GROUP_DIM
    x = x_ref[0]
    sh1 = mod_ref[0, 0:1, :]
    sc1 = mod_ref[0, 1:2, :]
    h = _rms(x, gpre_ref[...]) * (1.0 + sc1) + sh1
    h_hi32, h_lo32 = _split_bf16(h)
    h_hi, h_lo = h_hi32.astype(BF16), h_lo32.astype(BF16)
    cos, sin = cos_ref[0], sin_ref[0]

    puv = jnp.dot(h_hi, wuv_ref[...], preferred_element_type=F32)
    zu = jax.nn.gelu(puv[:, :d_sgu])
    zv = jax.nn.gelu(puv[:, d_sgu:])
    row = lax.broadcasted_iota(jnp.int32, (CHUNK, CHUNK), 0)
    col = lax.broadcasted_iota(jnp.int32, (CHUNK, CHUNK), 1)
    ya_groups = []
    for g in range(SGU_GROUPS):
        sl = slice(g * SGU_GROUP_DIM, (g + 1) * SGU_GROUP_DIM)
        vn = _rms(zv[:, sl], gv_ref[:, sl]).astype(BF16)
        wm = jnp.where(col <= row, wsp_ref[g], 0.0).astype(BF16)
        parts = []
        for ci in range(tm // CHUNK):
            rs = slice(ci * CHUNK, (ci + 1) * CHUNK)
            mixed = jnp.dot(wm, vn[rs], preferred_element_type=F32) + bsp_ref[g]
            parts.append(zu[rs, sl] * mixed)
        ya_groups.append(jnp.concatenate(parts, axis=0))
    ya = jnp.concatenate(ya_groups, axis=1)
    ya_ref[0] = _rms(ya, gosgu_ref[...]).astype(BF16)

    ptm = _dot_nt(wtm_ref[...], h_hi)
    nq = N_HEADS * HEAD_DIM
    nkv = N_KV_HEADS * HEAD_DIM
    for hh in range(N_HEADS):
        rs = slice(hh * HEAD_DIM, (hh + 1) * HEAD_DIM)
        qt_ref[0, rs, :] = (_rope_rows(ptm[rs], cos, sin) * Q_SCALE).astype(BF16)
    kt = jnp.concatenate(
        [_rope_rows(ptm[nq + g * HEAD_DIM:nq + (g + 1) * HEAD_DIM], cos, sin) for g in range(N_KV_HEADS)], axis=0)
    k_ref[0] = kt.T.astype(BF16)
    vt = ptm[nq + nkv:nq + 2 * nkv].astype(BF16)
    for ci in range(tm // ATT_CH):
        vt_ref[0, ci] = vt[:, ci * ATT_CH:(ci + 1) * ATT_CH]

    pti = _dot_nt(wti_hi_ref[...], h_hi) + _dot_nt(wti_lo_ref[...], h_hi) + _dot_nt(wti_hi_ref[...], h_lo)
    nqi = IDX_HEADS * IDX_DIM
    for hh in range(IDX_HEADS):
        r = _rope_rows(pti[hh * IDX_DIM:(hh + 1) * IDX_DIM], cos, sin)
        r_hi, r_lo = _split_bf16(r)
        qi_ref[0, hh] = jnp.concatenate([r_hi, r_hi, r_lo, r_lo], axis=0).astype(BF16)
    kr = _rope_rows(pti[nqi:nqi + IDX_DIM], cos, sin)
    k_hi, k_lo = _split_bf16(kr)
    ki_ref[0] = jnp.concatenate([k_hi, k_lo, k_hi, k_lo], axis=0).T.astype(BF16)
    wt_ref[0] = pti[nqi + IDX_DIM:nqi + IDX_DIM + IDX_HEADS] * ((IDX_HEADS ** -0.5) * (IDX_DIM ** -0.5))


def _in_proj(x, mod3, g_pre, w_uv, wt_main, wt_idx_hi, wt_idx_lo, cos_t, sin_t, g_v, w_sp, b_sp, g_osgu):
    bsz, s, d = x.shape
    tm = PROJ_TM
    d_sgu = SGU_GROUPS * SGU_GROUP_DIM
    nq = N_HEADS * HEAD_DIM
    nkv = N_KV_HEADS * HEAD_DIM
    const2 = lambda b, i: (0, 0)
    const3 = lambda b, i: (0, 0, 0)
    return pl.pallas_call(
        _in_proj_kernel,
        grid=(bsz, s // tm),
        in_specs=[
            pl.BlockSpec((1, tm, d), lambda b, i: (b, i, 0)),
            pl.BlockSpec((1, 6, d), lambda b, i: (b, 0, 0)),
            pl.BlockSpec((1, d), const2),
            pl.BlockSpec(w_uv.shape, const2),
            pl.BlockSpec(wt_main.shape, const2),
            pl.BlockSpec(wt_idx_hi.shape, const2),
            pl.BlockSpec(wt_idx_lo.shape, const2),
            pl.BlockSpec((1, ROT_HALF, tm), lambda b, i: (b, 0, i)),
            pl.BlockSpec((1, ROT_HALF, tm), lambda b, i: (b, 0, i)),
            pl.BlockSpec((1, d_sgu), const2),
            pl.BlockSpec(w_sp.shape, const3),
            pl.BlockSpec(b_sp.shape, const3),
            pl.BlockSpec((1, d_sgu), const2),
        ],
        out_specs=[
            pl.BlockSpec((1, tm, d_sgu), lambda b, i: (b, i, 0)),
            pl.BlockSpec((1, nq, tm), lambda b, i: (b, 0, i)),
            pl.BlockSpec((1, tm, nkv), lambda b, i: (b, i, 0)),
            pl.BlockSpec((1, tm // ATT_CH, nkv, ATT_CH), lambda b, i: (b, i, 0, 0)),
            pl.BlockSpec((1, IDX_HEADS, IDX_K, tm), lambda b, i: (b, 0, 0, i)),
            pl.BlockSpec((1, tm, IDX_K), lambda b, i: (b, i, 0)),
            pl.BlockSpec((1, IDX_HEADS, tm), lambda b, i: (b, 0, i)),
        ],
        out_shape=[
            jax.ShapeDtypeStruct((bsz, s, d_sgu), BF16),
            jax.ShapeDtypeStruct((bsz, nq, s), BF16),
            jax.ShapeDtypeStruct((bsz, s, nkv), BF16),
            jax.ShapeDtypeStruct((bsz, s // ATT_CH, nkv, ATT_CH), BF16),
            jax.ShapeDtypeStruct((bsz, IDX_HEADS, IDX_K, s), BF16),
            jax.ShapeDtypeStruct((bsz, s, IDX_K), BF16),
            jax.ShapeDtypeStruct((bsz, IDX_HEADS, s), F32),
        ],
        compiler_params=pltpu.CompilerParams(
            dimension_semantics=("arbitrary", "arbitrary"), vmem_limit_bytes=VMEM_LIMIT_BYTES),
        name="in_proj",
    )(x, mod3, g_pre, w_uv, wt_main, wt_idx_hi, wt_idx_lo, cos_t, sin_t, g_v, w_sp, b_sp, g_osgu)


def _attn_kernel(qi_ref, w_ref, ki_ref, q_ref, k_ref, vt_ref, g_ref, o_ref,
                 sc_ref, lg_ref, qpad_ref, m_ref, acc_ref, *, topk):
    tq, ch = ATT_TQ, ATT_CH
    j = pl.program_id(1)
    nch = j + 1
    q_pos = j * tq + lax.broadcasted_iota(jnp.int32, (1, tq), 1)
    row_iota = lax.broadcasted_iota(jnp.int32, (ch, tq), 0)

    def score_chunk(kc, carry):
        mn, mx = carry
        kic = ki_ref[0, pl.ds(pl.multiple_of(kc * ch, ch), ch), :]
        acc = jnp.zeros((ch, tq), F32)
        for hh in range(IDX_HEADS):
            dd = jnp.dot(kic, qi_ref[0, hh], preferred_element_type=F32)
            acc = acc + w_ref[0, hh:hh + 1, :] * jnp.maximum(dd, 0.0)
        causal = (row_iota + kc * ch) <= q_pos
        sc_ref[kc] = jnp.where(causal, acc, -jnp.inf)
        mn = jnp.minimum(mn, jnp.min(jnp.where(causal, acc, jnp.inf), axis=0, keepdims=True))
        mx = jnp.maximum(mx, jnp.max(jnp.where(causal, acc, -jnp.inf), axis=0, keepdims=True))
        return mn, mx

    mn, mx = lax.fori_loop(0, nch, score_chunk,
                           (jnp.full((1, tq), jnp.inf, F32), jnp.full((1, tq), -jnp.inf, F32)))

    def count(pred):
        def body(kc, c):
            ind = jnp.where(pred(sc_ref[kc]), 1.0, 0.0)
            return c + jnp.sum(ind.reshape(ch // 8, 8, tq), axis=0)
        c8 = lax.fori_loop(0, nch, body, jnp.zeros((8, tq), F32))
        return jnp.sum(c8, axis=0, keepdims=True)

    n_causal = (q_pos + 1).astype(F32)
    kk = jnp.minimum(float(topk), n_causal)
    zero = jnp.zeros((1, tq), F32)
    c0_gt = count(lambda s: s > zero)
    c0_ge = count(lambda s: s >= zero)
    c_mx = count(lambda s: s >= mx)
    at_max = c_mx >= kk
    at_zero = (c0_gt < kk) & (c0_ge >= kk)
    take_all = n_causal == kk
    pos = c0_gt >= kk
    lo0 = jnp.where(pos, jnp.maximum(mn, 0.0), mn)
    clo0 = jnp.where(pos & (mn <= 0.0), c0_ge, n_causal)
    hi0 = jnp.where(pos, mx, jnp.minimum(mx, 0.0))
    thr0 = jnp.where(at_max, mx, jnp.where(at_zero, 0.0, mn))
    tied0 = (at_max & (c_mx > kk)) | (jnp.logical_not(at_max) & at_zero & (c0_ge > kk))
    done0 = at_max | at_zero | take_all

    def bisect_step(state):
        lo, clo, hi, thr, done, tied = state
        mid = 0.5 * lo + 0.5 * hi
        valid = (mid > lo) & (mid < hi)
        c = count(lambda s: s >= mid)
        move_lo = valid & (c >= kk)
        thr = jnp.where(done, thr, jnp.where(valid, mid, lo))
        tied = tied | (jnp.logical_not(done) & jnp.logical_not(valid) & (clo > kk))
        done = done | jnp.logical_not(valid) | (c == kk)
        hi = jnp.where(valid & (c < kk), mid, hi)
        lo = jnp.where(move_lo, mid, lo)
        clo = jnp.where(move_lo, c, clo)
        return lo, clo, hi, thr, done, tied

    def as_f(m):
        return jnp.where(m, 1.0, 0.0)

    def bisect_body(carry):
        (lo, clo, hi, thr, done_f, tied_f), _ = carry
        st = (lo, clo, hi, thr, done_f > 0.5, tied_f > 0.5)
        for _ in range(BISECT_UNROLL):
            st = bisect_step(st)
        lo, clo, hi, thr, done, tied = st
        return (lo, clo, hi, thr, as_f(done), as_f(tied)), jnp.max(1.0 - as_f(done))

    (_, _, _, thr, _, tied_f), _ = lax.while_loop(
        lambda carry: carry[1] > 0.0, bisect_body,
        ((lo0, clo0, hi0, thr0, as_f(done0), as_f(tied0)), jnp.max(1.0 - as_f(done0))))

    tie = jnp.max(tied_f) > 0.0

    @pl.when(jnp.logical_not(tie))
    def _():
        def body(kc, _):
            sc_ref[kc] = jnp.where(sc_ref[kc] >= thr, 0.0, NEG)
            return 0
        lax.fori_loop(0, nch, body, 0)

    @pl.when(tie)
    def _():
        r = lax.broadcasted_iota(jnp.int32, (ch, ch), 0)
        c = lax.broadcasted_iota(jnp.int32, (ch, ch), 1)
        below = jnp.where(c < r, 1.0, 0.0).astype(BF16)
        need = kk - count(lambda s: s > thr)

        def body(kc, seen):
            s = sc_ref[kc]
            eq = jnp.where(s == thr, 1.0, 0.0)
            rank = jnp.dot(below, eq.astype(BF16), preferred_element_type=F32) + seen
            sel = (s > thr) | ((s == thr) & (rank < need))
            sc_ref[kc] = jnp.where(sel, 0.0, NEG)
            return seen + jnp.sum(eq, axis=0, keepdims=True)
        lax.fori_loop(0, nch, body, jnp.zeros((1, tq), F32))

    zq = jnp.zeros((HEAD_DIM, tq), BF16)
    for hh in range(N_HEADS):
        qh = q_ref[0, hh * HEAD_DIM:(hh + 1) * HEAD_DIM, :]
        qpad_ref[hh] = jnp.concatenate([qh, zq] if hh < GQA_GROUP else [zq, qh], axis=0)

    def logits_chunk(kc, ms):
        kcs = k_ref[0, pl.ds(pl.multiple_of(kc * ch, ch), ch), :]
        bias = sc_ref[kc]
        out = []
        for hh in range(N_HEADS):
            s = jnp.dot(kcs, qpad_ref[hh], preferred_element_type=F32) + bias
            lg_ref[hh, kc] = s
            out.append(jnp.maximum(ms[hh], jnp.max(s.reshape(ch // 8, 8, tq), axis=0)))
        return tuple(out)

    ms = lax.fori_loop(0, nch, logits_chunk, tuple(jnp.full((8, tq), NEG, F32) for _ in range(N_HEADS)))
    for hh in range(N_HEADS):
        m_ref[hh] = jnp.max(ms[hh], axis=0, keepdims=True)
    acc_ref[...] = jnp.zeros(acc_ref.shape, F32)
    ones = jnp.ones((ONES_ROWS, ch), BF16)

    def pv_chunk(kc, _):
        vt = vt_ref[0, kc]
        for g in range(N_KV_HEADS):
            vaug = jnp.concatenate([vt[g * HEAD_DIM:(g + 1) * HEAD_DIM], ones], axis=0)
            for hh in range(g * GQA_GROUP, (g + 1) * GQA_GROUP):
                p = jnp.exp2(lg_ref[hh, kc] - m_ref[hh]).astype(BF16)
                acc_ref[hh] += jnp.dot(vaug, p, preferred_element_type=F32)
        return 0

    lax.fori_loop(0, nch, pv_chunk, 0)

    outs = []
    for hh in range(N_HEADS):
        a = acc_ref[hh]
        outs.append(a[0:HEAD_DIM] / a[HEAD_DIM:HEAD_DIM + 1])
    y = jnp.concatenate(outs, axis=0)
    yn = y * lax.rsqrt(jnp.mean(y * y, axis=0, keepdims=True) + EPS) * g_ref[...]
    o_ref[0] = yn.T.astype(BF16)


def _dsa_attn(qi3t, wt, ki3, qt, kk, vt, g_attn_b, topk):
    bsz, s, _ = ki3.shape
    tq, ch = ATT_TQ, ATT_CH
    nq = N_HEADS * HEAD_DIM
    nkv = N_KV_HEADS * HEAD_DIM
    return pl.pallas_call(
        functools.partial(_attn_kernel, topk=topk),
        grid=(bsz, s // tq),
        in_specs=[
            pl.BlockSpec((1, IDX_HEADS, IDX_K, tq), lambda b, j: (b, 0, 0, j)),
            pl.BlockSpec((1, IDX_HEADS, tq), lambda b, j: (b, 0, j)),
            pl.BlockSpec((1, s, IDX_K), lambda b, j: (b, 0, 0)),
            pl.BlockSpec((1, nq, tq), lambda b, j: (b, 0, j)),
            pl.BlockSpec((1, s, nkv), lambda b, j: (b, 0, 0)),
            pl.BlockSpec((1, s // ch, nkv, ch), lambda b, j: (b, 0, 0, 0)),
            pl.BlockSpec((nq, tq), lambda b, j: (0, 0)),
        ],
        out_specs=pl.BlockSpec((1, tq, nq), lambda b, j: (b, j, 0)),
        out_shape=jax.ShapeDtypeStruct((bsz, s, nq), BF16),
        scratch_shapes=[
            pltpu.VMEM((s // ch, ch, tq), F32),
            pltpu.VMEM((N_HEADS, s // ch, ch, tq), F32),
            pltpu.VMEM((N_HEADS, 2 * HEAD_DIM, tq), BF16),
            pltpu.VMEM((N_HEADS, 1, tq), F32),
            pltpu.VMEM((N_HEADS, HEAD_DIM + ONES_ROWS, tq), F32),
        ],
        compiler_params=pltpu.CompilerParams(
            dimension_semantics=("arbitrary", "arbitrary"), vmem_limit_bytes=VMEM_LIMIT_BYTES),
        name="dsa_attn",
    )(qi3t, wt, ki3, qt, kk, vt, g_attn_b)


def _out_ffn_kernel(x_ref, ya_ref, yb_ref, mod_ref, wo_ref, gpm_ref, gpf_ref, w1_ref, w2_ref, gpo_ref, o_ref):
    d_a = ya_ref.shape[2]
    x = x_ref[0]
    g1 = mod_ref[0, 2:3, :]
    sh2 = mod_ref[0, 3:4, :]
    sc2 = mod_ref[0, 4:5, :]
    g2 = mod_ref[0, 5:6, :]
    o = jnp.dot(ya_ref[0], wo_ref[0:d_a, :], preferred_element_type=F32)
    o += jnp.dot(yb_ref[0], wo_ref[d_a:, :], preferred_element_type=F32)
    x1 = x + g1 * _rms(o, gpm_ref[...])
    h2 = (_rms(x1, gpf_ref[...]) * (1.0 + sc2) + sh2).astype(BF16)
    f = jnp.zeros(x.shape, F32)
    for c in range(w1_ref.shape[1] // FFN_TF):
        a = jnp.dot(h2, w1_ref[:, c * FFN_TF:(c + 1) * FFN_TF], preferred_element_type=F32)
        a = jnp.square(jnp.maximum(a, 0.0)).astype(BF16)
        f += jnp.dot(a, w2_ref[c * FFN_TF:(c + 1) * FFN_TF, :], preferred_element_type=F32)
    o_ref[0] = x1 + g2 * _rms(f, gpo_ref[...])


def _out_ffn(x, ya, yb, mod3, w_out, g_pm, g_pf, w1, w2, g_po):
    bsz, s, d = x.shape
    tm = FFN_TM
    d_a = ya.shape[2]
    const2 = lambda b, i: (0, 0)
    resident = functools.partial(pl.BlockSpec, index_map=const2, pipeline_mode=pl.Buffered(1))
    return pl.pallas_call(
        _out_ffn_kernel,
        grid=(bsz, s // tm),
        in_specs=[
            pl.BlockSpec((1, tm, d), lambda b, i: (b, i, 0)),
            pl.BlockSpec((1, tm, d_a), lambda b, i: (b, i, 0)),
            pl.BlockSpec((1, tm, yb.shape[2]), lambda b, i: (b, i, 0)),
            pl.BlockSpec((1, 6, d), lambda b, i: (b, 0, 0)),
            resident(w_out.shape),
            pl.BlockSpec((1, d), const2),
            pl.BlockSpec((1, d), const2),
            resident(w1.shape),
            resident(w2.shape),
            pl.BlockSpec((1, d), const2),
        ],
        out_specs=pl.BlockSpec((1, tm, d), lambda b, i: (b, i, 0)),
        out_shape=jax.ShapeDtypeStruct((bsz, s, d), F32),
        compiler_params=pltpu.CompilerParams(
            dimension_semantics=("arbitrary", "arbitrary"), vmem_limit_bytes=VMEM_LIMIT_BYTES),
        name="out_ffn",
    )(x, ya, yb, mod3, w_out, g_pm, g_pf, w1, w2, g_po)


def _layer(x, mod3, cos_t, sin_t, g_pre_mix, w_in, g_sgu_v, w_spatial, b_spatial, g_out_sgu, g_out_attn,
           w_out, g_post_mix, g_pre_ffn, w_ff1, w_ff2, g_post_ffn):
    bsz, s, d = x.shape
    d_sgu = SGU_GROUPS * SGU_GROUP_DIM
    nq = N_HEADS * HEAD_DIM
    nkv = N_KV_HEADS * HEAD_DIM
    nqi = IDX_HEADS * IDX_DIM
    topk = min(TOPK_MAX, s // 4)

    o_q = 2 * d_sgu
    o_qi = o_q + nq + 2 * nkv
    w_uv = w_in[:, :o_q].astype(BF16)
    wt_main = w_in[:, o_q:o_qi].T.astype(BF16)
    n_idx = nqi + IDX_DIM + IDX_HEADS
    wt_idx = jnp.pad(w_in[:, o_qi:o_qi + n_idx].T, ((0, (-n_idx) % 16), (0, 0)))
    wt_idx_hi = wt_idx.astype(BF16)
    wt_idx_lo = (wt_idx - wt_idx_hi.astype(F32)).astype(BF16)
    b_sp = jnp.broadcast_to(b_spatial[:, :, None], (SGU_GROUPS, CHUNK, SGU_GROUP_DIM))

    ya, qt, kk, vt, qi3t, ki3, wt = _in_proj(
        x, mod3, g_pre_mix.reshape(1, d), w_uv, wt_main, wt_idx_hi, wt_idx_lo, cos_t, sin_t,
        g_sgu_v.reshape(1, d_sgu), w_spatial, b_sp, g_out_sgu.reshape(1, d_sgu))

    g_attn_b = jnp.broadcast_to(g_out_attn[:, None], (nq, ATT_TQ))
    yb = _dsa_attn(qi3t, wt, ki3, qt, kk, vt, g_attn_b, topk)

    return _out_ffn(x, ya, yb, mod3, w_out.astype(BF16), g_post_mix.reshape(1, d), g_pre_ffn.reshape(1, d),
                    w_ff1.astype(BF16), w_ff2.astype(BF16), g_post_ffn.reshape(1, d))


def kernel(x, c, positions, w_ada, b_ada, g_pre_mix, w_in, g_sgu_v, w_spatial, b_spatial, g_out_sgu, g_out_attn,
           w_out, g_post_mix, g_pre_ffn, w_ff1, w_ff2, g_post_ffn):
    bsz, s, d = x.shape
    inv_freq = ROPE_THETA ** (-jnp.arange(0, ROT_DIM, 2, dtype=F32) / ROT_DIM)
    ang = positions.astype(F32)[:, None, :] * inv_freq[None, :, None]
    cos_t, sin_t = jnp.cos(ang), jnp.sin(ang)
    for l in range(w_ada.shape[0]):
        mod3 = _ada_mod(c, w_ada[l], b_ada[l]).reshape(bsz, 6, d)
        x = _layer(x, mod3, cos_t, sin_t, g_pre_mix[l], w_in[l], g_sgu_v[l], w_spatial[l], b_spatial[l],
                   g_out_sgu[l], g_out_attn[l], w_out[l], g_post_mix[l], g_pre_ffn[l], w_ff1[l], w_ff2[l],
                   g_post_ffn[l])
    return x
```

```python
import functools

import jax
import jax.numpy as jnp
from jax import lax
from jax.experimental import pallas as pl
from jax.experimental.pallas import tpu as pltpu

F32 = jnp.float32
BF16 = jnp.bfloat16

SGU_GROUPS = 4
SGU_GROUP_DIM = 128
CHUNK = 128
N_HEADS = 8
HEAD_DIM = 64
N_KV_HEADS = 2
GQA_GROUP = N_HEADS // N_KV_HEADS
ROT_DIM = 16
ROT_HALF = ROT_DIM // 2
ROPE_THETA = 500000.0
IDX_HEADS = 8
IDX_DIM = 64
TOPK_MAX = 256
EPS = 1e-6

ADA_TN = 1024
PROJ_TM = 512
ATT_TQ = 256
ATT_CH = 256
FFN_TM = 512
FFN_TF = 1024
BISECT_UNROLL = 4
IDX_K = 4 * IDX_DIM
ONES_ROWS = 16
NEG = -1e30
TINY = 1e-30
F32_MAX = 3.4028235e38
LOGIT_BOUND = 96.0
VMEM_LIMIT_BYTES = 56 * 1024 * 1024
Q_SCALE = (HEAD_DIM ** -0.5) * 1.4426950408889634


def _split_bf16(a):
    hi = a.astype(BF16).astype(F32)
    lo = (a - hi).astype(BF16).astype(F32)
    return hi, lo


def _rms(a, g):
    return a * lax.rsqrt(jnp.mean(a * a, axis=-1, keepdims=True) + EPS) * g


def _ada_kernel(c_ref, w_ref, b_ref, o_ref):
    c = c_ref[...]
    ca = c * (1.0 / (1.0 + jnp.exp(-c)))
    c_hi, c_lo = _split_bf16(ca)
    w_hi, w_lo = _split_bf16(w_ref[...])
    c_hi, c_lo, w_hi, w_lo = (t.astype(BF16) for t in (c_hi, c_lo, w_hi, w_lo))
    acc = jnp.dot(c_hi, w_hi, preferred_element_type=F32)
    acc += jnp.dot(c_hi, w_lo, preferred_element_type=F32)
    acc += jnp.dot(c_lo, w_hi, preferred_element_type=F32)
    o_ref[...] = acc + b_ref[...]


def _ada_mod(c, w_ada, b_ada):
    bsz, d = c.shape
    n = w_ada.shape[1]
    return pl.pallas_call(
        _ada_kernel,
        grid=(n // ADA_TN,),
        in_specs=[
            pl.BlockSpec((bsz, d), lambda i: (0, 0)),
            pl.BlockSpec((d, ADA_TN), lambda i: (0, i)),
            pl.BlockSpec((1, ADA_TN), lambda i: (0, i)),
        ],
        out_specs=pl.BlockSpec((bsz, ADA_TN), lambda i: (0, i)),
        out_shape=jax.ShapeDtypeStruct((bsz, n), F32),
        compiler_params=pltpu.CompilerParams(dimension_semantics=("arbitrary",)),
        name="ada_mod",
    )(c, w_ada, b_ada.reshape(1, n))


def _rope_rows(p, cos, sin):
    x1, x2 = p[0:ROT_HALF], p[ROT_HALF:ROT_DIM]
    return jnp.concatenate([x1 * cos - x2 * sin, x2 * cos + x1 * sin, p[ROT_DIM:]], axis=0)


def _dot_nt(a, b):
    return lax.dot_general(a, b, (((1,), (1,)), ((), ())), preferred_element_type=F32)


def _in_proj_kernel(x_ref, mod_ref, gpre_ref, wuv_ref, wtm_ref, wti_hi_ref, wti_lo_ref, cos_ref, sin_ref,
                    gv_ref, wsp_ref, bsp_ref, gosgu_ref,
                    ya_ref, qt_ref, k_ref, vt_ref, qi_ref, ki_ref, wt_ref, kn_ref):
    tm = x_ref.shape[1]
    d_sgu = SGU_GROUPS * SGU_GROUP_DIM
    x = x_ref[0]
    sh1 = mod_ref[0, 0:1, :]
    sc1 = mod_ref[0, 1:2, :]
    h = _rms(x, gpre_ref[...]) * (1.0 + sc1) + sh1
    h_hi32, h_lo32 = _split_bf16(h)
    h_hi, h_lo = h_hi32.astype(BF16), h_lo32.astype(BF16)
    cos, sin = cos_ref[0], sin_ref[0]

    puv = jnp.dot(h_hi, wuv_ref[...], preferred_element_type=F32)
    zu = jax.nn.gelu(puv[:, :d_sgu])
    zv = jax.nn.gelu(puv[:, d_sgu:])
    row = lax.broadcasted_iota(jnp.int32, (CHUNK, CHUNK), 0)
    col = lax.broadcasted_iota(jnp.int32, (CHUNK, CHUNK), 1)
    ya_groups = []
    for g in range(SGU_GROUPS):
        sl = slice(g * SGU_GROUP_DIM, (g + 1) * SGU_GROUP_DIM)
        vn = _rms(zv[:, sl], gv_ref[:, sl]).astype(BF16)
        wm = jnp.where(col <= row, wsp_ref[g], 0.0).astype(BF16)
        parts = []
        for ci in range(tm // CHUNK):
            rs = slice(ci * CHUNK, (ci + 1) * CHUNK)
            mixed = jnp.dot(wm, vn[rs], preferred_element_type=F32) + bsp_ref[g]
            parts.append(zu[rs, sl] * mixed)
        ya_groups.append(jnp.concatenate(parts, axis=0))
    ya = jnp.concatenate(ya_groups, axis=1)
    ya_ref[0] = _rms(ya, gosgu_ref[...]).astype(BF16)

    ptm = _dot_nt(wtm_ref[...], h_hi)
    nq = N_HEADS * HEAD_DIM
    nkv = N_KV_HEADS * HEAD_DIM
    for hh in range(N_HEADS):
        rs = slice(hh * HEAD_DIM, (hh + 1) * HEAD_DIM)
        qt_ref[0, rs, :] = (_rope_rows(ptm[rs], cos, sin) * Q_SCALE).astype(BF16)
    kt = jnp.concatenate(
        [_rope_rows(ptm[nq + g * HEAD_DIM:nq + (g + 1) * HEAD_DIM], cos, sin) for g in range(N_KV_HEADS)], axis=0)
    k_ref[0] = kt.T.astype(BF16)
    kb = kt.astype(BF16).astype(F32)
    kn_ref[0] = jnp.concatenate(
        [jnp.sum(jnp.square(kb[g * HEAD_DIM:(g + 1) * HEAD_DIM]), axis=0, keepdims=True)
         for g in range(N_KV_HEADS)], axis=0)
    vt = ptm[nq + nkv:nq + 2 * nkv].astype(BF16)
    for ci in range(tm // ATT_CH):
        vt_ref[0, ci] = vt[:, ci * ATT_CH:(ci + 1) * ATT_CH]

    pti = _dot_nt(wti_hi_ref[...], h_hi) + _dot_nt(wti_lo_ref[...], h_hi) + _dot_nt(wti_hi_ref[...], h_lo)
    nqi = IDX_HEADS * IDX_DIM
    for hh in range(IDX_HEADS):
        r = _rope_rows(pti[hh * IDX_DIM:(hh + 1) * IDX_DIM], cos, sin)
        r_hi, r_lo = _split_bf16(r)
        qi_ref[0, hh] = jnp.concatenate([r_hi, r_hi, r_lo, r_lo], axis=0).astype(BF16)
    kr = _rope_rows(pti[nqi:nqi + IDX_DIM], cos, sin)
    k_hi, k_lo = _split_bf16(kr)
    ki_ref[0] = jnp.concatenate([k_hi, k_lo, k_hi, k_lo], axis=0).T.astype(BF16)
    wt_ref[0] = pti[nqi + IDX_DIM:nqi + IDX_DIM + IDX_HEADS] * ((IDX_HEADS ** -0.5) * (IDX_DIM ** -0.5))


def _in_proj(x, mod3, g_pre, w_uv, wt_main, wt_idx_hi, wt_idx_lo, cos_t, sin_t, g_v, w_sp, b_sp, g_osgu):
    bsz, s, d = x.shape
    tm = PROJ_TM
    d_sgu = SGU_GROUPS * SGU_GROUP_DIM
    nq = N_HEADS * HEAD_DIM
    nkv = N_KV_HEADS * HEAD_DIM
    const2 = lambda b, i: (0, 0)
    const3 = lambda b, i: (0, 0, 0)
    return pl.pallas_call(
        _in_proj_kernel,
        grid=(bsz, s // tm),
        in_specs=[
            pl.BlockSpec((1, tm, d), lambda b, i: (b, i, 0)),
            pl.BlockSpec((1, 6, d), lambda b, i: (b, 0, 0)),
            pl.BlockSpec((1, d), const2),
            pl.BlockSpec(w_uv.shape, const2),
            pl.BlockSpec(wt_main.shape, const2),
            pl.BlockSpec(wt_idx_hi.shape, const2),
            pl.BlockSpec(wt_idx_lo.shape, const2),
            pl.BlockSpec((1, ROT_HALF, tm), lambda b, i: (b, 0, i)),
            pl.BlockSpec((1, ROT_HALF, tm), lambda b, i: (b, 0, i)),
            pl.BlockSpec((1, d_sgu), const2),
            pl.BlockSpec(w_sp.shape, const3),
            pl.BlockSpec(b_sp.shape, const3),
            pl.BlockSpec((1, d_sgu), const2),
        ],
        out_specs=[
            pl.BlockSpec((1, tm, d_sgu), lambda b, i: (b, i, 0)),
            pl.BlockSpec((1, nq, tm), lambda b, i: (b, 0, i)),
            pl.BlockSpec((1, tm, nkv), lambda b, i: (b, i, 0)),
            pl.BlockSpec((1, tm // ATT_CH, nkv, ATT_CH), lambda b, i: (b, i, 0, 0)),
            pl.BlockSpec((1, IDX_HEADS, IDX_K, tm), lambda b, i: (b, 0, 0, i)),
            pl.BlockSpec((1, tm, IDX_K), lambda b, i: (b, i, 0)),
            pl.BlockSpec((1, IDX_HEADS, tm), lambda b, i: (b, 0, i)),
            pl.BlockSpec((1, N_KV_HEADS, tm), lambda b, i: (b, 0, i)),
        ],
        out_shape=[
            jax.ShapeDtypeStruct((bsz, s, d_sgu), BF16),
            jax.ShapeDtypeStruct((bsz, nq, s), BF16),
            jax.ShapeDtypeStruct((bsz, s, nkv), BF16),
            jax.ShapeDtypeStruct((bsz, s // ATT_CH, nkv, ATT_CH), BF16),
            jax.ShapeDtypeStruct((bsz, IDX_HEADS, IDX_K, s), BF16),
            jax.ShapeDtypeStruct((bsz, s, IDX_K), BF16),
            jax.ShapeDtypeStruct((bsz, IDX_HEADS, s), F32),
            jax.ShapeDtypeStruct((bsz, N_KV_HEADS, s), F32),
        ],
        compiler_params=pltpu.CompilerParams(
            dimension_semantics=("arbitrary", "arbitrary"), vmem_limit_bytes=VMEM_LIMIT_BYTES),
        name="in_proj",
    )(x, mod3, g_pre, w_uv, wt_main, wt_idx_hi, wt_idx_lo, cos_t, sin_t, g_v, w_sp, b_sp, g_osgu)


def _attn_kernel(qi_ref, w_ref, ki_ref, q_ref, k_ref, vt_ref, kn_ref, g_ref, o_ref,
                 sc_ref, lg_ref, p_ref, qpad_ref, m_ref, acc_ref, *, topk):
    tq, ch = ATT_TQ, ATT_CH
    j = pl.program_id(1)
    nch = j + 1
    q_pos = j * tq + lax.broadcasted_iota(jnp.int32, (1, tq), 1)
    row_iota = lax.broadcasted_iota(jnp.int32, (ch, tq), 0)

    def for_chunks(body, init):
        def pair(kp, carry):
            return body(2 * kp + 1, body(2 * kp, carry))
        carry = lax.fori_loop(0, nch // 2, pair, init)
        return lax.cond(nch % 2 == 1, lambda c: body(nch - 1, c), lambda c: c, carry)

    def fold8(a, op):
        return op(a.reshape(ch // 8, 8, tq), axis=0)

    def score_chunk(kc, carry):
        mn, mx, n_pos, n_nonneg = carry
        kic = ki_ref[0, pl.ds(pl.multiple_of(kc * ch, ch), ch), :]
        acc = jnp.zeros((ch, tq), F32)
        for hh in range(IDX_HEADS):
            dd = jnp.dot(kic, qi_ref[0, hh], preferred_element_type=F32)
            acc = acc + w_ref[0, hh:hh + 1, :] * jnp.maximum(dd, 0.0)
        causal = (row_iota + kc * ch) <= q_pos
        sc = jnp.where(causal, acc, -jnp.inf)
        sc_ref[kc] = sc
        mn = jnp.minimum(mn, fold8(jnp.where(causal, acc, jnp.inf), jnp.min))
        mx = jnp.maximum(mx, fold8(sc, jnp.max))
        n_pos = n_pos + fold8(jnp.where(sc > 0.0, 1.0, 0.0), jnp.sum)
        n_nonneg = n_nonneg + fold8(jnp.where(sc >= 0.0, 1.0, 0.0), jnp.sum)
        return mn, mx, n_pos, n_nonneg

    mn, mx, c0_gt, c0_ge = for_chunks(
        score_chunk,
        (jnp.full((8, tq), jnp.inf, F32), jnp.full((8, tq), -jnp.inf, F32),
         jnp.zeros((8, tq), F32), jnp.zeros((8, tq), F32)))
    mn = jnp.min(mn, axis=0, keepdims=True)
    mx = jnp.max(mx, axis=0, keepdims=True)
    c0_gt = jnp.sum(c0_gt, axis=0, keepdims=True)
    c0_ge = jnp.sum(c0_ge, axis=0, keepdims=True)

    npair = (nch + 1) // 2

    @pl.when(nch % 2 == 1)
    def _():
        sc_ref[nch] = jnp.full((ch, tq), -jnp.inf, F32)

    def count(pred):
        def body(kp, c):
            for half in range(2):
                c = c + fold8(jnp.where(pred(sc_ref[2 * kp + half]), 1.0, 0.0), jnp.sum)
            return c
        c8 = lax.fori_loop(0, npair, body, jnp.zeros((8, tq), F32))
        return jnp.sum(c8, axis=0, keepdims=True)

    n_causal = (q_pos + 1).astype(F32)
    kk = jnp.minimum(float(topk), n_causal)
    at_zero = (c0_gt < kk) & (c0_ge >= kk)
    take_all = n_causal == kk
    pos = c0_gt >= kk
    lo0 = jnp.where(pos, jnp.maximum(mn, 0.0), mn)
    clo0 = jnp.where(pos & (mn <= 0.0), c0_ge, n_causal)
    hi0 = jnp.where(pos, jnp.minimum(2.0 * mx + TINY, F32_MAX), 0.0)
    thr0 = jnp.where(take_all, mn, 0.0)
    tied0 = jnp.logical_not(take_all) & at_zero & (c0_ge > kk)
    done0 = at_zero | take_all

    def bisect_step(state):
        lo, clo, hi, thr, done, tied = state
        mid = 0.5 * lo + 0.5 * hi
        valid = (mid > lo) & (mid < hi)
        c = count(lambda s: s >= mid)
        move_lo = valid & (c >= kk)
        thr = jnp.where(done, thr, jnp.where(valid, mid, lo))
        tied = tied | (jnp.logical_not(done) & jnp.logical_not(valid) & (clo > kk))
        done = done | jnp.logical_not(valid) | (c == kk)
        hi = jnp.where(valid & (c < kk), mid, hi)
        lo = jnp.where(move_lo, mid, lo)
        clo = jnp.where(move_lo, c, clo)
        return lo, clo, hi, thr, done, tied

    def as_f(m):
        return jnp.where(m, 1.0, 0.0)

    def bisect_body(carry):
        (lo, clo, hi, thr, done_f, tied_f), _ = carry
        st = (lo, clo, hi, thr, done_f > 0.5, tied_f > 0.5)
        for _ in range(BISECT_UNROLL):
            st = bisect_step(st)
        lo, clo, hi, thr, done, tied = st
        return (lo, clo, hi, thr, as_f(done), as_f(tied)), jnp.max(1.0 - as_f(done))

    (_, _, _, thr, _, tied_f), _ = lax.while_loop(
        lambda carry: carry[1] > 0.0, bisect_body,
        ((lo0, clo0, hi0, thr0, as_f(done0), as_f(tied0)), jnp.max(1.0 - as_f(done0))))

    tie = jnp.max(tied_f) > 0.0

    @pl.when(jnp.logical_not(tie))
    def _():
        def body(kc, _):
            sc_ref[kc] = jnp.where(sc_ref[kc] >= thr, 0.0, NEG)
            return 0
        lax.fori_loop(0, nch, body, 0)

    @pl.when(tie)
    def _():
        r = lax.broadcasted_iota(jnp.int32, (ch, ch), 0)
        c = lax.broadcasted_iota(jnp.int32, (ch, ch), 1)
        below = jnp.where(c < r, 1.0, 0.0).astype(BF16)
        need = kk - count(lambda s: s > thr)

        def body(kc, seen):
            s = sc_ref[kc]
            eq = jnp.where(s == thr, 1.0, 0.0)
            rank = jnp.dot(below, eq.astype(BF16), preferred_element_type=F32) + seen
            sel = (s > thr) | ((s == thr) & (rank < need))
            sc_ref[kc] = jnp.where(sel, 0.0, NEG)
            return seen + jnp.sum(eq, axis=0, keepdims=True)
        lax.fori_loop(0, nch, body, jnp.zeros((1, tq), F32))

    zq = jnp.zeros((HEAD_DIM, tq), BF16)
    for hh in range(N_HEADS):
        qh = q_ref[0, hh * HEAD_DIM:(hh + 1) * HEAD_DIM, :]
        qpad_ref[hh] = jnp.concatenate([qh, zq] if hh < GQA_GROUP else [zq, qh], axis=0)

    acc_ref[...] = jnp.zeros(acc_ref.shape, F32)
    ones = jnp.ones((ONES_ROWS, ch), BF16)

    q_sq = jnp.max(jnp.sum(jnp.square(qpad_ref[...].astype(F32)), axis=1))
    bounded = q_sq * jnp.max(kn_ref[0]) <= LOGIT_BOUND * LOGIT_BOUND

    def key_chunk(kc):
        return k_ref[0, pl.ds(pl.multiple_of(kc * ch, ch), ch), :]

    def pv_accumulate(kc, probs):
        vt = vt_ref[0, kc]
        for g in range(N_KV_HEADS):
            vaug = jnp.concatenate([vt[g * HEAD_DIM:(g + 1) * HEAD_DIM], ones], axis=0)
            for hh in range(g * GQA_GROUP, (g + 1) * GQA_GROUP):
                acc_ref[hh] += jnp.dot(vaug, probs(hh), preferred_element_type=F32)

    @pl.when(bounded)
    def _():
        def prob_chunk(kc, _):
            kcs = key_chunk(kc)
            bias = sc_ref[kc]
            for hh in range(N_HEADS):
                s = jnp.dot(kcs, qpad_ref[hh], preferred_element_type=F32) + bias
                p_ref[hh, kc] = jnp.exp2(s).astype(BF16)
            return 0
        for_chunks(prob_chunk, 0)

        def pv_chunk(kc, _):
            pv_accumulate(kc, lambda hh: p_ref[hh, kc])
            return 0
        for_chunks(pv_chunk, 0)

    @pl.when(jnp.logical_not(bounded))
    def _():
        def logits_chunk(kc, ms):
            kcs = key_chunk(kc)
            bias = sc_ref[kc]
            out = []
            for hh in range(N_HEADS):
                s = jnp.dot(kcs, qpad_ref[hh], preferred_element_type=F32) + bias
                lg_ref[hh, kc] = s
                out.append(jnp.maximum(ms[hh], fold8(s, jnp.max)))
            return tuple(out)

        ms = for_chunks(logits_chunk, tuple(jnp.full((8, tq), NEG, F32) for _ in range(N_HEADS)))
        for hh in range(N_HEADS):
            m_ref[hh] = jnp.max(ms[hh], axis=0, keepdims=True)

        def pv_chunk(kc, _):
            pv_accumulate(kc, lambda hh: jnp.exp2(lg_ref[hh, kc] - m_ref[hh]).astype(BF16))
            return 0
        for_chunks(pv_chunk, 0)

    outs = []
    for hh in range(N_HEADS):
        a = acc_ref[hh]
        outs.append(a[0:HEAD_DIM] / a[HEAD_DIM:HEAD_DIM + 1])
    y = jnp.concatenate(outs, axis=0)
    yn = y * lax.rsqrt(jnp.mean(y * y, axis=0, keepdims=True) + EPS) * g_ref[...]
    o_ref[0] = yn.T.astype(BF16)


def _dsa_attn(qi3t, wt, ki3, qt, kk, vt, kn, g_attn_b, topk):
    bsz, s, _ = ki3.shape
    tq, ch = ATT_TQ, ATT_CH
    nq = N_HEADS * HEAD_DIM
    nkv = N_KV_HEADS * HEAD_DIM
    return pl.pallas_call(
        functools.partial(_attn_kernel, topk=topk),
        grid=(bsz, s // tq),
        in_specs=[
            pl.BlockSpec((1, IDX_HEADS, IDX_K, tq), lambda b, j: (b, 0, 0, j)),
            pl.BlockSpec((1, IDX_HEADS, tq), lambda b, j: (b, 0, j)),
            pl.BlockSpec((1, s, IDX_K), lambda b, j: (b, 0, 0)),
            pl.BlockSpec((1, nq, tq), lambda b, j: (b, 0, j)),
            pl.BlockSpec((1, s, nkv), lambda b, j: (b, 0, 0)),
            pl.BlockSpec((1, s // ch, nkv, ch), lambda b, j: (b, 0, 0, 0)),
            pl.BlockSpec((1, N_KV_HEADS, s), lambda b, j: (b, 0, 0)),
            pl.BlockSpec((nq, tq), lambda b, j: (0, 0)),
        ],
        out_specs=pl.BlockSpec((1, tq, nq), lambda b, j: (b, j, 0)),
        out_shape=jax.ShapeDtypeStruct((bsz, s, nq), BF16),
        scratch_shapes=[
            pltpu.VMEM((2 * ((s // ch + 1) // 2), ch, tq), F32),
            pltpu.VMEM((N_HEADS, s // ch, ch, tq), F32),
            pltpu.VMEM((N_HEADS, s // ch, ch, tq), BF16),
            pltpu.VMEM((N_HEADS, 2 * HEAD_DIM, tq), BF16),
            pltpu.VMEM((N_HEADS, 1, tq), F32),
            pltpu.VMEM((N_HEADS, HEAD_DIM + ONES_ROWS, tq), F32),
        ],
        compiler_params=pltpu.CompilerParams(
            dimension_semantics=("arbitrary", "arbitrary"), vmem_limit_bytes=VMEM_LIMIT_BYTES),
        name="dsa_attn",
    )(qi3t, wt, ki3, qt, kk, vt, kn, g_attn_b)


def _out_ffn_kernel(x_ref, ya_ref, yb_ref, mod_ref, wo_ref, gpm_ref, gpf_ref, w1_ref, w2_ref, gpo_ref, o_ref):
    d_a = ya_ref.shape[2]
    x = x_ref[0]
    g1 = mod_ref[0, 2:3, :]
    sh2 = mod_ref[0, 3:4, :]
    sc2 = mod_ref[0, 4:5, :]
    g2 = mod_ref[0, 5:6, :]
    o = jnp.dot(ya_ref[0], wo_ref[0:d_a, :], preferred_element_type=F32)
    o += jnp.dot(yb_ref[0], wo_ref[d_a:, :], preferred_element_type=F32)
    x1 = x + g1 * _rms(o, gpm_ref[...])
    h2 = (_rms(x1, gpf_ref[...]) * (1.0 + sc2) + sh2).astype(BF16)
    f = jnp.zeros(x.shape, F32)
    for c in range(w1_ref.shape[1] // FFN_TF):
        a = jnp.dot(h2, w1_ref[:, c * FFN_TF:(c + 1) * FFN_TF], preferred_element_type=F32)
        a = jnp.square(jnp.maximum(a, 0.0)).astype(BF16)
        f += jnp.dot(a, w2_ref[c * FFN_TF:(c + 1) * FFN_TF, :], preferred_element_type=F32)
    o_ref[0] = x1 + g2 * _rms(f, gpo_ref[...])


def _out_ffn(x, ya, yb, mod3, w_out, g_pm, g_pf, w1, w2, g_po):
    bsz, s, d = x.shape
    tm = FFN_TM
    d_a = ya.shape[2]
    const2 = lambda b, i: (0, 0)
    resident = functools.partial(pl.BlockSpec, index_map=const2, pipeline_mode=pl.Buffered(1))
    return pl.pallas_call(
        _out_ffn_kernel,
        grid=(bsz, s // tm),
        in_specs=[
            pl.BlockSpec((1, tm, d), lambda b, i: (b, i, 0)),
            pl.BlockSpec((1, tm, d_a), lambda b, i: (b, i, 0)),
            pl.BlockSpec((1, tm, yb.shape[2]), lambda b, i: (b, i, 0)),
            pl.BlockSpec((1, 6, d), lambda b, i: (b, 0, 0)),
            resident(w_out.shape),
            pl.BlockSpec((1, d), const2),
            pl.BlockSpec((1, d), const2),
            resident(w1.shape),
            resident(w2.shape),
            pl.BlockSpec((1, d), const2),
        ],
        out_specs=pl.BlockSpec((1, tm, d), lambda b, i: (b, i, 0)),
        out_shape=jax.ShapeDtypeStruct((bsz, s, d), F32),
        compiler_params=pltpu.CompilerParams(
            dimension_semantics=("arbitrary", "arbitrary"), vmem_limit_bytes=VMEM_LIMIT_BYTES),
        name="out_ffn",
    )(x, ya, yb, mod3, w_out, g_pm, g_pf, w1, w2, g_po)


def _layer(x, mod3, cos_t, sin_t, g_pre_mix, w_in, g_sgu_v, w_spatial, b_spatial, g_out_sgu, g_out_attn,
           w_out, g_post_mix, g_pre_ffn, w_ff1, w_ff2, g_post_ffn):
    bsz, s, d = x.shape
    d_sgu = SGU_GROUPS * SGU_GROUP_DIM
    nq = N_HEADS * HEAD_DIM
    nkv = N_KV_HEADS * HEAD_DIM
    nqi = IDX_HEADS * IDX_DIM
    topk = min(TOPK_MAX, s // 4)

    o_q = 2 * d_sgu
    o_qi = o_q + nq + 2 * nkv
    w_uv = w_in[:, :o_q].astype(BF16)
    wt_main = w_in[:, o_q:o_qi].T.astype(BF16)
    n_idx = nqi + IDX_DIM + IDX_HEADS
    wt_idx = jnp.pad(w_in[:, o_qi:o_qi + n_idx].T, ((0, (-n_idx) % 16), (0, 0)))
    wt_idx_hi = wt_idx.astype(BF16)
    wt_idx_lo = (wt_idx - wt_idx_hi.astype(F32)).astype(BF16)
    b_sp = jnp.broadcast_to(b_spatial[:, :, None], (SGU_GROUPS, CHUNK, SGU_GROUP_DIM))

    ya, qt, kk, vt, qi3t, ki3, wt, kn = _in_proj(
        x, mod3, g_pre_mix.reshape(1, d), w_uv, wt_main, wt_idx_hi, wt_idx_lo, cos_t, sin_t,
        g_sgu_v.reshape(1, d_sgu), w_spatial, b_sp, g_out_sgu.reshape(1, d_sgu))

    g_attn_b = jnp.broadcast_to(g_out_attn[:, None], (nq, ATT_TQ))
    yb = _dsa_attn(qi3t, wt, ki3, qt, kk, vt, kn, g_attn_b, topk)

    return _out_ffn(x, ya, yb, mod3, w_out.astype(BF16), g_post_mix.reshape(1, d), g_pre_ffn.reshape(1, d),
                    w_ff1.astype(BF16), w_ff2.astype(BF16), g_post_ffn.reshape(1, d))


def kernel(x, c, positions, w_ada, b_ada, g_pre_mix, w_in, g_sgu_v, w_spatial, b_spatial, g_out_sgu, g_out_attn,
           w_out, g_post_mix, g_pre_ffn, w_ff1, w_ff2, g_post_ffn):
    bsz, s, d = x.shape
    inv_freq = ROPE_THETA ** (-jnp.arange(0, ROT_DIM, 2, dtype=F32) / ROT_DIM)
    ang = positions.astype(F32)[:, None, :] * inv_freq[None, :, None]
    cos_t, sin_t = jnp.cos(ang), jnp.sin(ang)
    for l in range(w_ada.shape[0]):
        mod3 = _ada_mod(c, w_ada[l], b_ada[l]).reshape(bsz, 6, d)
        x = _layer(x, mod3, cos_t, sin_t, g_pre_mix[l], w_in[l], g_sgu_v[l], w_spatial[l], b_spatial[l],
                   g_out_sgu[l], g_out_attn[l], w_out[l], g_post_mix[l], g_pre_ffn[l], w_ff1[l], w_ff2[l],
                   g_post_ffn[l])
    return x
```

```python
import functools

import jax
import jax.numpy as jnp
from jax import lax
from jax.experimental import pallas as pl
from jax.experimental.pallas import tpu as pltpu

F32 = jnp.float32
BF16 = jnp.bfloat16

SGU_GROUPS = 4
SGU_GROUP_DIM = 128
CHUNK = 128
N_HEADS = 8
HEAD_DIM = 64
N_KV_HEADS = 2
GQA_GROUP = N_HEADS // N_KV_HEADS
ROT_DIM = 16
ROT_HALF = ROT_DIM // 2
ROPE_THETA = 500000.0
IDX_HEADS = 8
IDX_DIM = 64
TOPK_MAX = 256
EPS = 1e-6

ADA_TN = 1024
PROJ_TM = 512
ATT_TQ = 256
ATT_CH = 256
FFN_TM = 512
FFN_TF = 1024
BISECT_UNROLL = 4
IDX_K = 4 * IDX_DIM
ONES_ROWS = 16
NEG = -1e30
TINY = 1e-30
F32_MAX = 3.4028235e38
LOGIT_BOUND = 96.0
VMEM_LIMIT_BYTES = 56 * 1024 * 1024
Q_SCALE = (HEAD_DIM ** -0.5) * 1.4426950408889634


def _split_bf16(a):
    hi = a.astype(BF16).astype(F32)
    lo = (a - hi).astype(BF16).astype(F32)
    return hi, lo


def _rms(a, g):
    return a * lax.rsqrt(jnp.mean(a * a, axis=-1, keepdims=True) + EPS) * g


def _ada_kernel(c_ref, w_ref, b_ref, o_ref):
    c = c_ref[...]
    ca = c * (1.0 / (1.0 + jnp.exp(-c)))
    c_hi, c_lo = _split_bf16(ca)
    w_hi, w_lo = _split_bf16(w_ref[...])
    c_hi, c_lo, w_hi, w_lo = (t.astype(BF16) for t in (c_hi, c_lo, w_hi, w_lo))
    acc = jnp.dot(c_hi, w_hi, preferred_element_type=F32)
    acc += jnp.dot(c_hi, w_lo, preferred_element_type=F32)
    acc += jnp.dot(c_lo, w_hi, preferred_element_type=F32)
    o_ref[...] = acc + b_ref[...]


def _ada_mod(c, w_ada, b_ada):
    bsz, d = c.shape
    n = w_ada.shape[1]
    return pl.pallas_call(
        _ada_kernel,
        grid=(n // ADA_TN,),
        in_specs=[
            pl.BlockSpec((bsz, d), lambda i: (0, 0)),
            pl.BlockSpec((d, ADA_TN), lambda i: (0, i)),
            pl.BlockSpec((1, ADA_TN), lambda i: (0, i)),
        ],
        out_specs=pl.BlockSpec((bsz, ADA_TN), lambda i: (0, i)),
        out_shape=jax.ShapeDtypeStruct((bsz, n), F32),
        compiler_params=pltpu.CompilerParams(dimension_semantics=("arbitrary",)),
        name="ada_mod",
    )(c, w_ada, b_ada.reshape(1, n))


def _rope_rows(p, cos, sin):
    x1, x2 = p[0:ROT_HALF], p[ROT_HALF:ROT_DIM]
    return jnp.concatenate([x1 * cos - x2 * sin, x2 * cos + x1 * sin, p[ROT_DIM:]], axis=0)


def _dot_nt(a, b):
    return lax.dot_general(a, b, (((1,), (1,)), ((), ())), preferred_element_type=F32)


def _in_proj_kernel(x_ref, mod_ref, gpre_ref, wuv_ref, wtm_ref, wti_hi_ref, wti_lo_ref, cos_ref, sin_ref,
                    gv_ref, wsp_ref, bsp_ref, gosgu_ref,
                    ya_ref, qt_ref, k_ref, vt_ref, qi_ref, ki_ref, wt_ref, kn_ref):
    d_sgu = SGU_GROUPS * SGU_GROUP_DIM
    nq = N_HEADS * HEAD_DIM
    nkv = N_KV_HEADS * HEAD_DIM
    nqi = IDX_HEADS * IDX_DIM
    sh1 = mod_ref[0, 0:1, :]
    sc1 = mod_ref[0, 1:2, :]
    row = lax.broadcasted_iota(jnp.int32, (CHUNK, CHUNK), 0)
    col = lax.broadcasted_iota(jnp.int32, (CHUNK, CHUNK), 1)
    w_mix = [jnp.where(col <= row, wsp_ref[g], 0.0).astype(BF16) for g in range(SGU_GROUPS)]

    def project(t0, tn):
        tok = slice(t0, t0 + tn)
        h = _rms(x_ref[0, tok, :], gpre_ref[...]) * (1.0 + sc1) + sh1
        h_hi32, h_lo32 = _split_bf16(h)
        h_hi, h_lo = h_hi32.astype(BF16), h_lo32.astype(BF16)
        cos, sin = cos_ref[0, :, tok], sin_ref[0, :, tok]

        puv = jnp.dot(h_hi, wuv_ref[...], preferred_element_type=F32)
        pti = _dot_nt(wti_hi_ref[...], h_hi) + _dot_nt(wti_lo_ref[...], h_hi) + _dot_nt(wti_hi_ref[...], h_lo)
        ptm = _dot_nt(wtm_ref[...], h_hi)

        zu = jax.nn.gelu(puv[:, :d_sgu])
        zv = jax.nn.gelu(puv[:, d_sgu:])
        ya_groups = []
        for g in range(SGU_GROUPS):
            sl = slice(g * SGU_GROUP_DIM, (g + 1) * SGU_GROUP_DIM)
            vn = _rms(zv[:, sl], gv_ref[:, sl]).astype(BF16)
            parts = []
            for ci in range(tn // CHUNK):
                rs = slice(ci * CHUNK, (ci + 1) * CHUNK)
                mixed = jnp.dot(w_mix[g], vn[rs], preferred_element_type=F32) + bsp_ref[g]
                parts.append(zu[rs, sl] * mixed)
            ya_groups.append(jnp.concatenate(parts, axis=0))
        ya = jnp.concatenate(ya_groups, axis=1)
        ya_ref[0, tok, :] = _rms(ya, gosgu_ref[...]).astype(BF16)

        for hh in range(N_HEADS):
            rs = slice(hh * HEAD_DIM, (hh + 1) * HEAD_DIM)
            qt_ref[0, rs, tok] = (_rope_rows(ptm[rs], cos, sin) * Q_SCALE).astype(BF16)
        kt = jnp.concatenate(
            [_rope_rows(ptm[nq + g * HEAD_DIM:nq + (g + 1) * HEAD_DIM], cos, sin) for g in range(N_KV_HEADS)],
            axis=0)
        k_ref[0, tok, :] = kt.T.astype(BF16)
        kb = kt.astype(BF16).astype(F32)
        kn_ref[0, :, tok] = jnp.concatenate(
            [jnp.sum(jnp.square(kb[g * HEAD_DIM:(g + 1) * HEAD_DIM]), axis=0, keepdims=True)
             for g in range(N_KV_HEADS)], axis=0)
        vt = ptm[nq + nkv:nq + 2 * nkv].astype(BF16)
        for ci in range(tn // ATT_CH):
            vt_ref[0, t0 // ATT_CH + ci] = vt[:, ci * ATT_CH:(ci + 1) * ATT_CH]

        for hh in range(IDX_HEADS):
            r = _rope_rows(pti[hh * IDX_DIM:(hh + 1) * IDX_DIM], cos, sin)
            r_hi, r_lo = _split_bf16(r)
            qi_ref[0, hh, :, tok] = jnp.concatenate([r_hi, r_lo], axis=0).astype(BF16)
        kr = _rope_rows(pti[nqi:nqi + IDX_DIM], cos, sin)
        k_hi, k_lo = _split_bf16(kr)
        ki_ref[0, tok, :] = jnp.concatenate([k_hi, k_lo, k_hi, k_lo], axis=0).T.astype(BF16)
        wt_ref[0, :, tok] = (pti[nqi + IDX_DIM:nqi + IDX_DIM + IDX_HEADS]
                             * ((IDX_HEADS ** -0.5) * (IDX_DIM ** -0.5)))

    tm = x_ref.shape[1]
    for t0 in range(0, tm, tm // 2):
        project(t0, tm // 2)


def _in_proj(x, mod3, g_pre, w_uv, wt_main, wt_idx_hi, wt_idx_lo, cos_t, sin_t, g_v, w_sp, b_sp, g_osgu):
    bsz, s, d = x.shape
    tm = PROJ_TM
    d_sgu = SGU_GROUPS * SGU_GROUP_DIM
    nq = N_HEADS * HEAD_DIM
    nkv = N_KV_HEADS * HEAD_DIM
    const2 = lambda b, i: (0, 0)
    const3 = lambda b, i: (0, 0, 0)
    return pl.pallas_call(
        _in_proj_kernel,
        grid=(bsz, s // tm),
        in_specs=[
            pl.BlockSpec((1, tm, d), lambda b, i: (b, i, 0)),
            pl.BlockSpec((1, 6, d), lambda b, i: (b, 0, 0)),
            pl.BlockSpec((1, d), const2),
            pl.BlockSpec(w_uv.shape, const2),
            pl.BlockSpec(wt_main.shape, const2),
            pl.BlockSpec(wt_idx_hi.shape, const2),
            pl.BlockSpec(wt_idx_lo.shape, const2),
            pl.BlockSpec((1, ROT_HALF, tm), lambda b, i: (b, 0, i)),
            pl.BlockSpec((1, ROT_HALF, tm), lambda b, i: (b, 0, i)),
            pl.BlockSpec((1, d_sgu), const2),
            pl.BlockSpec(w_sp.shape, const3),
            pl.BlockSpec(b_sp.shape, const3),
            pl.BlockSpec((1, d_sgu), const2),
        ],
        out_specs=[
            pl.BlockSpec((1, tm, d_sgu), lambda b, i: (b, i, 0)),
            pl.BlockSpec((1, nq, tm), lambda b, i: (b, 0, i)),
            pl.BlockSpec((1, tm, nkv), lambda b, i: (b, i, 0)),
            pl.BlockSpec((1, tm // ATT_CH, nkv, ATT_CH), lambda b, i: (b, i, 0, 0)),
            pl.BlockSpec((1, IDX_HEADS, 2 * IDX_DIM, tm), lambda b, i: (b, 0, 0, i)),
            pl.BlockSpec((1, tm, IDX_K), lambda b, i: (b, i, 0)),
            pl.BlockSpec((1, IDX_HEADS, tm), lambda b, i: (b, 0, i)),
            pl.BlockSpec((1, N_KV_HEADS, tm), lambda b, i: (b, 0, i)),
        ],
        out_shape=[
            jax.ShapeDtypeStruct((bsz, s, d_sgu), BF16),
            jax.ShapeDtypeStruct((bsz, nq, s), BF16),
            jax.ShapeDtypeStruct((bsz, s, nkv), BF16),
            jax.ShapeDtypeStruct((bsz, s // ATT_CH, nkv, ATT_CH), BF16),
            jax.ShapeDtypeStruct((bsz, IDX_HEADS, 2 * IDX_DIM, s), BF16),
            jax.ShapeDtypeStruct((bsz, s, IDX_K), BF16),
            jax.ShapeDtypeStruct((bsz, IDX_HEADS, s), F32),
            jax.ShapeDtypeStruct((bsz, N_KV_HEADS, s), F32),
        ],
        compiler_params=pltpu.CompilerParams(
            dimension_semantics=("arbitrary", "arbitrary"), vmem_limit_bytes=VMEM_LIMIT_BYTES),
        name="in_proj",
    )(x, mod3, g_pre, w_uv, wt_main, wt_idx_hi, wt_idx_lo, cos_t, sin_t, g_v, w_sp, b_sp, g_osgu)


def _attn_kernel(qi_ref, w_ref, ki_ref, q_ref, k_ref, vt_ref, kn_ref, g_ref, o_ref,
                 sc_ref, lg_ref, p_ref, qi4_ref, qpad_ref, m_ref, acc_ref, *, topk):
    tq, ch = ATT_TQ, ATT_CH
    j = pl.program_id(1)
    nch = j + 1
    q_pos = j * tq + lax.broadcasted_iota(jnp.int32, (1, tq), 1)
    row_iota = lax.broadcasted_iota(jnp.int32, (ch, tq), 0)

    def for_chunks(body, init, n=nch):
        def pair(kp, carry):
            return body(2 * kp + 1, body(2 * kp, carry))
        carry = lax.fori_loop(0, n // 2, pair, init)
        return lax.cond(n % 2 == 1, lambda c: body(n - 1, c), lambda c: c, carry)

    def fold8(a, op):
        return op(a.reshape(ch // 8, 8, tq), axis=0)

    for hh in range(IDX_HEADS):
        q_hi, q_lo = qi_ref[0, hh, 0:IDX_DIM, :], qi_ref[0, hh, IDX_DIM:, :]
        qi4_ref[hh] = jnp.concatenate([q_hi, q_hi, q_lo, q_lo], axis=0)

    def score_chunk(kc, carry, diagonal):
        mn, mx, n_pos, n_nonneg = carry
        kic = ki_ref[0, pl.ds(pl.multiple_of(kc * ch, ch), ch), :]
        acc = jnp.zeros((ch, tq), F32)
        for hh in range(IDX_HEADS):
            dd = jnp.dot(kic, qi4_ref[hh], preferred_element_type=F32)
            acc = acc + w_ref[0, hh:hh + 1, :] * jnp.maximum(dd, 0.0)
        if diagonal:
            causal = (row_iota + kc * ch) <= q_pos
            sc = jnp.where(causal, acc, -jnp.inf)
            mn = jnp.minimum(mn, fold8(jnp.where(causal, acc, jnp.inf), jnp.min))
        else:
            sc = acc
            mn = jnp.minimum(mn, fold8(acc, jnp.min))
        sc_ref[kc] = sc
        mx = jnp.maximum(mx, fold8(sc, jnp.max))
        n_pos = n_pos + fold8(jnp.where(sc > 0.0, 1.0, 0.0), jnp.sum)
        n_nonneg = n_nonneg + fold8(jnp.where(sc >= 0.0, 1.0, 0.0), jnp.sum)
        return mn, mx, n_pos, n_nonneg

    stats = for_chunks(
        functools.partial(score_chunk, diagonal=False),
        (jnp.full((8, tq), jnp.inf, F32), jnp.full((8, tq), -jnp.inf, F32),
         jnp.zeros((8, tq), F32), jnp.zeros((8, tq), F32)), j)
    mn, mx, c0_gt, c0_ge = score_chunk(j, stats, diagonal=True)
    mn = jnp.min(mn, axis=0, keepdims=True)
    mx = jnp.max(mx, axis=0, keepdims=True)
    c0_gt = jnp.sum(c0_gt, axis=0, keepdims=True)
    c0_ge = jnp.sum(c0_ge, axis=0, keepdims=True)

    npair = (nch + 1) // 2

    @pl.when(nch % 2 == 1)
    def _():
        sc_ref[nch] = jnp.full((ch, tq), -jnp.inf, F32)

    def count(pred):
        def body(kp, c):
            for half in range(2):
                c = c + fold8(jnp.where(pred(sc_ref[2 * kp + half]), 1.0, 0.0), jnp.sum)
            return c
        c8 = lax.fori_loop(0, npair, body, jnp.zeros((8, tq), F32))
        return jnp.sum(c8, axis=0, keepdims=True)

    n_causal = (q_pos + 1).astype(F32)
    kk = jnp.minimum(float(topk), n_causal)
    at_zero = (c0_gt < kk) & (c0_ge >= kk)
    take_all = n_causal == kk
    pos = c0_gt >= kk
    lo0 = jnp.where(pos, jnp.maximum(mn, 0.0), mn)
    clo0 = jnp.where(pos & (mn <= 0.0), c0_ge, n_causal)
    hi0 = jnp.where(pos, jnp.minimum(2.0 * mx + TINY, F32_MAX), 0.0)
    thr0 = jnp.where(take_all, mn, 0.0)
    tied0 = jnp.logical_not(take_all) & at_zero & (c0_ge > kk)
    done0 = at_zero | take_all

    def bisect_step(state):
        lo, clo, hi, thr, done, tied = state
        mid = 0.5 * lo + 0.5 * hi
        valid = (mid > lo) & (mid < hi)
        c = count(lambda s: s >= mid)
        move_lo = valid & (c >= kk)
        thr = jnp.where(done, thr, jnp.where(valid, mid, lo))
        tied = tied | (jnp.logical_not(done) & jnp.logical_not(valid) & (clo > kk))
        done = done | jnp.logical_not(valid) | (c == kk)
        hi = jnp.where(valid & (c < kk), mid, hi)
        lo = jnp.where(move_lo, mid, lo)
        clo = jnp.where(move_lo, c, clo)
        return lo, clo, hi, thr, done, tied

    def as_f(m):
        return jnp.where(m, 1.0, 0.0)

    def bisect_body(carry):
        (lo, clo, hi, thr, done_f, tied_f), _ = carry
        st = (lo, clo, hi, thr, done_f > 0.5, tied_f > 0.5)
        for _ in range(BISECT_UNROLL):
            st = bisect_step(st)
        lo, clo, hi, thr, done, tied = st
        return (lo, clo, hi, thr, as_f(done), as_f(tied)), jnp.max(1.0 - as_f(done))

    (_, _, _, thr, _, tied_f), _ = lax.while_loop(
        lambda carry: carry[1] > 0.0, bisect_body,
        ((lo0, clo0, hi0, thr0, as_f(done0), as_f(tied0)), jnp.max(1.0 - as_f(done0))))

    tie = jnp.max(tied_f) > 0.0

    @pl.when(jnp.logical_not(tie))
    def _():
        def body(kc, _):
            sc_ref[kc] = jnp.where(sc_ref[kc] >= thr, 0.0, NEG)
            return 0
        lax.fori_loop(0, nch, body, 0)

    @pl.when(tie)
    def _():
        r = lax.broadcasted_iota(jnp.int32, (ch, ch), 0)
        c = lax.broadcasted_iota(jnp.int32, (ch, ch), 1)
        below = jnp.where(c < r, 1.0, 0.0).astype(BF16)
        need = kk - count(lambda s: s > thr)

        def body(kc, seen):
            s = sc_ref[kc]
            eq = jnp.where(s == thr, 1.0, 0.0)
            rank = jnp.dot(below, eq.astype(BF16), preferred_element_type=F32) + seen
            sel = (s > thr) | ((s == thr) & (rank < need))
            sc_ref[kc] = jnp.where(sel, 0.0, NEG)
            return seen + jnp.sum(eq, axis=0, keepdims=True)
        lax.fori_loop(0, nch, body, jnp.zeros((1, tq), F32))

    zq = jnp.zeros((HEAD_DIM, tq), BF16)
    for hh in range(N_HEADS):
        qh = q_ref[0, hh * HEAD_DIM:(hh + 1) * HEAD_DIM, :]
        qpad_ref[hh] = jnp.concatenate([qh, zq] if hh < GQA_GROUP else [zq, qh], axis=0)

    acc_ref[...] = jnp.zeros(acc_ref.shape, F32)
    ones = jnp.ones((ONES_ROWS, ch), BF16)

    q_sq = jnp.max(jnp.sum(jnp.square(qpad_ref[...].astype(F32)), axis=1))
    bounded = q_sq * jnp.max(kn_ref[0]) <= LOGIT_BOUND * LOGIT_BOUND

    def key_chunk(kc):
        return k_ref[0, pl.ds(pl.multiple_of(kc * ch, ch), ch), :]

    def pv_accumulate(kc, probs):
        vt = vt_ref[0, kc]
        for g in range(N_KV_HEADS):
            vaug = jnp.concatenate([vt[g * HEAD_DIM:(g + 1) * HEAD_DIM], ones], axis=0)
            for hh in range(g * GQA_GROUP, (g + 1) * GQA_GROUP):
                acc_ref[hh] += jnp.dot(vaug, probs(hh), preferred_element_type=F32)

    @pl.when(bounded)
    def _():
        def prob_chunk(kc, _):
            kcs = key_chunk(kc)
            bias = sc_ref[kc]
            for hh in range(N_HEADS):
                s = jnp.dot(kcs, qpad_ref[hh], preferred_element_type=F32) + bias
                p_ref[hh, kc] = jnp.exp2(s).astype(BF16)
            return 0
        for_chunks(prob_chunk, 0)

        def pv_chunk(kc, _):
            pv_accumulate(kc, lambda hh: p_ref[hh, kc])
            return 0
        for_chunks(pv_chunk, 0)

    @pl.when(jnp.logical_not(bounded))
    def _():
        def logits_chunk(kc, ms):
            kcs = key_chunk(kc)
            bias = sc_ref[kc]
            out = []
            for hh in range(N_HEADS):
                s = jnp.dot(kcs, qpad_ref[hh], preferred_element_type=F32) + bias
                lg_ref[hh, kc] = s
                out.append(jnp.maximum(ms[hh], fold8(s, jnp.max)))
            return tuple(out)

        ms = for_chunks(logits_chunk, tuple(jnp.full((8, tq), NEG, F32) for _ in range(N_HEADS)))
        for hh in range(N_HEADS):
            m_ref[hh] = jnp.max(ms[hh], axis=0, keepdims=True)

        def pv_chunk(kc, _):
            pv_accumulate(kc, lambda hh: jnp.exp2(lg_ref[hh, kc] - m_ref[hh]).astype(BF16))
            return 0
        for_chunks(pv_chunk, 0)

    outs = []
    for hh in range(N_HEADS):
        a = acc_ref[hh]
        outs.append(a[0:HEAD_DIM] / a[HEAD_DIM:HEAD_DIM + 1])
    y = jnp.concatenate(outs, axis=0)
    yn = y * lax.rsqrt(jnp.mean(y * y, axis=0, keepdims=True) + EPS) * g_ref[...]
    o_ref[0] = yn.T.astype(BF16)


def _dsa_attn(qi3t, wt, ki3, qt, kk, vt, kn, g_attn_b, topk):
    bsz, s, _ = ki3.shape
    tq, ch = ATT_TQ, ATT_CH
    nq = N_HEADS * HEAD_DIM
    nkv = N_KV_HEADS * HEAD_DIM
    return pl.pallas_call(
        functools.partial(_attn_kernel, topk=topk),
        grid=(bsz, s // tq),
        in_specs=[
            pl.BlockSpec((1, IDX_HEADS, 2 * IDX_DIM, tq), lambda b, j: (b, 0, 0, j)),
            pl.BlockSpec((1, IDX_HEADS, tq), lambda b, j: (b, 0, j)),
            pl.BlockSpec((1, s, IDX_K), lambda b, j: (b, 0, 0)),
            pl.BlockSpec((1, nq, tq), lambda b, j: (b, 0, j)),
            pl.BlockSpec((1, s, nkv), lambda b, j: (b, 0, 0)),
            pl.BlockSpec((1, s // ch, nkv, ch), lambda b, j: (b, 0, 0, 0)),
            pl.BlockSpec((1, N_KV_HEADS, s), lambda b, j: (b, 0, 0)),
            pl.BlockSpec((nq, tq), lambda b, j: (0, 0)),
        ],
        out_specs=pl.BlockSpec((1, tq, nq), lambda b, j: (b, j, 0)),
        out_shape=jax.ShapeDtypeStruct((bsz, s, nq), BF16),
        scratch_shapes=[
            pltpu.VMEM((2 * ((s // ch + 1) // 2), ch, tq), F32),
            pltpu.VMEM((N_HEADS, s // ch, ch, tq), F32),
            pltpu.VMEM((N_HEADS, s // ch, ch, tq), BF16),
            pltpu.VMEM((IDX_HEADS, IDX_K, tq), BF16),
            pltpu.VMEM((N_HEADS, 2 * HEAD_DIM, tq), BF16),
            pltpu.VMEM((N_HEADS, 1, tq), F32),
            pltpu.VMEM((N_HEADS, HEAD_DIM + ONES_ROWS, tq), F32),
        ],
        compiler_params=pltpu.CompilerParams(
            dimension_semantics=("arbitrary", "arbitrary"), vmem_limit_bytes=VMEM_LIMIT_BYTES),
        name="dsa_attn",
    )(qi3t, wt, ki3, qt, kk, vt, kn, g_attn_b)


def _out_ffn_kernel(x_ref, ya_ref, yb_ref, mod_ref, wo_ref, gpm_ref, gpf_ref, w1_ref, w2_ref, gpo_ref, o_ref):
    d_a = ya_ref.shape[2]
    tm = x_ref.shape[1]
    g1 = mod_ref[0, 2:3, :]
    sh2 = mod_ref[0, 3:4, :]
    sc2 = mod_ref[0, 4:5, :]
    g2 = mod_ref[0, 5:6, :]

    def mix_residual(rows):
        o = jnp.dot(ya_ref[0, rows, :], wo_ref[0:d_a, :], preferred_element_type=F32)
        o += jnp.dot(yb_ref[0, rows, :], wo_ref[d_a:, :], preferred_element_type=F32)
        return x_ref[0, rows, :] + g1 * _rms(o, gpm_ref[...])

    def mlp(x1):
        h2 = (_rms(x1, gpf_ref[...]) * (1.0 + sc2) + sh2).astype(BF16)
        f = jnp.zeros(x1.shape, F32)
        for c in range(w1_ref.shape[1] // FFN_TF):
            a = jnp.dot(h2, w1_ref[:, c * FFN_TF:(c + 1) * FFN_TF], preferred_element_type=F32)
            a = jnp.square(jnp.maximum(a, 0.0)).astype(BF16)
            f += jnp.dot(a, w2_ref[c * FFN_TF:(c + 1) * FFN_TF, :], preferred_element_type=F32)
        return f

    halves = [pl.ds(i * (tm // 2), tm // 2) for i in range(2)]
    x1 = [mix_residual(rows) for rows in halves]
    f = [mlp(x1h) for x1h in x1]
    for rows, x1h, fh in zip(halves, x1, f):
        o_ref[0, rows, :] = x1h + g2 * _rms(fh, gpo_ref[...])


def _out_ffn(x, ya, yb, mod3, w_out, g_pm, g_pf, w1, w2, g_po):
    bsz, s, d = x.shape
    tm = FFN_TM
    d_a = ya.shape[2]
    const2 = lambda b, i: (0, 0)
    resident = functools.partial(pl.BlockSpec, index_map=const2, pipeline_mode=pl.Buffered(1))
    return pl.pallas_call(
        _out_ffn_kernel,
        grid=(bsz, s // tm),
        in_specs=[
            pl.BlockSpec((1, tm, d), lambda b, i: (b, i, 0)),
            pl.BlockSpec((1, tm, d_a), lambda b, i: (b, i, 0)),
            pl.BlockSpec((1, tm, yb.shape[2]), lambda b, i: (b, i, 0)),
            pl.BlockSpec((1, 6, d), lambda b, i: (b, 0, 0)),
            resident(w_out.shape),
            pl.BlockSpec((1, d), const2),
            pl.BlockSpec((1, d), const2),
            resident(w1.shape),
            resident(w2.shape),
            pl.BlockSpec((1, d), const2),
        ],
        out_specs=pl.BlockSpec((1, tm, d), lambda b, i: (b, i, 0)),
        out_shape=jax.ShapeDtypeStruct((bsz, s, d), F32),
        compiler_params=pltpu.CompilerParams(
            dimension_semantics=("arbitrary", "arbitrary"), vmem_limit_bytes=VMEM_LIMIT_BYTES),
        name="out_ffn",
    )(x, ya, yb, mod3, w_out, g_pm, g_pf, w1, w2, g_po)


def _layer(x, mod3, cos_t, sin_t, g_pre_mix, w_in, g_sgu_v, w_spatial, b_spatial, g_out_sgu, g_out_attn,
           w_out, g_post_mix, g_pre_ffn, w_ff1, w_ff2, g_post_ffn):
    bsz, s, d = x.shape
    d_sgu = SGU_GROUPS * SGU_GROUP_DIM
    nq = N_HEADS * HEAD_DIM
    nkv = N_KV_HEADS * HEAD_DIM
    nqi = IDX_HEADS * IDX_DIM
    topk = min(TOPK_MAX, s // 4)

    o_q = 2 * d_sgu
    o_qi = o_q + nq + 2 * nkv
    w_uv = w_in[:, :o_q].astype(BF16)
    wt_main = w_in[:, o_q:o_qi].T.astype(BF16)
    n_idx = nqi + IDX_DIM + IDX_HEADS
    wt_idx = jnp.pad(w_in[:, o_qi:o_qi + n_idx].T, ((0, (-n_idx) % 16), (0, 0)))
    wt_idx_hi = wt_idx.astype(BF16)
    wt_idx_lo = (wt_idx - wt_idx_hi.astype(F32)).astype(BF16)
    b_sp = jnp.broadcast_to(b_spatial[:, :, None], (SGU_GROUPS, CHUNK, SGU_GROUP_DIM))

    ya, qt, kk, vt, qi3t, ki3, wt, kn = _in_proj(
        x, mod3, g_pre_mix.reshape(1, d), w_uv, wt_main, wt_idx_hi, wt_idx_lo, cos_t, sin_t,
        g_sgu_v.reshape(1, d_sgu), w_spatial, b_sp, g_out_sgu.reshape(1, d_sgu))

    g_attn_b = jnp.broadcast_to(g_out_attn[:, None], (nq, ATT_TQ))
    yb = _dsa_attn(qi3t, wt, ki3, qt, kk, vt, kn, g_attn_b, topk)

    return _out_ffn(x, ya, yb, mod3, w_out.astype(BF16), g_post_mix.reshape(1, d), g_pre_ffn.reshape(1, d),
                    w_ff1.astype(BF16), w_ff2.astype(BF16), g_post_ffn.reshape(1, d))


def kernel(x, c, positions, w_ada, b_ada, g_pre_mix, w_in, g_sgu_v, w_spatial, b_spatial, g_out_sgu, g_out_attn,
           w_out, g_post_mix, g_pre_ffn, w_ff1, w_ff2, g_post_ffn):
    bsz, s, d = x.shape
    inv_freq = ROPE_THETA ** (-jnp.arange(0, ROT_DIM, 2, dtype=F32) / ROT_DIM)
    ang = positions.astype(F32)[:, None, :] * inv_freq[None, :, None]
    cos_t, sin_t = jnp.cos(ang), jnp.sin(ang)
    for l in range(w_ada.shape[0]):
        mod3 = _ada_mod(c, w_ada[l], b_ada[l]).reshape(bsz, 6, d)
        x = _layer(x, mod3, cos_t, sin_t, g_pre_mix[l], w_in[l], g_sgu_v[l], w_spatial[l], b_spatial[l],
                   g_out_sgu[l], g_out_attn[l], w_out[l], g_post_mix[l], g_pre_ffn[l], w_ff1[l], w_ff2[l],
                   g_post_ffn[l])
    return x
```

```python
import functools

import jax
import jax.numpy as jnp
from jax import lax
from jax.experimental import pallas as pl
from jax.experimental.pallas import tpu as pltpu

F32 = jnp.float32
BF16 = jnp.bfloat16

SGU_GROUPS = 4
SGU_GROUP_DIM = 128
CHUNK = 128
N_HEADS = 8
HEAD_DIM = 64
N_KV_HEADS = 2
GQA_GROUP = N_HEADS // N_KV_HEADS
ROT_DIM = 16
ROT_HALF = ROT_DIM // 2
ROPE_THETA = 500000.0
IDX_HEADS = 8
IDX_DIM = 64
TOPK_MAX = 256
EPS = 1e-6

ADA_TN = 1024
PROJ_TM = 512
ATT_TQ = 256
ATT_CH = 256
FFN_TM = 512
FFN_TF = 1024
BISECT_UNROLL = 4
BISECT_BLIND_ROUNDS = 4
IDX_K = 4 * IDX_DIM
ONES_ROWS = 16
NEG = -1e30
TINY = 1e-30
F32_MAX = 3.4028235e38
LOGIT_BOUND = 96.0
VMEM_LIMIT_BYTES = 56 * 1024 * 1024
Q_SCALE = (HEAD_DIM ** -0.5) * 1.4426950408889634


def _split_bf16(a):
    hi = a.astype(BF16).astype(F32)
    lo = (a - hi).astype(BF16).astype(F32)
    return hi, lo


def _rms(a, g):
    return a * lax.rsqrt(jnp.mean(a * a, axis=-1, keepdims=True) + EPS) * g


def _ada_kernel(c_ref, w_ref, b_ref, o_ref):
    c = c_ref[...]
    ca = c * (1.0 / (1.0 + jnp.exp(-c)))
    c_hi, c_lo = _split_bf16(ca)
    w_hi, w_lo = _split_bf16(w_ref[...])
    c_hi, c_lo, w_hi, w_lo = (t.astype(BF16) for t in (c_hi, c_lo, w_hi, w_lo))
    acc = jnp.dot(c_hi, w_hi, preferred_element_type=F32)
    acc += jnp.dot(c_hi, w_lo, preferred_element_type=F32)
    acc += jnp.dot(c_lo, w_hi, preferred_element_type=F32)
    o_ref[...] = acc + b_ref[...]


def _ada_mod(c, w_ada, b_ada):
    bsz, d = c.shape
    n = w_ada.shape[1]
    return pl.pallas_call(
        _ada_kernel,
        grid=(n // ADA_TN,),
        in_specs=[
            pl.BlockSpec((bsz, d), lambda i: (0, 0)),
            pl.BlockSpec((d, ADA_TN), lambda i: (0, i)),
            pl.BlockSpec((1, ADA_TN), lambda i: (0, i)),
        ],
        out_specs=pl.BlockSpec((bsz, ADA_TN), lambda i: (0, i)),
        out_shape=jax.ShapeDtypeStruct((bsz, n), F32),
        compiler_params=pltpu.CompilerParams(dimension_semantics=("arbitrary",)),
        name="ada_mod",
    )(c, w_ada, b_ada.reshape(1, n))


def _rope_rows(p, cos, sin):
    x1, x2 = p[0:ROT_HALF], p[ROT_HALF:ROT_DIM]
    return jnp.concatenate([x1 * cos - x2 * sin, x2 * cos + x1 * sin, p[ROT_DIM:]], axis=0)


def _dot_nt(a, b):
    return lax.dot_general(a, b, (((1,), (1,)), ((), ())), preferred_element_type=F32)


def _in_proj_kernel(x_ref, mod_ref, gpre_ref, wuv_ref, wtm_ref, wti_hi_ref, wti_lo_ref, cos_ref, sin_ref,
                    gv_ref, wsp_ref, bsp_ref, gosgu_ref,
                    ya_ref, qt_ref, k_ref, vt_ref, qi_ref, ki_ref, wt_ref, kn_ref):
    d_sgu = SGU_GROUPS * SGU_GROUP_DIM
    nq = N_HEADS * HEAD_DIM
    nkv = N_KV_HEADS * HEAD_DIM
    nqi = IDX_HEADS * IDX_DIM
    sh1 = mod_ref[0, 0:1, :]
    sc1 = mod_ref[0, 1:2, :]
    row = lax.broadcasted_iota(jnp.int32, (CHUNK, CHUNK), 0)
    col = lax.broadcasted_iota(jnp.int32, (CHUNK, CHUNK), 1)
    w_mix = [jnp.where(col <= row, wsp_ref[g], 0.0).astype(BF16) for g in range(SGU_GROUPS)]

    def project(t0, tn):
        tok = slice(t0, t0 + tn)
        h = _rms(x_ref[0, tok, :], gpre_ref[...]) * (1.0 + sc1) + sh1
        h_hi32, h_lo32 = _split_bf16(h)
        h_hi, h_lo = h_hi32.astype(BF16), h_lo32.astype(BF16)
        cos, sin = cos_ref[0, :, tok], sin_ref[0, :, tok]

        puv = jnp.dot(h_hi, wuv_ref[...], preferred_element_type=F32)
        pti = _dot_nt(wti_hi_ref[...], h_hi) + _dot_nt(wti_lo_ref[...], h_hi) + _dot_nt(wti_hi_ref[...], h_lo)
        ptm = _dot_nt(wtm_ref[...], h_hi)

        zu = jax.nn.gelu(puv[:, :d_sgu])
        zv = jax.nn.gelu(puv[:, d_sgu:])
        ya_groups = []
        for g in range(SGU_GROUPS):
            sl = slice(g * SGU_GROUP_DIM, (g + 1) * SGU_GROUP_DIM)
            vn = _rms(zv[:, sl], gv_ref[:, sl]).astype(BF16)
            parts = []
            for ci in range(tn // CHUNK):
                rs = slice(ci * CHUNK, (ci + 1) * CHUNK)
                mixed = jnp.dot(w_mix[g], vn[rs], preferred_element_type=F32) + bsp_ref[g]
                parts.append(zu[rs, sl] * mixed)
            ya_groups.append(jnp.concatenate(parts, axis=0))
        ya = jnp.concatenate(ya_groups, axis=1)
        ya_ref[0, tok, :] = _rms(ya, gosgu_ref[...]).astype(BF16)

        for hh in range(N_HEADS):
            rs = slice(hh * HEAD_DIM, (hh + 1) * HEAD_DIM)
            qt_ref[0, rs, tok] = (_rope_rows(ptm[rs], cos, sin) * Q_SCALE).astype(BF16)
        kt = jnp.concatenate(
            [_rope_rows(ptm[nq + g * HEAD_DIM:nq + (g + 1) * HEAD_DIM], cos, sin) for g in range(N_KV_HEADS)],
            axis=0)
        k_ref[0, tok, :] = kt.T.astype(BF16)
        kb = kt.astype(BF16).astype(F32)
        kn_ref[0, :, tok] = jnp.concatenate(
            [jnp.sum(jnp.square(kb[g * HEAD_DIM:(g + 1) * HEAD_DIM]), axis=0, keepdims=True)
             for g in range(N_KV_HEADS)], axis=0)
        vt = ptm[nq + nkv:nq + 2 * nkv].astype(BF16)
        for ci in range(tn // ATT_CH):
            vt_ref[0, t0 // ATT_CH + ci] = vt[:, ci * ATT_CH:(ci + 1) * ATT_CH]

        for hh in range(IDX_HEADS):
            r = _rope_rows(pti[hh * IDX_DIM:(hh + 1) * IDX_DIM], cos, sin)
            r_hi, r_lo = _split_bf16(r)
            qi_ref[0, hh, :, tok] = jnp.concatenate([r_hi, r_lo], axis=0).astype(BF16)
        kr = _rope_rows(pti[nqi:nqi + IDX_DIM], cos, sin)
        k_hi, k_lo = _split_bf16(kr)
        ki_ref[0, tok, :] = jnp.concatenate([k_hi, k_lo, k_hi, k_lo], axis=0).T.astype(BF16)
        wt_ref[0, :, tok] = (pti[nqi + IDX_DIM:nqi + IDX_DIM + IDX_HEADS]
                             * ((IDX_HEADS ** -0.5) * (IDX_DIM ** -0.5)))

    tm = x_ref.shape[1]
    for t0 in range(0, tm, tm // 2):
        project(t0, tm // 2)


def _in_proj(x, mod3, g_pre, w_uv, wt_main, wt_idx_hi, wt_idx_lo, cos_t, sin_t, g_v, w_sp, b_sp, g_osgu):
    bsz, s, d = x.shape
    tm = PROJ_TM
    d_sgu = SGU_GROUPS * SGU_GROUP_DIM
    nq = N_HEADS * HEAD_DIM
    nkv = N_KV_HEADS * HEAD_DIM
    const2 = lambda b, i: (0, 0)
    const3 = lambda b, i: (0, 0, 0)
    return pl.pallas_call(
        _in_proj_kernel,
        grid=(bsz, s // tm),
        in_specs=[
            pl.BlockSpec((1, tm, d), lambda b, i: (b, i, 0)),
            pl.BlockSpec((1, 6, d), lambda b, i: (b, 0, 0)),
            pl.BlockSpec((1, d), const2),
            pl.BlockSpec(w_uv.shape, const2),
            pl.BlockSpec(wt_main.shape, const2),
            pl.BlockSpec(wt_idx_hi.shape, const2),
            pl.BlockSpec(wt_idx_lo.shape, const2),
            pl.BlockSpec((1, ROT_HALF, tm), lambda b, i: (b, 0, i)),
            pl.BlockSpec((1, ROT_HALF, tm), lambda b, i: (b, 0, i)),
            pl.BlockSpec((1, d_sgu), const2),
            pl.BlockSpec(w_sp.shape, const3),
            pl.BlockSpec(b_sp.shape, const3),
            pl.BlockSpec((1, d_sgu), const2),
        ],
        out_specs=[
            pl.BlockSpec((1, tm, d_sgu), lambda b, i: (b, i, 0)),
            pl.BlockSpec((1, nq, tm), lambda b, i: (b, 0, i)),
            pl.BlockSpec((1, tm, nkv), lambda b, i: (b, i, 0)),
            pl.BlockSpec((1, tm // ATT_CH, nkv, ATT_CH), lambda b, i: (b, i, 0, 0)),
            pl.BlockSpec((1, IDX_HEADS, 2 * IDX_DIM, tm), lambda b, i: (b, 0, 0, i)),
            pl.BlockSpec((1, tm, IDX_K), lambda b, i: (b, i, 0)),
            pl.BlockSpec((1, IDX_HEADS, tm), lambda b, i: (b, 0, i)),
            pl.BlockSpec((1, N_KV_HEADS, tm), lambda b, i: (b, 0, i)),
        ],
        out_shape=[
            jax.ShapeDtypeStruct((bsz, s, d_sgu), BF16),
            jax.ShapeDtypeStruct((bsz, nq, s), BF16),
            jax.ShapeDtypeStruct((bsz, s, nkv), BF16),
            jax.ShapeDtypeStruct((bsz, s // ATT_CH, nkv, ATT_CH), BF16),
            jax.ShapeDtypeStruct((bsz, IDX_HEADS, 2 * IDX_DIM, s), BF16),
            jax.ShapeDtypeStruct((bsz, s, IDX_K), BF16),
            jax.ShapeDtypeStruct((bsz, IDX_HEADS, s), F32),
            jax.ShapeDtypeStruct((bsz, N_KV_HEADS, s), F32),
        ],
        compiler_params=pltpu.CompilerParams(
            dimension_semantics=("arbitrary", "arbitrary"), vmem_limit_bytes=VMEM_LIMIT_BYTES),
        name="in_proj",
    )(x, mod3, g_pre, w_uv, wt_main, wt_idx_hi, wt_idx_lo, cos_t, sin_t, g_v, w_sp, b_sp, g_osgu)


def _attn_kernel(qi_ref, w_ref, ki_ref, q_ref, k_ref, vt_ref, kn_ref, g_ref, o_ref,
                 sc_ref, lg_ref, p_ref, qi4_ref, qpad_ref, m_ref, acc_ref, *, topk):
    tq, ch = ATT_TQ, ATT_CH
    j = pl.program_id(1)
    nch = j + 1
    q_pos = j * tq + lax.broadcasted_iota(jnp.int32, (1, tq), 1)
    row_iota = lax.broadcasted_iota(jnp.int32, (ch, tq), 0)

    def for_chunks(body, init, n=nch):
        def pair(kp, carry):
            return body(2 * kp + 1, body(2 * kp, carry))
        carry = lax.fori_loop(0, n // 2, pair, init)
        return lax.cond(n % 2 == 1, lambda c: body(n - 1, c), lambda c: c, carry)

    def fold8(a, op):
        return op(a.reshape(ch // 8, 8, tq), axis=0)

    for hh in range(IDX_HEADS):
        q_hi, q_lo = qi_ref[0, hh, 0:IDX_DIM, :], qi_ref[0, hh, IDX_DIM:, :]
        qi4_ref[hh] = jnp.concatenate([q_hi, q_hi, q_lo, q_lo], axis=0)

    def score_chunk(kc, carry, diagonal):
        mn, mx, n_pos, n_nonneg = carry
        kic = ki_ref[0, pl.ds(pl.multiple_of(kc * ch, ch), ch), :]
        acc = jnp.zeros((ch, tq), F32)
        for hh in range(IDX_HEADS):
            dd = jnp.dot(kic, qi4_ref[hh], preferred_element_type=F32)
            acc = acc + w_ref[0, hh:hh + 1, :] * jnp.maximum(dd, 0.0)
        if diagonal:
            causal = (row_iota + kc * ch) <= q_pos
            sc = jnp.where(causal, acc, -jnp.inf)
            mn = jnp.minimum(mn, fold8(jnp.where(causal, acc, jnp.inf), jnp.min))
        else:
            sc = acc
            mn = jnp.minimum(mn, fold8(acc, jnp.min))
        sc_ref[kc] = sc
        mx = jnp.maximum(mx, fold8(sc, jnp.max))
        n_pos = n_pos + fold8(jnp.where(sc > 0.0, 1.0, 0.0), jnp.sum)
        n_nonneg = n_nonneg + fold8(jnp.where(sc >= 0.0, 1.0, 0.0), jnp.sum)
        return mn, mx, n_pos, n_nonneg

    stats = for_chunks(
        functools.partial(score_chunk, diagonal=False),
        (jnp.full((8, tq), jnp.inf, F32), jnp.full((8, tq), -jnp.inf, F32),
         jnp.zeros((8, tq), F32), jnp.zeros((8, tq), F32)), j)
    mn, mx, c0_gt, c0_ge = score_chunk(j, stats, diagonal=True)
    mn = jnp.min(mn, axis=0, keepdims=True)
    mx = jnp.max(mx, axis=0, keepdims=True)
    c0_gt = jnp.sum(c0_gt, axis=0, keepdims=True)
    c0_ge = jnp.sum(c0_ge, axis=0, keepdims=True)

    npair = (nch + 1) // 2

    @pl.when(nch % 2 == 1)
    def _():
        sc_ref[nch] = jnp.full((ch, tq), -jnp.inf, F32)

    def count(pred):
        def body(kp, c):
            for half in range(2):
                c = c + fold8(jnp.where(pred(sc_ref[2 * kp + half]), 1.0, 0.0), jnp.sum)
            return c
        c8 = lax.fori_loop(0, npair, body, jnp.zeros((8, tq), F32))
        return jnp.sum(c8, axis=0, keepdims=True)

    n_causal = (q_pos + 1).astype(F32)
    kk = jnp.minimum(float(topk), n_causal)
    at_zero = (c0_gt < kk) & (c0_ge >= kk)
    take_all = n_causal == kk
    pos = c0_gt >= kk
    lo0 = jnp.where(pos, jnp.maximum(mn, 0.0), mn)
    clo0 = jnp.where(pos & (mn <= 0.0), c0_ge, n_causal)
    hi0 = jnp.where(pos, jnp.minimum(2.0 * mx + TINY, F32_MAX), 0.0)
    thr0 = jnp.where(take_all, mn, 0.0)
    tied0 = jnp.logical_not(take_all) & at_zero & (c0_ge > kk)
    done0 = at_zero | take_all

    def bisect_step(state):
        lo, clo, hi, thr, done, tied = state
        mid = 0.5 * lo + 0.5 * hi
        valid = (mid > lo) & (mid < hi)
        c = count(lambda s: s >= mid)
        move_lo = valid & (c >= kk)
        thr = jnp.where(done, thr, jnp.where(valid, mid, lo))
        tied = tied | (jnp.logical_not(done) & jnp.logical_not(valid) & (clo > kk))
        done = done | jnp.logical_not(valid) | (c == kk)
        hi = jnp.where(valid & (c < kk), mid, hi)
        lo = jnp.where(move_lo, mid, lo)
        clo = jnp.where(move_lo, c, clo)
        return lo, clo, hi, thr, done, tied

    def as_f(m):
        return jnp.where(m, 1.0, 0.0)

    def bisect_round(state):
        lo, clo, hi, thr, done_f, tied_f = state
        st = (lo, clo, hi, thr, done_f > 0.5, tied_f > 0.5)
        for _ in range(BISECT_UNROLL):
            st = bisect_step(st)
        lo, clo, hi, thr, done, tied = st
        return lo, clo, hi, thr, as_f(done), as_f(tied)

    def bisect_body(carry):
        state = bisect_round(carry[0])
        return state, jnp.max(1.0 - state[4])

    blind = jnp.where((j + 1) * tq <= topk, 0, BISECT_BLIND_ROUNDS)
    state = lax.fori_loop(0, blind, lambda _, st: bisect_round(st),
                          (lo0, clo0, hi0, thr0, as_f(done0), as_f(tied0)))
    (_, _, _, thr, _, tied_f), _ = lax.while_loop(
        lambda carry: carry[1] > 0.0, bisect_body, (state, jnp.max(1.0 - state[4])))

    tie = jnp.max(tied_f) > 0.0

    @pl.when(jnp.logical_not(tie))
    def _():
        def body(kc, _):
            sc_ref[kc] = jnp.where(sc_ref[kc] >= thr, 0.0, NEG)
            return 0
        lax.fori_loop(0, nch, body, 0)

    @pl.when(tie)
    def _():
        r = lax.broadcasted_iota(jnp.int32, (ch, ch), 0)
        c = lax.broadcasted_iota(jnp.int32, (ch, ch), 1)
        below = jnp.where(c < r, 1.0, 0.0).astype(BF16)
        need = kk - count(lambda s: s > thr)

        def body(kc, seen):
            s = sc_ref[kc]
            eq = jnp.where(s == thr, 1.0, 0.0)
            rank = jnp.dot(below, eq.astype(BF16), preferred_element_type=F32) + seen
            sel = (s > thr) | ((s == thr) & (rank < need))
            sc_ref[kc] = jnp.where(sel, 0.0, NEG)
            return seen + jnp.sum(eq, axis=0, keepdims=True)
        lax.fori_loop(0, nch, body, jnp.zeros((1, tq), F32))

    zq = jnp.zeros((HEAD_DIM, tq), BF16)
    for hh in range(N_HEADS):
        qh = q_ref[0, hh * HEAD_DIM:(hh + 1) * HEAD_DIM, :]
        qpad_ref[hh] = jnp.concatenate([qh, zq] if hh < GQA_GROUP else [zq, qh], axis=0)

    acc_ref[...] = jnp.zeros(acc_ref.shape, F32)
    ones = jnp.ones((ONES_ROWS, ch), BF16)

    q_sq = jnp.max(jnp.sum(jnp.square(q_ref[0].astype(F32)).reshape(N_HEADS, HEAD_DIM, tq), axis=1))
    bounded = q_sq * jnp.max(kn_ref[0]) <= LOGIT_BOUND * LOGIT_BOUND

    def key_chunk(kc):
        return k_ref[0, pl.ds(pl.multiple_of(kc * ch, ch), ch), :]

    def pv_accumulate(kc, probs):
        vt = vt_ref[0, kc]
        for g in range(N_KV_HEADS):
            vaug = jnp.concatenate([vt[g * HEAD_DIM:(g + 1) * HEAD_DIM], ones], axis=0)
            for hh in range(g * GQA_GROUP, (g + 1) * GQA_GROUP):
                acc_ref[hh] += jnp.dot(vaug, probs(hh), preferred_element_type=F32)

    @pl.when(bounded)
    def _():
        def prob_chunk(kc):
            kcs = key_chunk(kc)
            bias = sc_ref[kc]
            for hh in range(N_HEADS):
                s = jnp.dot(kcs, qpad_ref[hh], preferred_element_type=F32) + bias
                p_ref[hh, kc] = jnp.exp2(s).astype(BF16)

        def pv_chunk(kc):
            pv_accumulate(kc, lambda hh: p_ref[hh, kc])

        npair_full = nch // 2

        @pl.when(npair_full > 0)
        def _():
            prob_chunk(0)
            prob_chunk(1)

            def body(t, _):
                pv_chunk(2 * t - 2)
                pv_chunk(2 * t - 1)
                prob_chunk(2 * t)
                prob_chunk(2 * t + 1)
                return 0
            lax.fori_loop(1, npair_full, body, 0)
            pv_chunk(2 * npair_full - 2)
            pv_chunk(2 * npair_full - 1)

        @pl.when(nch % 2 == 1)
        def _():
            prob_chunk(nch - 1)
            pv_chunk(nch - 1)

    @pl.when(jnp.logical_not(bounded))
    def _():
        def logits_chunk(kc, ms):
            kcs = key_chunk(kc)
            bias = sc_ref[kc]
            out = []
            for hh in range(N_HEADS):
                s = jnp.dot(kcs, qpad_ref[hh], preferred_element_type=F32) + bias
                lg_ref[hh, kc] = s
                out.append(jnp.maximum(ms[hh], fold8(s, jnp.max)))
            return tuple(out)

        ms = for_chunks(logits_chunk, tuple(jnp.full((8, tq), NEG, F32) for _ in range(N_HEADS)))
        for hh in range(N_HEADS):
            m_ref[hh] = jnp.max(ms[hh], axis=0, keepdims=True)

        def pv_chunk(kc, _):
            pv_accumulate(kc, lambda hh: jnp.exp2(lg_ref[hh, kc] - m_ref[hh]).astype(BF16))
            return 0
        for_chunks(pv_chunk, 0)

    outs = []
    for hh in range(N_HEADS):
        a = acc_ref[hh]
        outs.append(a[0:HEAD_DIM] * (1.0 / a[HEAD_DIM:HEAD_DIM + 1]))
    y = jnp.concatenate(outs, axis=0)
    yn = y * lax.rsqrt(jnp.mean(y * y, axis=0, keepdims=True) + EPS) * g_ref[...]
    o_ref[0] = yn.T.astype(BF16)


def _dsa_attn(qi3t, wt, ki3, qt, kk, vt, kn, g_attn_b, topk):
    bsz, s, _ = ki3.shape
    tq, ch = ATT_TQ, ATT_CH
    nq = N_HEADS * HEAD_DIM
    nkv = N_KV_HEADS * HEAD_DIM
    return pl.pallas_call(
        functools.partial(_attn_kernel, topk=topk),
        grid=(bsz, s // tq),
        in_specs=[
            pl.BlockSpec((1, IDX_HEADS, 2 * IDX_DIM, tq), lambda b, j: (b, 0, 0, j)),
            pl.BlockSpec((1, IDX_HEADS, tq), lambda b, j: (b, 0, j)),
            pl.BlockSpec((1, s, IDX_K), lambda b, j: (b, 0, 0)),
            pl.BlockSpec((1, nq, tq), lambda b, j: (b, 0, j)),
            pl.BlockSpec((1, s, nkv), lambda b, j: (b, 0, 0)),
            pl.BlockSpec((1, s // ch, nkv, ch), lambda b, j: (b, 0, 0, 0)),
            pl.BlockSpec((1, N_KV_HEADS, s), lambda b, j: (b, 0, 0)),
            pl.BlockSpec((nq, tq), lambda b, j: (0, 0)),
        ],
        out_specs=pl.BlockSpec((1, tq, nq), lambda b, j: (b, j, 0)),
        out_shape=jax.ShapeDtypeStruct((bsz, s, nq), BF16),
        scratch_shapes=[
            pltpu.VMEM((2 * ((s // ch + 1) // 2), ch, tq), F32),
            pltpu.VMEM((N_HEADS, s // ch, ch, tq), F32),
            pltpu.VMEM((N_HEADS, s // ch, ch, tq), BF16),
            pltpu.VMEM((IDX_HEADS, IDX_K, tq), BF16),
            pltpu.VMEM((N_HEADS, 2 * HEAD_DIM, tq), BF16),
            pltpu.VMEM((N_HEADS, 1, tq), F32),
            pltpu.VMEM((N_HEADS, HEAD_DIM + ONES_ROWS, tq), F32),
        ],
        compiler_params=pltpu.CompilerParams(
            dimension_semantics=("arbitrary", "arbitrary"), vmem_limit_bytes=VMEM_LIMIT_BYTES),
        name="dsa_attn",
    )(qi3t, wt, ki3, qt, kk, vt, kn, g_attn_b)


def _out_ffn_kernel(x_ref, ya_ref, yb_ref, mod_ref, wo_ref, gpm_ref, gpf_ref, w1_ref, w2_ref, gpo_ref, o_ref):
    d_a = ya_ref.shape[2]
    tm = x_ref.shape[1]
    g1 = mod_ref[0, 2:3, :]
    sh2 = mod_ref[0, 3:4, :]
    sc2 = mod_ref[0, 4:5, :]
    g2 = mod_ref[0, 5:6, :]

    def mix_residual(rows):
        o = jnp.dot(ya_ref[0, rows, :], wo_ref[0:d_a, :], preferred_element_type=F32)
        o += jnp.dot(yb_ref[0, rows, :], wo_ref[d_a:, :], preferred_element_type=F32)
        return x_ref[0, rows, :] + g1 * _rms(o, gpm_ref[...])

    def mlp(x1):
        h2 = (_rms(x1, gpf_ref[...]) * (1.0 + sc2) + sh2).astype(BF16)
        f = jnp.zeros(x1.shape, F32)
        for c in range(w1_ref.shape[1] // FFN_TF):
            a = jnp.dot(h2, w1_ref[:, c * FFN_TF:(c + 1) * FFN_TF], preferred_element_type=F32)
            a = jnp.square(jnp.maximum(a, 0.0)).astype(BF16)
            f += jnp.dot(a, w2_ref[c * FFN_TF:(c + 1) * FFN_TF, :], preferred_element_type=F32)
        return f

    halves = [pl.ds(i * (tm // 2), tm // 2) for i in range(2)]
    x1 = [mix_residual(rows) for rows in halves]
    f = [mlp(x1h) for x1h in x1]
    for rows, x1h, fh in zip(halves, x1, f):
        o_ref[0, rows, :] = x1h + g2 * _rms(fh, gpo_ref[...])


def _out_ffn(x, ya, yb, mod3, w_out, g_pm, g_pf, w1, w2, g_po):
    bsz, s, d = x.shape
    tm = FFN_TM
    d_a = ya.shape[2]
    const2 = lambda b, i: (0, 0)
    resident = functools.partial(pl.BlockSpec, index_map=const2, pipeline_mode=pl.Buffered(1))
    return pl.pallas_call(
        _out_ffn_kernel,
        grid=(bsz, s // tm),
        in_specs=[
            pl.BlockSpec((1, tm, d), lambda b, i: (b, i, 0)),
            pl.BlockSpec((1, tm, d_a), lambda b, i: (b, i, 0)),
            pl.BlockSpec((1, tm, yb.shape[2]), lambda b, i: (b, i, 0)),
            pl.BlockSpec((1, 6, d), lambda b, i: (b, 0, 0)),
            resident(w_out.shape),
            pl.BlockSpec((1, d), const2),
            pl.BlockSpec((1, d), const2),
            resident(w1.shape),
            resident(w2.shape),
            pl.BlockSpec((1, d), const2),
        ],
        out_specs=pl.BlockSpec((1, tm, d), lambda b, i: (b, i, 0)),
        out_shape=jax.ShapeDtypeStruct((bsz, s, d), F32),
        compiler_params=pltpu.CompilerParams(
            dimension_semantics=("arbitrary", "arbitrary"), vmem_limit_bytes=VMEM_LIMIT_BYTES),
        name="out_ffn",
    )(x, ya, yb, mod3, w_out, g_pm, g_pf, w1, w2, g_po)


def _layer(x, mod3, cos_t, sin_t, g_pre_mix, w_in, g_sgu_v, w_spatial, b_spatial, g_out_sgu, g_out_attn,
           w_out, g_post_mix, g_pre_ffn, w_ff1, w_ff2, g_post_ffn):
    bsz, s, d = x.shape
    d_sgu = SGU_GROUPS * SGU_GROUP_DIM
    nq = N_HEADS * HEAD_DIM
    nkv = N_KV_HEADS * HEAD_DIM
    nqi = IDX_HEADS * IDX_DIM
    topk = min(TOPK_MAX, s // 4)

    o_q = 2 * d_sgu
    o_qi = o_q + nq + 2 * nkv
    w_uv = w_in[:, :o_q].astype(BF16)
    wt_main = w_in[:, o_q:o_qi].T.astype(BF16)
    n_idx = nqi + IDX_DIM + IDX_HEADS
    wt_idx = jnp.pad(w_in[:, o_qi:o_qi + n_idx].T, ((0, (-n_idx) % 16), (0, 0)))
    wt_idx_hi = wt_idx.astype(BF16)
    wt_idx_lo = (wt_idx - wt_idx_hi.astype(F32)).astype(BF16)
    b_sp = jnp.broadcast_to(b_spatial[:, :, None], (SGU_GROUPS, CHUNK, SGU_GROUP_DIM))

    ya, qt, kk, vt, qi3t, ki3, wt, kn = _in_proj(
        x, mod3, g_pre_mix.reshape(1, d), w_uv, wt_main, wt_idx_hi, wt_idx_lo, cos_t, sin_t,
        g_sgu_v.reshape(1, d_sgu), w_spatial, b_sp, g_out_sgu.reshape(1, d_sgu))

    g_attn_b = jnp.broadcast_to(g_out_attn[:, None], (nq, ATT_TQ))
    yb = _dsa_attn(qi3t, wt, ki3, qt, kk, vt, kn, g_attn_b, topk)

    return _out_ffn(x, ya, yb, mod3, w_out.astype(BF16), g_post_mix.reshape(1, d), g_pre_ffn.reshape(1, d),
                    w_ff1.astype(BF16), w_ff2.astype(BF16), g_post_ffn.reshape(1, d))


def kernel(x, c, positions, w_ada, b_ada, g_pre_mix, w_in, g_sgu_v, w_spatial, b_spatial, g_out_sgu, g_out_attn,
           w_out, g_post_mix, g_pre_ffn, w_ff1, w_ff2, g_post_ffn):
    bsz, s, d = x.shape
    inv_freq = ROPE_THETA ** (-jnp.arange(0, ROT_DIM, 2, dtype=F32) / ROT_DIM)
    ang = positions.astype(F32)[:, None, :] * inv_freq[None, :, None]
    cos_t, sin_t = jnp.cos(ang), jnp.sin(ang)
    for l in range(w_ada.shape[0]):
        mod3 = _ada_mod(c, w_ada[l], b_ada[l]).reshape(bsz, 6, d)
        x = _layer(x, mod3, cos_t, sin_t, g_pre_mix[l], w_in[l], g_sgu_v[l], w_spatial[l], b_spatial[l],
                   g_out_sgu[l], g_out_attn[l], w_out[l], g_post_mix[l], g_pre_ffn[l], w_ff1[l], w_ff2[l],
                   g_post_ffn[l])
    return x
```

```python
import functools

import jax
import jax.numpy as jnp
from jax import lax
from jax.experimental import pallas as pl
from jax.experimental.pallas import tpu as pltpu

F32 = jnp.float32
BF16 = jnp.bfloat16

SGU_GROUPS = 4
SGU_GROUP_DIM = 128
CHUNK = 128
N_HEADS = 8
HEAD_DIM = 64
N_KV_HEADS = 2
GQA_GROUP = N_HEADS // N_KV_HEADS
ROT_DIM = 16
ROT_HALF = ROT_DIM // 2
ROPE_THETA = 500000.0
IDX_HEADS = 8
IDX_DIM = 64
TOPK_MAX = 256
EPS = 1e-6

ADA_TN = 1024
PROJ_TM = 512
ATT_TQ = 256
ATT_CH = 256
FFN_TM = 512
FFN_TF = 1024
BISECT_UNROLL = 4
BISECT_BLIND_ROUNDS = 4
BISECT_CHECKED_STEPS = 2
IDX_K = 4 * IDX_DIM
ONES_ROWS = 16
NEG = -1e30
TINY = 1e-30
F32_MAX = 3.4028235e38
LOGIT_BOUND = 96.0
VMEM_LIMIT_BYTES = 56 * 1024 * 1024
Q_SCALE = (HEAD_DIM ** -0.5) * 1.4426950408889634


def _split_bf16(a):
    hi = a.astype(BF16).astype(F32)
    lo = (a - hi).astype(BF16).astype(F32)
    return hi, lo


def _rms(a, g):
    return a * lax.rsqrt(jnp.mean(a * a, axis=-1, keepdims=True) + EPS) * g


def _ada_kernel(c_ref, w_ref, b_ref, o_ref):
    c = c_ref[...]
    ca = c * (1.0 / (1.0 + jnp.exp(-c)))
    c_hi, c_lo = _split_bf16(ca)
    w_hi, w_lo = _split_bf16(w_ref[...])
    c_hi, c_lo, w_hi, w_lo = (t.astype(BF16) for t in (c_hi, c_lo, w_hi, w_lo))
    acc = jnp.dot(c_hi, w_hi, preferred_element_type=F32)
    acc += jnp.dot(c_hi, w_lo, preferred_element_type=F32)
    acc += jnp.dot(c_lo, w_hi, preferred_element_type=F32)
    o_ref[...] = acc + b_ref[...]


def _ada_mod(c, w_ada, b_ada):
    bsz, d = c.shape
    n = w_ada.shape[1]
    return pl.pallas_call(
        _ada_kernel,
        grid=(n // ADA_TN,),
        in_specs=[
            pl.BlockSpec((bsz, d), lambda i: (0, 0)),
            pl.BlockSpec((d, ADA_TN), lambda i: (0, i)),
            pl.BlockSpec((1, ADA_TN), lambda i: (0, i)),
        ],
        out_specs=pl.BlockSpec((bsz, ADA_TN), lambda i: (0, i)),
        out_shape=jax.ShapeDtypeStruct((bsz, n), F32),
        compiler_params=pltpu.CompilerParams(dimension_semantics=("arbitrary",)),
        name="ada_mod",
    )(c, w_ada, b_ada.reshape(1, n))


def _rope_rows(p, cos, sin):
    x1, x2 = p[0:ROT_HALF], p[ROT_HALF:ROT_DIM]
    return jnp.concatenate([x1 * cos - x2 * sin, x2 * cos + x1 * sin, p[ROT_DIM:]], axis=0)


def _dot_nt(a, b):
    return lax.dot_general(a, b, (((1,), (1,)), ((), ())), preferred_element_type=F32)


def _in_proj_kernel(x_ref, mod_ref, gpre_ref, wuv_ref, wtm_ref, wti_hi_ref, wti_lo_ref, cos_ref, sin_ref,
                    gv_ref, wsp_ref, bsp_ref, gosgu_ref,
                    ya_ref, qt_ref, k_ref, vt_ref, qi_ref, ki_ref, wt_ref, kn_ref):
    d_sgu = SGU_GROUPS * SGU_GROUP_DIM
    nq = N_HEADS * HEAD_DIM
    nkv = N_KV_HEADS * HEAD_DIM
    nqi = IDX_HEADS * IDX_DIM
    sh1 = mod_ref[0, 0:1, :]
    sc1 = mod_ref[0, 1:2, :]
    row = lax.broadcasted_iota(jnp.int32, (CHUNK, CHUNK), 0)
    col = lax.broadcasted_iota(jnp.int32, (CHUNK, CHUNK), 1)
    w_mix = [jnp.where(col <= row, wsp_ref[g], 0.0).astype(BF16) for g in range(SGU_GROUPS)]

    def project(t0, tn):
        tok = slice(t0, t0 + tn)
        h = _rms(x_ref[0, tok, :], gpre_ref[...]) * (1.0 + sc1) + sh1
        h_hi32, h_lo32 = _split_bf16(h)
        h_hi, h_lo = h_hi32.astype(BF16), h_lo32.astype(BF16)
        cos, sin = cos_ref[0, :, tok], sin_ref[0, :, tok]

        puv = jnp.dot(h_hi, wuv_ref[...], preferred_element_type=F32)
        pti = _dot_nt(wti_hi_ref[...], h_hi) + _dot_nt(wti_lo_ref[...], h_hi) + _dot_nt(wti_hi_ref[...], h_lo)
        ptm = _dot_nt(wtm_ref[...], h_hi)

        zu = jax.nn.gelu(puv[:, :d_sgu])
        zv = jax.nn.gelu(puv[:, d_sgu:])
        ya_groups = []
        for g in range(SGU_GROUPS):
            sl = slice(g * SGU_GROUP_DIM, (g + 1) * SGU_GROUP_DIM)
            vn = _rms(zv[:, sl], gv_ref[:, sl]).astype(BF16)
            parts = []
            for ci in range(tn // CHUNK):
                rs = slice(ci * CHUNK, (ci + 1) * CHUNK)
                mixed = jnp.dot(w_mix[g], vn[rs], preferred_element_type=F32) + bsp_ref[g]
                parts.append(zu[rs, sl] * mixed)
            ya_groups.append(jnp.concatenate(parts, axis=0))
        ya = jnp.concatenate(ya_groups, axis=1)
        ya_ref[0, tok, :] = _rms(ya, gosgu_ref[...]).astype(BF16)

        for hh in range(N_HEADS):
            rs = slice(hh * HEAD_DIM, (hh + 1) * HEAD_DIM)
            qt_ref[0, rs, tok] = (_rope_rows(ptm[rs], cos, sin) * Q_SCALE).astype(BF16)
        kt = jnp.concatenate(
            [_rope_rows(ptm[nq + g * HEAD_DIM:nq + (g + 1) * HEAD_DIM], cos, sin) for g in range(N_KV_HEADS)],
            axis=0)
        k_ref[0, tok, :] = kt.T.astype(BF16)
        kb = kt.astype(BF16).astype(F32)
        kn_ref[0, :, tok] = jnp.concatenate(
            [jnp.sum(jnp.square(kb[g * HEAD_DIM:(g + 1) * HEAD_DIM]), axis=0, keepdims=True)
             for g in range(N_KV_HEADS)], axis=0)
        vt = ptm[nq + nkv:nq + 2 * nkv].astype(BF16)
        for ci in range(tn // ATT_CH):
            vt_ref[0, t0 // ATT_CH + ci] = vt[:, ci * ATT_CH:(ci + 1) * ATT_CH]

        for hh in range(IDX_HEADS):
            r = _rope_rows(pti[hh * IDX_DIM:(hh + 1) * IDX_DIM], cos, sin)
            r_hi, r_lo = _split_bf16(r)
            qi_ref[0, hh, :, tok] = jnp.concatenate([r_hi, r_lo], axis=0).astype(BF16)
        kr = _rope_rows(pti[nqi:nqi + IDX_DIM], cos, sin)
        k_hi, k_lo = _split_bf16(kr)
        ki_ref[0, tok, :] = jnp.concatenate([k_hi, k_lo, k_hi, k_lo], axis=0).T.astype(BF16)
        wt_ref[0, :, tok] = (pti[nqi + IDX_DIM:nqi + IDX_DIM + IDX_HEADS]
                             * ((IDX_HEADS ** -0.5) * (IDX_DIM ** -0.5)))

    tm = x_ref.shape[1]
    for t0 in range(0, tm, tm // 2):
        project(t0, tm // 2)


def _in_proj(x, mod3, g_pre, w_uv, wt_main, wt_idx_hi, wt_idx_lo, cos_t, sin_t, g_v, w_sp, b_sp, g_osgu):
    bsz, s, d = x.shape
    tm = PROJ_TM
    d_sgu = SGU_GROUPS * SGU_GROUP_DIM
    nq = N_HEADS * HEAD_DIM
    nkv = N_KV_HEADS * HEAD_DIM
    const2 = lambda b, i: (0, 0)
    const3 = lambda b, i: (0, 0, 0)
    return pl.pallas_call(
        _in_proj_kernel,
        grid=(bsz, s // tm),
        in_specs=[
            pl.BlockSpec((1, tm, d), lambda b, i: (b, i, 0)),
            pl.BlockSpec((1, 6, d), lambda b, i: (b, 0, 0)),
            pl.BlockSpec((1, d), const2),
            pl.BlockSpec(w_uv.shape, const2),
            pl.BlockSpec(wt_main.shape, const2),
            pl.BlockSpec(wt_idx_hi.shape, const2),
            pl.BlockSpec(wt_idx_lo.shape, const2),
            pl.BlockSpec((1, ROT_HALF, tm), lambda b, i: (b, 0, i)),
            pl.BlockSpec((1, ROT_HALF, tm), lambda b, i: (b, 0, i)),
            pl.BlockSpec((1, d_sgu), const2),
            pl.BlockSpec(w_sp.shape, const3),
            pl.BlockSpec(b_sp.shape, const3),
            pl.BlockSpec((1, d_sgu), const2),
        ],
        out_specs=[
            pl.BlockSpec((1, tm, d_sgu), lambda b, i: (b, i, 0)),
            pl.BlockSpec((1, nq, tm), lambda b, i: (b, 0, i)),
            pl.BlockSpec((1, tm, nkv), lambda b, i: (b, i, 0)),
            pl.BlockSpec((1, tm // ATT_CH, nkv, ATT_CH), lambda b, i: (b, i, 0, 0)),
            pl.BlockSpec((1, IDX_HEADS, 2 * IDX_DIM, tm), lambda b, i: (b, 0, 0, i)),
            pl.BlockSpec((1, tm, IDX_K), lambda b, i: (b, i, 0)),
            pl.BlockSpec((1, IDX_HEADS, tm), lambda b, i: (b, 0, i)),
            pl.BlockSpec((1, N_KV_HEADS, tm), lambda b, i: (b, 0, i)),
        ],
        out_shape=[
            jax.ShapeDtypeStruct((bsz, s, d_sgu), BF16),
            jax.ShapeDtypeStruct((bsz, nq, s), BF16),
            jax.ShapeDtypeStruct((bsz, s, nkv), BF16),
            jax.ShapeDtypeStruct((bsz, s // ATT_CH, nkv, ATT_CH), BF16),
            jax.ShapeDtypeStruct((bsz, IDX_HEADS, 2 * IDX_DIM, s), BF16),
            jax.ShapeDtypeStruct((bsz, s, IDX_K), BF16),
            jax.ShapeDtypeStruct((bsz, IDX_HEADS, s), F32),
            jax.ShapeDtypeStruct((bsz, N_KV_HEADS, s), F32),
        ],
        compiler_params=pltpu.CompilerParams(
            dimension_semantics=("arbitrary", "arbitrary"), vmem_limit_bytes=VMEM_LIMIT_BYTES),
        name="in_proj",
    )(x, mod3, g_pre, w_uv, wt_main, wt_idx_hi, wt_idx_lo, cos_t, sin_t, g_v, w_sp, b_sp, g_osgu)


def _attn_kernel(qi_ref, w_ref, ki_ref, q_ref, k_ref, vt_ref, kn_ref, g_ref, o_ref,
                 sc_ref, lg_ref, p_ref, qi4_ref, qpad_ref, m_ref, acc_ref, *, topk):
    tq, ch = ATT_TQ, ATT_CH
    j = pl.program_id(1)
    nch = j + 1
    q_pos = j * tq + lax.broadcasted_iota(jnp.int32, (1, tq), 1)
    row_iota = lax.broadcasted_iota(jnp.int32, (ch, tq), 0)

    def for_chunks(body, init, n=nch):
        def pair(kp, carry):
            return body(2 * kp + 1, body(2 * kp, carry))
        carry = lax.fori_loop(0, n // 2, pair, init)
        return lax.cond(n % 2 == 1, lambda c: body(n - 1, c), lambda c: c, carry)

    def fold8(a, op):
        return op(a.reshape(ch // 8, 8, tq), axis=0)

    for hh in range(IDX_HEADS):
        q_hi, q_lo = qi_ref[0, hh, 0:IDX_DIM, :], qi_ref[0, hh, IDX_DIM:, :]
        qi4_ref[hh] = jnp.concatenate([q_hi, q_hi, q_lo, q_lo], axis=0)

    def score_chunk(kc, carry, diagonal):
        mn, mx, n_pos, n_nonneg = carry
        kic = ki_ref[0, pl.ds(pl.multiple_of(kc * ch, ch), ch), :]
        acc = jnp.zeros((ch, tq), F32)
        for hh in range(IDX_HEADS):
            dd = jnp.dot(kic, qi4_ref[hh], preferred_element_type=F32)
            acc = acc + w_ref[0, hh:hh + 1, :] * jnp.maximum(dd, 0.0)
        if diagonal:
            causal = (row_iota + kc * ch) <= q_pos
            sc = jnp.where(causal, acc, -jnp.inf)
            mn = jnp.minimum(mn, fold8(jnp.where(causal, acc, jnp.inf), jnp.min))
        else:
            sc = acc
            mn = jnp.minimum(mn, fold8(acc, jnp.min))
        sc_ref[kc] = sc
        mx = jnp.maximum(mx, fold8(sc, jnp.max))
        n_pos = n_pos + fold8(jnp.where(sc > 0.0, 1.0, 0.0), jnp.sum)
        n_nonneg = n_nonneg + fold8(jnp.where(sc >= 0.0, 1.0, 0.0), jnp.sum)
        return mn, mx, n_pos, n_nonneg

    stats = for_chunks(
        functools.partial(score_chunk, diagonal=False),
        (jnp.full((8, tq), jnp.inf, F32), jnp.full((8, tq), -jnp.inf, F32),
         jnp.zeros((8, tq), F32), jnp.zeros((8, tq), F32)), j)
    mn, mx, c0_gt, c0_ge = score_chunk(j, stats, diagonal=True)
    mn = jnp.min(mn, axis=0, keepdims=True)
    mx = jnp.max(mx, axis=0, keepdims=True)
    c0_gt = jnp.sum(c0_gt, axis=0, keepdims=True)
    c0_ge = jnp.sum(c0_ge, axis=0, keepdims=True)

    npair = (nch + 1) // 2

    @pl.when(nch % 2 == 1)
    def _():
        sc_ref[nch] = jnp.full((ch, tq), -jnp.inf, F32)

    def count(pred):
        def body(kp, c):
            for half in range(2):
                c = c + fold8(jnp.where(pred(sc_ref[2 * kp + half]), 1.0, 0.0), jnp.sum)
            return c
        c8 = lax.fori_loop(0, npair, body, jnp.zeros((8, tq), F32))
        return jnp.sum(c8, axis=0, keepdims=True)

    n_causal = (q_pos + 1).astype(F32)
    kk = jnp.minimum(float(topk), n_causal)
    at_zero = (c0_gt < kk) & (c0_ge >= kk)
    take_all = n_causal == kk
    pos = c0_gt >= kk
    lo0 = jnp.where(pos, jnp.maximum(mn, 0.0), mn)
    clo0 = jnp.where(pos & (mn <= 0.0), c0_ge, n_causal)
    hi0 = jnp.where(pos, jnp.minimum(2.0 * mx + TINY, F32_MAX), 0.0)
    thr0 = jnp.where(take_all, mn, 0.0)
    tied0 = jnp.logical_not(take_all) & at_zero & (c0_ge > kk)
    done0 = at_zero | take_all

    def bisect_step(state):
        lo, clo, hi, thr, done, tied = state
        mid = 0.5 * lo + 0.5 * hi
        valid = (mid > lo) & (mid < hi)
        c = count(lambda s: s >= mid)
        move_lo = valid & (c >= kk)
        thr = jnp.where(done, thr, jnp.where(valid, mid, lo))
        tied = tied | (jnp.logical_not(done) & jnp.logical_not(valid) & (clo > kk))
        done = done | jnp.logical_not(valid) | (c == kk)
        hi = jnp.where(valid & (c < kk), mid, hi)
        lo = jnp.where(move_lo, mid, lo)
        clo = jnp.where(move_lo, c, clo)
        return lo, clo, hi, thr, done, tied

    def as_f(m):
        return jnp.where(m, 1.0, 0.0)

    def bisect_round(state, steps):
        lo, clo, hi, thr, done_f, tied_f = state
        st = (lo, clo, hi, thr, done_f > 0.5, tied_f > 0.5)
        for _ in range(steps):
            st = bisect_step(st)
        lo, clo, hi, thr, done, tied = st
        return lo, clo, hi, thr, as_f(done), as_f(tied)

    def bisect_body(carry):
        state = bisect_round(carry[0], BISECT_CHECKED_STEPS)
        return state, jnp.max(1.0 - state[4])

    blind = jnp.where((j + 1) * tq <= topk, 0, BISECT_BLIND_ROUNDS)
    state = lax.fori_loop(0, blind, lambda _, st: bisect_round(st, BISECT_UNROLL),
                          (lo0, clo0, hi0, thr0, as_f(done0), as_f(tied0)))
    (_, _, _, thr, _, tied_f), _ = lax.while_loop(
        lambda carry: carry[1] > 0.0, bisect_body, (state, jnp.where(blind > 0, 1.0, 0.0)))

    q_sq = jnp.sum(jnp.square(q_ref[0].astype(F32)).reshape(N_HEADS, HEAD_DIM, tq), axis=1)
    q_sq = jnp.max(jnp.max(q_sq, axis=0, keepdims=True), axis=1, keepdims=True)
    k_sq = jnp.max(jnp.max(kn_ref[0], axis=0, keepdims=True), axis=1, keepdims=True)
    bounded_f = jnp.where(q_sq * k_sq <= LOGIT_BOUND * LOGIT_BOUND, 2.0, 0.0)

    code = jnp.max(tied_f + bounded_f)
    tie = jnp.logical_or(code == 1.0, code == 3.0)
    bounded = code >= 2.0


    @pl.when(jnp.logical_not(tie))
    def _():
        def body(kc, _):
            sc_ref[kc] = jnp.where(sc_ref[kc] >= thr, 0.0, NEG)
            return 0
        lax.fori_loop(0, nch, body, 0)

    @pl.when(tie)
    def _():
        r = lax.broadcasted_iota(jnp.int32, (ch, ch), 0)
        c = lax.broadcasted_iota(jnp.int32, (ch, ch), 1)
        below = jnp.where(c < r, 1.0, 0.0).astype(BF16)
        need = kk - count(lambda s: s > thr)

        def body(kc, seen):
            s = sc_ref[kc]
            eq = jnp.where(s == thr, 1.0, 0.0)
            rank = jnp.dot(below, eq.astype(BF16), preferred_element_type=F32) + seen
            sel = (s > thr) | ((s == thr) & (rank < need))
            sc_ref[kc] = jnp.where(sel, 0.0, NEG)
            return seen + jnp.sum(eq, axis=0, keepdims=True)
        lax.fori_loop(0, nch, body, jnp.zeros((1, tq), F32))

    zq = jnp.zeros((HEAD_DIM, tq), BF16)
    for hh in range(N_HEADS):
        qh = q_ref[0, hh * HEAD_DIM:(hh + 1) * HEAD_DIM, :]
        qpad_ref[hh] = jnp.concatenate([qh, zq] if hh < GQA_GROUP else [zq, qh], axis=0)

    acc_ref[...] = jnp.zeros(acc_ref.shape, F32)
    ones = jnp.ones((ONES_ROWS, ch), BF16)

    def key_chunk(kc):
        return k_ref[0, pl.ds(pl.multiple_of(kc * ch, ch), ch), :]

    def pv_accumulate(kc, probs):
        vt = vt_ref[0, kc]
        for g in range(N_KV_HEADS):
            vaug = jnp.concatenate([vt[g * HEAD_DIM:(g + 1) * HEAD_DIM], ones], axis=0)
            for hh in range(g * GQA_GROUP, (g + 1) * GQA_GROUP):
                acc_ref[hh] += jnp.dot(vaug, probs(hh), preferred_element_type=F32)

    @pl.when(bounded)
    def _():
        def prob_chunk(kc):
            kcs = key_chunk(kc)
            bias = sc_ref[kc]
            for hh in range(N_HEADS):
                s = jnp.dot(kcs, qpad_ref[hh], preferred_element_type=F32) + bias
                p_ref[hh, kc] = jnp.exp2(s).astype(BF16)

        def pv_chunk(kc):
            pv_accumulate(kc, lambda hh: p_ref[hh, kc])

        npair_full = nch // 2

        @pl.when(npair_full > 0)
        def _():
            prob_chunk(0)
            prob_chunk(1)

            def body(t, _):
                pv_chunk(2 * t - 2)
                pv_chunk(2 * t - 1)
                prob_chunk(2 * t)
                prob_chunk(2 * t + 1)
                return 0
            lax.fori_loop(1, npair_full, body, 0)
            pv_chunk(2 * npair_full - 2)
            pv_chunk(2 * npair_full - 1)

        @pl.when(nch % 2 == 1)
        def _():
            prob_chunk(nch - 1)
            pv_chunk(nch - 1)

    @pl.when(jnp.logical_not(bounded))
    def _():
        def logits_chunk(kc, ms):
            kcs = key_chunk(kc)
            bias = sc_ref[kc]
            out = []
            for hh in range(N_HEADS):
                s = jnp.dot(kcs, qpad_ref[hh], preferred_element_type=F32) + bias
                lg_ref[hh, kc] = s
                out.append(jnp.maximum(ms[hh], fold8(s, jnp.max)))
            return tuple(out)

        ms = for_chunks(logits_chunk, tuple(jnp.full((8, tq), NEG, F32) for _ in range(N_HEADS)))
        for hh in range(N_HEADS):
            m_ref[hh] = jnp.max(ms[hh], axis=0, keepdims=True)

        def pv_chunk(kc, _):
            pv_accumulate(kc, lambda hh: jnp.exp2(lg_ref[hh, kc] - m_ref[hh]).astype(BF16))
            return 0
        for_chunks(pv_chunk, 0)

    outs = []
    for hh in range(N_HEADS):
        a = acc_ref[hh]
        outs.append(a[0:HEAD_DIM] * (1.0 / a[HEAD_DIM:HEAD_DIM + 1]))
    y = jnp.concatenate(outs, axis=0)
    yn = y * lax.rsqrt(jnp.mean(y * y, axis=0, keepdims=True) + EPS) * g_ref[...]
    o_ref[0] = yn.T.astype(BF16)


def _dsa_attn(qi3t, wt, ki3, qt, kk, vt, kn, g_attn_b, topk):
    bsz, s, _ = ki3.shape
    tq, ch = ATT_TQ, ATT_CH
    nq = N_HEADS * HEAD_DIM
    nkv = N_KV_HEADS * HEAD_DIM
    return pl.pallas_call(
        functools.partial(_attn_kernel, topk=topk),
        grid=(bsz, s // tq),
        in_specs=[
            pl.BlockSpec((1, IDX_HEADS, 2 * IDX_DIM, tq), lambda b, j: (b, 0, 0, j)),
            pl.BlockSpec((1, IDX_HEADS, tq), lambda b, j: (b, 0, j)),
            pl.BlockSpec((1, s, IDX_K), lambda b, j: (b, 0, 0)),
            pl.BlockSpec((1, nq, tq), lambda b, j: (b, 0, j)),
            pl.BlockSpec((1, s, nkv), lambda b, j: (b, 0, 0)),
            pl.BlockSpec((1, s // ch, nkv, ch), lambda b, j: (b, 0, 0, 0)),
            pl.BlockSpec((1, N_KV_HEADS, s), lambda b, j: (b, 0, 0)),
            pl.BlockSpec((nq, tq), lambda b, j: (0, 0)),
        ],
        out_specs=pl.BlockSpec((1, tq, nq), lambda b, j: (b, j, 0)),
        out_shape=jax.ShapeDtypeStruct((bsz, s, nq), BF16),
        scratch_shapes=[
            pltpu.VMEM((2 * ((s // ch + 1) // 2), ch, tq), F32),
            pltpu.VMEM((N_HEADS, s // ch, ch, tq), F32),
            pltpu.VMEM((N_HEADS, s // ch, ch, tq), BF16),
            pltpu.VMEM((IDX_HEADS, IDX_K, tq), BF16),
            pltpu.VMEM((N_HEADS, 2 * HEAD_DIM, tq), BF16),
            pltpu.VMEM((N_HEADS, 1, tq), F32),
            pltpu.VMEM((N_HEADS, HEAD_DIM + ONES_ROWS, tq), F32),
        ],
        compiler_params=pltpu.CompilerParams(
            dimension_semantics=("arbitrary", "arbitrary"), vmem_limit_bytes=VMEM_LIMIT_BYTES),
        name="dsa_attn",
    )(qi3t, wt, ki3, qt, kk, vt, kn, g_attn_b)


def _out_ffn_kernel(x_ref, ya_ref, yb_ref, mod_ref, wo_ref, gpm_ref, gpf_ref, w1_ref, w2_ref, gpo_ref, o_ref):
    d_a = ya_ref.shape[2]
    tm = x_ref.shape[1]
    g1 = mod_ref[0, 2:3, :]
    sh2 = mod_ref[0, 3:4, :]
    sc2 = mod_ref[0, 4:5, :]
    g2 = mod_ref[0, 5:6, :]

    def mix_residual(rows):
        o = jnp.dot(ya_ref[0, rows, :], wo_ref[0:d_a, :], preferred_element_type=F32)
        o += jnp.dot(yb_ref[0, rows, :], wo_ref[d_a:, :], preferred_element_type=F32)
        return x_ref[0, rows, :] + g1 * _rms(o, gpm_ref[...])

    def mlp(x1):
        h2 = (_rms(x1, gpf_ref[...]) * (1.0 + sc2) + sh2).astype(BF16)
        f = jnp.zeros(x1.shape, F32)
        for c in range(w1_ref.shape[1] // FFN_TF):
            a = jnp.dot(h2, w1_ref[:, c * FFN_TF:(c + 1) * FFN_TF], preferred_element_type=F32)
            a = jnp.square(jnp.maximum(a, 0.0)).astype(BF16)
            f += jnp.dot(a, w2_ref[c * FFN_TF:(c + 1) * FFN_TF, :], preferred_element_type=F32)
        return f

    halves = [pl.ds(i * (tm // 2), tm // 2) for i in range(2)]
    x1 = [mix_residual(rows) for rows in halves]
    f = [mlp(x1h) for x1h in x1]
    for rows, x1h, fh in zip(halves, x1, f):
        o_ref[0, rows, :] = x1h + g2 * _rms(fh, gpo_ref[...])


def _out_ffn(x, ya, yb, mod3, w_out, g_pm, g_pf, w1, w2, g_po):
    bsz, s, d = x.shape
    tm = FFN_TM
    d_a = ya.shape[2]
    const2 = lambda b, i: (0, 0)
    resident = functools.partial(pl.BlockSpec, index_map=const2, pipeline_mode=pl.Buffered(1))
    return pl.pallas_call(
        _out_ffn_kernel,
        grid=(bsz, s // tm),
        in_specs=[
            pl.BlockSpec((1, tm, d), lambda b, i: (b, i, 0)),
            pl.BlockSpec((1, tm, d_a), lambda b, i: (b, i, 0)),
            pl.BlockSpec((1, tm, yb.shape[2]), lambda b, i: (b, i, 0)),
            pl.BlockSpec((1, 6, d), lambda b, i: (b, 0, 0)),
            resident(w_out.shape),
            pl.BlockSpec((1, d), const2),
            pl.BlockSpec((1, d), const2),
            resident(w1.shape),
            resident(w2.shape),
            pl.BlockSpec((1, d), const2),
        ],
        out_specs=pl.BlockSpec((1, tm, d), lambda b, i: (b, i, 0)),
        out_shape=jax.ShapeDtypeStruct((bsz, s, d), F32),
        compiler_params=pltpu.CompilerParams(
            dimension_semantics=("arbitrary", "arbitrary"), vmem_limit_bytes=VMEM_LIMIT_BYTES),
        name="out_ffn",
    )(x, ya, yb, mod3, w_out, g_pm, g_pf, w1, w2, g_po)


def _layer(x, mod3, cos_t, sin_t, g_pre_mix, w_in, g_sgu_v, w_spatial, b_spatial, g_out_sgu, g_out_attn,
           w_out, g_post_mix, g_pre_ffn, w_ff1, w_ff2, g_post_ffn):
    bsz, s, d = x.shape
    d_sgu = SGU_GROUPS * SGU_GROUP_DIM
    nq = N_HEADS * HEAD_DIM
    nkv = N_KV_HEADS * HEAD_DIM
    nqi = IDX_HEADS * IDX_DIM
    topk = min(TOPK_MAX, s // 4)

    o_q = 2 * d_sgu
    o_qi = o_q + nq + 2 * nkv
    w_uv = w_in[:, :o_q].astype(BF16)
    wt_main = w_in[:, o_q:o_qi].T.astype(BF16)
    n_idx = nqi + IDX_DIM + IDX_HEADS
    wt_idx = jnp.pad(w_in[:, o_qi:o_qi + n_idx].T, ((0, (-n_idx) % 16), (0, 0)))
    wt_idx_hi = wt_idx.astype(BF16)
    wt_idx_lo = (wt_idx - wt_idx_hi.astype(F32)).astype(BF16)
    b_sp = jnp.broadcast_to(b_spatial[:, :, None], (SGU_GROUPS, CHUNK, SGU_GROUP_DIM))

    ya, qt, kk, vt, qi3t, ki3, wt, kn = _in_proj(
        x, mod3, g_pre_mix.reshape(1, d), w_uv, wt_main, wt_idx_hi, wt_idx_lo, cos_t, sin_t,
        g_sgu_v.reshape(1, d_sgu), w_spatial, b_sp, g_out_sgu.reshape(1, d_sgu))

    g_attn_b = jnp.broadcast_to(g_out_attn[:, None], (nq, ATT_TQ))
    yb = _dsa_attn(qi3t, wt, ki3, qt, kk, vt, kn, g_attn_b, topk)

    return _out_ffn(x, ya, yb, mod3, w_out.astype(BF16), g_post_mix.reshape(1, d), g_pre_ffn.reshape(1, d),
                    w_ff1.astype(BF16), w_ff2.astype(BF16), g_post_ffn.reshape(1, d))


def kernel(x, c, positions, w_ada, b_ada, g_pre_mix, w_in, g_sgu_v, w_spatial, b_spatial, g_out_sgu, g_out_attn,
           w_out, g_post_mix, g_pre_ffn, w_ff1, w_ff2, g_post_ffn):
    bsz, s, d = x.shape
    inv_freq = ROPE_THETA ** (-jnp.arange(0, ROT_DIM, 2, dtype=F32) / ROT_DIM)
    ang = positions.astype(F32)[:, None, :] * inv_freq[None, :, None]
    cos_t, sin_t = jnp.cos(ang), jnp.sin(ang)
    for l in range(w_ada.shape[0]):
        mod3 = _ada_mod(c, w_ada[l], b_ada[l]).reshape(bsz, 6, d)
        x = _layer(x, mod3, cos_t, sin_t, g_pre_mix[l], w_in[l], g_sgu_v[l], w_spatial[l], b_spatial[l],
                   g_out_sgu[l], g_out_attn[l], w_out[l], g_post_mix[l], g_pre_ffn[l], w_ff1[l], w_ff2[l],
                   g_post_ffn[l])
    return x
```

```python
import functools

import jax
import jax.numpy as jnp
from jax import lax
from jax.experimental import pallas as pl
from jax.experimental.pallas import tpu as pltpu

F32 = jnp.float32
BF16 = jnp.bfloat16

SGU_GROUPS = 4
SGU_GROUP_DIM = 128
CHUNK = 128
N_HEADS = 8
HEAD_DIM = 64
N_KV_HEADS = 2
GQA_GROUP = N_HEADS // N_KV_HEADS
ROT_DIM = 16
ROT_HALF = ROT_DIM // 2
ROPE_THETA = 500000.0
IDX_HEADS = 8
IDX_DIM = 64
TOPK_MAX = 256
EPS = 1e-6

ADA_TN = 1024
PROJ_TM = 512
ATT_TQ = 256
ATT_CH = 256
ATT_NB = 2
FFN_TM = 512
FFN_TF = 1024
BISECT_UNROLL = 4
BISECT_BLIND_ROUNDS = 4
BISECT_CHECKED_STEPS = 2
IDX_K = 4 * IDX_DIM
ONES_ROWS = 16
NEG = -1e30
TINY = 1e-30
F32_MAX = 3.4028235e38
LOGIT_BOUND = 96.0
VMEM_LIMIT_BYTES = 56 * 1024 * 1024
Q_SCALE = (HEAD_DIM ** -0.5) * 1.4426950408889634


def _split_bf16(a):
    hi = a.astype(BF16).astype(F32)
    lo = (a - hi).astype(BF16).astype(F32)
    return hi, lo


def _rms(a, g):
    return a * lax.rsqrt(jnp.mean(a * a, axis=-1, keepdims=True) + EPS) * g


def _ada_kernel(c_ref, w_ref, b_ref, o_ref):
    c = c_ref[...]
    ca = c * (1.0 / (1.0 + jnp.exp(-c)))
    c_hi, c_lo = _split_bf16(ca)
    w_hi, w_lo = _split_bf16(w_ref[...])
    c_hi, c_lo, w_hi, w_lo = (t.astype(BF16) for t in (c_hi, c_lo, w_hi, w_lo))
    acc = jnp.dot(c_hi, w_hi, preferred_element_type=F32)
    acc += jnp.dot(c_hi, w_lo, preferred_element_type=F32)
    acc += jnp.dot(c_lo, w_hi, preferred_element_type=F32)
    o_ref[...] = acc + b_ref[...]


def _ada_mod(c, w_ada, b_ada):
    bsz, d = c.shape
    n = w_ada.shape[1]
    return pl.pallas_call(
        _ada_kernel,
        grid=(n // ADA_TN,),
        in_specs=[
            pl.BlockSpec((bsz, d), lambda i: (0, 0)),
            pl.BlockSpec((d, ADA_TN), lambda i: (0, i)),
            pl.BlockSpec((1, ADA_TN), lambda i: (0, i)),
        ],
        out_specs=pl.BlockSpec((bsz, ADA_TN), lambda i: (0, i)),
        out_shape=jax.ShapeDtypeStruct((bsz, n), F32),
        compiler_params=pltpu.CompilerParams(dimension_semantics=("arbitrary",)),
        name="ada_mod",
    )(c, w_ada, b_ada.reshape(1, n))


def _rope_rows(p, cos, sin):
    x1, x2 = p[0:ROT_HALF], p[ROT_HALF:ROT_DIM]
    return jnp.concatenate([x1 * cos - x2 * sin, x2 * cos + x1 * sin, p[ROT_DIM:]], axis=0)


def _dot_nt(a, b):
    return lax.dot_general(a, b, (((1,), (1,)), ((), ())), preferred_element_type=F32)


def _in_proj_kernel(x_ref, mod_ref, gpre_ref, wuv_ref, wtm_ref, wti_hi_ref, wti_lo_ref, cos_ref, sin_ref,
                    gv_ref, wsp_ref, bsp_ref, gosgu_ref,
                    ya_ref, qt_ref, k_ref, vt_ref, qi_ref, ki_ref, wt_ref, kn_ref):
    d_sgu = SGU_GROUPS * SGU_GROUP_DIM
    nq = N_HEADS * HEAD_DIM
    nkv = N_KV_HEADS * HEAD_DIM
    nqi = IDX_HEADS * IDX_DIM
    sh1 = mod_ref[0, 0:1, :]
    sc1 = mod_ref[0, 1:2, :]
    row = lax.broadcasted_iota(jnp.int32, (CHUNK, CHUNK), 0)
    col = lax.broadcasted_iota(jnp.int32, (CHUNK, CHUNK), 1)
    w_mix = [jnp.where(col <= row, wsp_ref[g], 0.0).astype(BF16) for g in range(SGU_GROUPS)]

    def project(t0, tn):
        tok = slice(t0, t0 + tn)
        h = _rms(x_ref[0, tok, :], gpre_ref[...]) * (1.0 + sc1) + sh1
        h_hi32, h_lo32 = _split_bf16(h)
        h_hi, h_lo = h_hi32.astype(BF16), h_lo32.astype(BF16)
        cos, sin = cos_ref[0, :, tok], sin_ref[0, :, tok]

        puv = jnp.dot(h_hi, wuv_ref[...], preferred_element_type=F32)
        pti = _dot_nt(wti_hi_ref[...], h_hi) + _dot_nt(wti_lo_ref[...], h_hi) + _dot_nt(wti_hi_ref[...], h_lo)
        ptm = _dot_nt(wtm_ref[...], h_hi)

        zu = jax.nn.gelu(puv[:, :d_sgu])
        zv = jax.nn.gelu(puv[:, d_sgu:])
        ya_groups = []
        for g in range(SGU_GROUPS):
            sl = slice(g * SGU_GROUP_DIM, (g + 1) * SGU_GROUP_DIM)
            vn = _rms(zv[:, sl], gv_ref[:, sl]).astype(BF16)
            parts = []
            for ci in range(tn // CHUNK):
                rs = slice(ci * CHUNK, (ci + 1) * CHUNK)
                mixed = jnp.dot(w_mix[g], vn[rs], preferred_element_type=F32) + bsp_ref[g]
                parts.append(zu[rs, sl] * mixed)
            ya_groups.append(jnp.concatenate(parts, axis=0))
        ya = jnp.concatenate(ya_groups, axis=1)
        ya_ref[0, tok, :] = _rms(ya, gosgu_ref[...]).astype(BF16)

        for hh in range(N_HEADS):
            rs = slice(hh * HEAD_DIM, (hh + 1) * HEAD_DIM)
            qt_ref[0, rs, tok] = (_rope_rows(ptm[rs], cos, sin) * Q_SCALE).astype(BF16)
        kt = jnp.concatenate(
            [_rope_rows(ptm[nq + g * HEAD_DIM:nq + (g + 1) * HEAD_DIM], cos, sin) for g in range(N_KV_HEADS)],
            axis=0)
        k_ref[0, tok, :] = kt.T.astype(BF16)
        kb = kt.astype(BF16).astype(F32)
        kn_ref[0, :, tok] = jnp.concatenate(
            [jnp.sum(jnp.square(kb[g * HEAD_DIM:(g + 1) * HEAD_DIM]), axis=0, keepdims=True)
             for g in range(N_KV_HEADS)], axis=0)
        vt = ptm[nq + nkv:nq + 2 * nkv].astype(BF16)
        for ci in range(tn // ATT_CH):
            vt_ref[0, t0 // ATT_CH + ci] = vt[:, ci * ATT_CH:(ci + 1) * ATT_CH]

        for hh in range(IDX_HEADS):
            r = _rope_rows(pti[hh * IDX_DIM:(hh + 1) * IDX_DIM], cos, sin)
            r_hi, r_lo = _split_bf16(r)
            qi_ref[0, hh, :, tok] = jnp.concatenate([r_hi, r_lo], axis=0).astype(BF16)
        kr = _rope_rows(pti[nqi:nqi + IDX_DIM], cos, sin)
        k_hi, k_lo = _split_bf16(kr)
        ki_ref[0, tok, :] = jnp.concatenate([k_hi, k_lo, k_hi, k_lo], axis=0).T.astype(BF16)
        wt_ref[0, :, tok] = (pti[nqi + IDX_DIM:nqi + IDX_DIM + IDX_HEADS]
                             * ((IDX_HEADS ** -0.5) * (IDX_DIM ** -0.5)))

    tm = x_ref.shape[1]
    for t0 in range(0, tm, tm // 2):
        project(t0, tm // 2)


def _in_proj(x, mod3, g_pre, w_uv, wt_main, wt_idx_hi, wt_idx_lo, cos_t, sin_t, g_v, w_sp, b_sp, g_osgu):
    bsz, s, d = x.shape
    tm = PROJ_TM
    d_sgu = SGU_GROUPS * SGU_GROUP_DIM
    nq = N_HEADS * HEAD_DIM
    nkv = N_KV_HEADS * HEAD_DIM
    const2 = lambda b, i: (0, 0)
    const3 = lambda b, i: (0, 0, 0)
    return pl.pallas_call(
        _in_proj_kernel,
        grid=(bsz, s // tm),
        in_specs=[
            pl.BlockSpec((1, tm, d), lambda b, i: (b, i, 0)),
            pl.BlockSpec((1, 6, d), lambda b, i: (b, 0, 0)),
            pl.BlockSpec((1, d), const2),
            pl.BlockSpec(w_uv.shape, const2),
            pl.BlockSpec(wt_main.shape, const2),
            pl.BlockSpec(wt_idx_hi.shape, const2),
            pl.BlockSpec(wt_idx_lo.shape, const2),
            pl.BlockSpec((1, ROT_HALF, tm), lambda b, i: (b, 0, i)),
            pl.BlockSpec((1, ROT_HALF, tm), lambda b, i: (b, 0, i)),
            pl.BlockSpec((1, d_sgu), const2),
            pl.BlockSpec(w_sp.shape, const3),
            pl.BlockSpec(b_sp.shape, const3),
            pl.BlockSpec((1, d_sgu), const2),
        ],
        out_specs=[
            pl.BlockSpec((1, tm, d_sgu), lambda b, i: (b, i, 0)),
            pl.BlockSpec((1, nq, tm), lambda b, i: (b, 0, i)),
            pl.BlockSpec((1, tm, nkv), lambda b, i: (b, i, 0)),
            pl.BlockSpec((1, tm // ATT_CH, nkv, ATT_CH), lambda b, i: (b, i, 0, 0)),
            pl.BlockSpec((1, IDX_HEADS, 2 * IDX_DIM, tm), lambda b, i: (b, 0, 0, i)),
            pl.BlockSpec((1, tm, IDX_K), lambda b, i: (b, i, 0)),
            pl.BlockSpec((1, IDX_HEADS, tm), lambda b, i: (b, 0, i)),
            pl.BlockSpec((1, N_KV_HEADS, tm), lambda b, i: (b, 0, i)),
        ],
        out_shape=[
            jax.ShapeDtypeStruct((bsz, s, d_sgu), BF16),
            jax.ShapeDtypeStruct((bsz, nq, s), BF16),
            jax.ShapeDtypeStruct((bsz, s, nkv), BF16),
            jax.ShapeDtypeStruct((bsz, s // ATT_CH, nkv, ATT_CH), BF16),
            jax.ShapeDtypeStruct((bsz, IDX_HEADS, 2 * IDX_DIM, s), BF16),
            jax.ShapeDtypeStruct((bsz, s, IDX_K), BF16),
            jax.ShapeDtypeStruct((bsz, IDX_HEADS, s), F32),
            jax.ShapeDtypeStruct((bsz, N_KV_HEADS, s), F32),
        ],
        compiler_params=pltpu.CompilerParams(
            dimension_semantics=("arbitrary", "arbitrary"), vmem_limit_bytes=VMEM_LIMIT_BYTES),
        name="in_proj",
    )(x, mod3, g_pre, w_uv, wt_main, wt_idx_hi, wt_idx_lo, cos_t, sin_t, g_v, w_sp, b_sp, g_osgu)


def _attn_kernel(qi_ref, w_ref, ki_ref, q_ref, k_ref, vt_ref, kn_ref, g_ref, o_ref,
                 sc_ref, p_ref, qi4_ref, qpad_ref, m_ref, acc_ref, *, topk):
    nb = qi_ref.shape[0]
    rows = range(nb)
    tq, ch = ATT_TQ, ATT_CH
    j = pl.program_id(1)
    nch = j + 1
    q_pos = j * tq + lax.broadcasted_iota(jnp.int32, (1, tq), 1)
    row_iota = lax.broadcasted_iota(jnp.int32, (ch, tq), 0)

    def fold8(a, op):
        return op(a.reshape(ch // 8, 8, tq), axis=0)

    def chunk_rows(kc):
        return pl.ds(pl.multiple_of(kc * ch, ch), ch)

    for bi in rows:
        for hh in range(IDX_HEADS):
            q_hi, q_lo = qi_ref[bi, hh, 0:IDX_DIM, :], qi_ref[bi, hh, IDX_DIM:, :]
            qi4_ref[bi, hh] = jnp.concatenate([q_hi, q_hi, q_lo, q_lo], axis=0)

    def score_chunk(kc, carry, diagonal):
        out = []
        for bi in rows:
            mn, mx, n_pos, n_nonneg = carry[bi]
            kic = ki_ref[bi, chunk_rows(kc), :]
            acc = jnp.zeros((ch, tq), F32)
            for hh in range(IDX_HEADS):
                dd = jnp.dot(kic, qi4_ref[bi, hh], preferred_element_type=F32)
                acc = acc + w_ref[bi, hh:hh + 1, :] * jnp.maximum(dd, 0.0)
            if diagonal:
                causal = (row_iota + kc * ch) <= q_pos
                sc = jnp.where(causal, acc, -jnp.inf)
                mn = jnp.minimum(mn, fold8(jnp.where(causal, acc, jnp.inf), jnp.min))
            else:
                sc = acc
                mn = jnp.minimum(mn, fold8(acc, jnp.min))
            sc_ref[bi, kc] = sc
            mx = jnp.maximum(mx, fold8(sc, jnp.max))
            n_pos = n_pos + fold8(jnp.where(sc > 0.0, 1.0, 0.0), jnp.sum)
            n_nonneg = n_nonneg + fold8(jnp.where(sc >= 0.0, 1.0, 0.0), jnp.sum)
            out.append((mn, mx, n_pos, n_nonneg))
        return tuple(out)

    stats0 = (jnp.full((8, tq), jnp.inf, F32), jnp.full((8, tq), -jnp.inf, F32),
              jnp.zeros((8, tq), F32), jnp.zeros((8, tq), F32))
    stats = lax.fori_loop(0, j, functools.partial(score_chunk, diagonal=False), tuple(stats0 for _ in rows))
    stats = score_chunk(j, stats, diagonal=True)

    def count(preds):
        def body(kc, cs):
            return tuple(c + fold8(jnp.where(pred(sc_ref[bi, kc]), 1.0, 0.0), jnp.sum)
                         for bi, (c, pred) in enumerate(zip(cs, preds)))
        c8 = lax.fori_loop(0, nch, body, tuple(jnp.zeros((8, tq), F32) for _ in rows))
        return [jnp.sum(c, axis=0, keepdims=True) for c in c8]

    n_causal = (q_pos + 1).astype(F32)
    kk = jnp.minimum(float(topk), n_causal)

    def as_f(m):
        return jnp.where(m, 1.0, 0.0)

    def initial_state(stat):
        mn = jnp.min(stat[0], axis=0, keepdims=True)
        mx = jnp.max(stat[1], axis=0, keepdims=True)
        c0_gt = jnp.sum(stat[2], axis=0, keepdims=True)
        c0_ge = jnp.sum(stat[3], axis=0, keepdims=True)
        at_zero = (c0_gt < kk) & (c0_ge >= kk)
        take_all = n_causal == kk
        pos = c0_gt >= kk
        lo0 = jnp.where(pos, jnp.maximum(mn, 0.0), mn)
        clo0 = jnp.where(pos & (mn <= 0.0), c0_ge, n_causal)
        hi0 = jnp.where(pos, jnp.minimum(2.0 * mx + TINY, F32_MAX), 0.0)
        thr0 = jnp.where(take_all, mn, 0.0)
        tied0 = jnp.logical_not(take_all) & at_zero & (c0_ge > kk)
        done0 = at_zero | take_all
        return lo0, clo0, hi0, thr0, as_f(done0), as_f(tied0)

    def bisect_step(states):
        mids = [0.5 * st[0] + 0.5 * st[2] for st in states]
        counts = count([lambda s, mid=mid: s >= mid for mid in mids])
        out = []
        for (lo, clo, hi, thr, done, tied), mid, c in zip(states, mids, counts):
            valid = (mid > lo) & (mid < hi)
            move_lo = valid & (c >= kk)
            thr = jnp.where(done, thr, jnp.where(valid, mid, lo))
            tied = tied | (jnp.logical_not(done) & jnp.logical_not(valid) & (clo > kk))
            done = done | jnp.logical_not(valid) | (c == kk)
            hi = jnp.where(valid & (c < kk), mid, hi)
            lo = jnp.where(move_lo, mid, lo)
            clo = jnp.where(move_lo, c, clo)
            out.append((lo, clo, hi, thr, done, tied))
        return out

    def bisect_round(states, steps):
        sts = [(lo, clo, hi, thr, done_f > 0.5, tied_f > 0.5) for lo, clo, hi, thr, done_f, tied_f in states]
        for _ in range(steps):
            sts = bisect_step(sts)
        return tuple((lo, clo, hi, thr, as_f(done), as_f(tied)) for lo, clo, hi, thr, done, tied in sts)

    def bisect_body(carry):
        states = bisect_round(carry[0], BISECT_CHECKED_STEPS)
        return states, jnp.max(1.0 - functools.reduce(jnp.minimum, [st[4] for st in states]))

    blind = jnp.where((j + 1) * tq <= topk, 0, BISECT_BLIND_ROUNDS)
    states = lax.fori_loop(0, blind, lambda _, sts: bisect_round(sts, BISECT_UNROLL),
                           tuple(initial_state(stat) for stat in stats))
    states, _ = lax.while_loop(
        lambda carry: carry[1] > 0.0, bisect_body, (states, jnp.where(blind > 0, 1.0, 0.0)))
    thrs = [st[3] for st in states]
    tied_f = functools.reduce(jnp.maximum, [st[5] for st in states])

    def logit_bound_sq(bi):
        q_sq = jnp.sum(jnp.square(q_ref[bi].astype(F32)).reshape(N_HEADS, HEAD_DIM, tq), axis=1)
        q_sq = jnp.max(jnp.max(q_sq, axis=0, keepdims=True), axis=1, keepdims=True)
        k_sq = jnp.max(jnp.max(kn_ref[bi], axis=0, keepdims=True), axis=1, keepdims=True)
        return q_sq * k_sq
    worst = functools.reduce(jnp.maximum, [logit_bound_sq(bi) for bi in rows])
    bounded_f = jnp.where(worst <= LOGIT_BOUND * LOGIT_BOUND, 2.0, 0.0)

    code = jnp.max(tied_f + bounded_f)
    tie = jnp.logical_or(code == 1.0, code == 3.0)
    bounded = code >= 2.0

    @pl.when(jnp.logical_not(tie))
    def _():
        def body(kc, _):
            for bi in rows:
                sc_ref[bi, kc] = jnp.where(sc_ref[bi, kc] >= thrs[bi], 0.0, NEG)
            return 0
        lax.fori_loop(0, nch, body, 0)

    @pl.when(tie)
    def _():
        r = lax.broadcasted_iota(jnp.int32, (ch, ch), 0)
        c = lax.broadcasted_iota(jnp.int32, (ch, ch), 1)
        below = jnp.where(c < r, 1.0, 0.0).astype(BF16)
        n_gt = count([lambda s, thr=thr: s > thr for thr in thrs])
        needs = [kk - n for n in n_gt]

        def body(kc, seen):
            out = []
            for bi in rows:
                s = sc_ref[bi, kc]
                eq = jnp.where(s == thrs[bi], 1.0, 0.0)
                rank = jnp.dot(below, eq.astype(BF16), preferred_element_type=F32) + seen[bi]
                sel = (s > thrs[bi]) | ((s == thrs[bi]) & (rank < needs[bi]))
                sc_ref[bi, kc] = jnp.where(sel, 0.0, NEG)
                out.append(seen[bi] + jnp.sum(eq, axis=0, keepdims=True))
            return tuple(out)
        lax.fori_loop(0, nch, body, tuple(jnp.zeros((1, tq), F32) for _ in rows))

    zq = jnp.zeros((HEAD_DIM, tq), BF16)
    for bi in rows:
        for hh in range(N_HEADS):
            qh = q_ref[bi, hh * HEAD_DIM:(hh + 1) * HEAD_DIM, :]
            qpad_ref[bi, hh] = jnp.concatenate([qh, zq] if hh < GQA_GROUP else [zq, qh], axis=0)
    acc_ref[...] = jnp.zeros(acc_ref.shape, F32)
    m_ref[...] = jnp.zeros(m_ref.shape, F32)
    ones = jnp.ones((ONES_ROWS, ch), BF16)

    def logits(bi, kc, hh, kcs, bias):
        return jnp.dot(kcs, qpad_ref[bi, hh], preferred_element_type=F32) + bias

    @pl.when(jnp.logical_not(bounded))
    def _():
        def body(kc, ms):
            out = []
            for bi in rows:
                kcs, bias = k_ref[bi, chunk_rows(kc), :], sc_ref[bi, kc]
                out.append(tuple(jnp.maximum(ms[bi][hh], fold8(logits(bi, kc, hh, kcs, bias), jnp.max))
                                 for hh in range(N_HEADS)))
            return tuple(out)
        ms = lax.fori_loop(0, nch, body,
                           tuple(tuple(jnp.full((8, tq), NEG, F32) for _ in range(N_HEADS)) for _ in rows))
        for bi in rows:
            for hh in range(N_HEADS):
                m_ref[bi, hh] = jnp.max(ms[bi][hh], axis=0, keepdims=True)

    def prob_chunk(kc):
        for bi in rows:
            kcs, bias = k_ref[bi, chunk_rows(kc), :], sc_ref[bi, kc]
            for hh in range(N_HEADS):
                p_ref[bi, hh, kc] = jnp.exp2(logits(bi, kc, hh, kcs, bias) - m_ref[bi, hh]).astype(BF16)

    def pv_chunk(kc):
        for bi in rows:
            vt = vt_ref[bi, kc]
            for g in range(N_KV_HEADS):
                vaug = jnp.concatenate([vt[g * HEAD_DIM:(g + 1) * HEAD_DIM], ones], axis=0)
                for hh in range(g * GQA_GROUP, (g + 1) * GQA_GROUP):
                    acc_ref[bi, hh] += jnp.dot(vaug, p_ref[bi, hh, kc], preferred_element_type=F32)

    prob_chunk(0)

    def attend(kc, _):
        pv_chunk(kc - 1)
        prob_chunk(kc)
        return 0
    lax.fori_loop(1, nch, attend, 0)
    pv_chunk(nch - 1)

    for bi in rows:
        outs = []
        for hh in range(N_HEADS):
            a = acc_ref[bi, hh]
            outs.append(a[0:HEAD_DIM] * (1.0 / a[HEAD_DIM:HEAD_DIM + 1]))
        y = jnp.concatenate(outs, axis=0)
        yn = y * lax.rsqrt(jnp.mean(y * y, axis=0, keepdims=True) + EPS) * g_ref[...]
        o_ref[bi] = yn.T.astype(BF16)


def _dsa_attn(qi3t, wt, ki3, qt, kk, vt, kn, g_attn_b, topk):
    bsz, s, _ = ki3.shape
    tq, ch = ATT_TQ, ATT_CH
    nb = ATT_NB if bsz % ATT_NB == 0 else 1
    nq = N_HEADS * HEAD_DIM
    nkv = N_KV_HEADS * HEAD_DIM
    return pl.pallas_call(
        functools.partial(_attn_kernel, topk=topk),
        grid=(bsz // nb, s // tq),
        in_specs=[
            pl.BlockSpec((nb, IDX_HEADS, 2 * IDX_DIM, tq), lambda b, j: (b, 0, 0, j)),
            pl.BlockSpec((nb, IDX_HEADS, tq), lambda b, j: (b, 0, j)),
            pl.BlockSpec((nb, s, IDX_K), lambda b, j: (b, 0, 0)),
            pl.BlockSpec((nb, nq, tq), lambda b, j: (b, 0, j)),
            pl.BlockSpec((nb, s, nkv), lambda b, j: (b, 0, 0)),
            pl.BlockSpec((nb, s // ch, nkv, ch), lambda b, j: (b, 0, 0, 0)),
            pl.BlockSpec((nb, N_KV_HEADS, s), lambda b, j: (b, 0, 0)),
            pl.BlockSpec((nq, tq), lambda b, j: (0, 0)),
        ],
        out_specs=pl.BlockSpec((nb, tq, nq), lambda b, j: (b, j, 0)),
        out_shape=jax.ShapeDtypeStruct((bsz, s, nq), BF16),
        scratch_shapes=[
            pltpu.VMEM((nb, s // ch, ch, tq), F32),
            pltpu.VMEM((nb, N_HEADS, s // ch, ch, tq), BF16),
            pltpu.VMEM((nb, IDX_HEADS, IDX_K, tq), BF16),
            pltpu.VMEM((nb, N_HEADS, 2 * HEAD_DIM, tq), BF16),
            pltpu.VMEM((nb, N_HEADS, 1, tq), F32),
            pltpu.VMEM((nb, N_HEADS, HEAD_DIM + ONES_ROWS, tq), F32),
        ],
        compiler_params=pltpu.CompilerParams(
            dimension_semantics=("arbitrary", "arbitrary"), vmem_limit_bytes=VMEM_LIMIT_BYTES),
        name="dsa_attn",
    )(qi3t, wt, ki3, qt, kk, vt, kn, g_attn_b)


def _out_ffn_kernel(x_ref, ya_ref, yb_ref, mod_ref, wo_ref, gpm_ref, gpf_ref, w1_ref, w2_ref, gpo_ref, o_ref):
    d_a = ya_ref.shape[2]
    tm = x_ref.shape[1]
    g1 = mod_ref[0, 2:3, :]
    sh2 = mod_ref[0, 3:4, :]
    sc2 = mod_ref[0, 4:5, :]
    g2 = mod_ref[0, 5:6, :]

    def mix_residual(rows):
        o = jnp.dot(ya_ref[0, rows, :], wo_ref[0:d_a, :], preferred_element_type=F32)
        o += jnp.dot(yb_ref[0, rows, :], wo_ref[d_a:, :], preferred_element_type=F32)
        return x_ref[0, rows, :] + g1 * _rms(o, gpm_ref[...])

    def mlp(x1):
        h2 = (_rms(x1, gpf_ref[...]) * (1.0 + sc2) + sh2).astype(BF16)
        f = jnp.zeros(x1.shape, F32)
        for c in range(w1_ref.shape[1] // FFN_TF):
            a = jnp.dot(h2, w1_ref[:, c * FFN_TF:(c + 1) * FFN_TF], preferred_element_type=F32)
            a = jnp.square(jnp.maximum(a, 0.0)).astype(BF16)
            f += jnp.dot(a, w2_ref[c * FFN_TF:(c + 1) * FFN_TF, :], preferred_element_type=F32)
        return f

    halves = [pl.ds(i * (tm // 2), tm // 2) for i in range(2)]
    x1 = [mix_residual(rows) for rows in halves]
    f = [mlp(x1h) for x1h in x1]
    for rows, x1h, fh in zip(halves, x1, f):
        o_ref[0, rows, :] = x1h + g2 * _rms(fh, gpo_ref[...])


def _out_ffn(x, ya, yb, mod3, w_out, g_pm, g_pf, w1, w2, g_po):
    bsz, s, d = x.shape
    tm = FFN_TM
    d_a = ya.shape[2]
    const2 = lambda b, i: (0, 0)
    resident = functools.partial(pl.BlockSpec, index_map=const2, pipeline_mode=pl.Buffered(1))
    return pl.pallas_call(
        _out_ffn_kernel,
        grid=(bsz, s // tm),
        in_specs=[
            pl.BlockSpec((1, tm, d), lambda b, i: (b, i, 0)),
            pl.BlockSpec((1, tm, d_a), lambda b, i: (b, i, 0)),
            pl.BlockSpec((1, tm, yb.shape[2]), lambda b, i: (b, i, 0)),
            pl.BlockSpec((1, 6, d), lambda b, i: (b, 0, 0)),
            resident(w_out.shape),
            pl.BlockSpec((1, d), const2),
            pl.BlockSpec((1, d), const2),
            resident(w1.shape),
            resident(w2.shape),
            pl.BlockSpec((1, d), const2),
        ],
        out_specs=pl.BlockSpec((1, tm, d), lambda b, i: (b, i, 0)),
        out_shape=jax.ShapeDtypeStruct((bsz, s, d), F32),
        compiler_params=pltpu.CompilerParams(
            dimension_semantics=("arbitrary", "arbitrary"), vmem_limit_bytes=VMEM_LIMIT_BYTES),
        name="out_ffn",
    )(x, ya, yb, mod3, w_out, g_pm, g_pf, w1, w2, g_po)


def _layer(x, mod3, cos_t, sin_t, g_pre_mix, w_in, g_sgu_v, w_spatial, b_spatial, g_out_sgu, g_out_attn,
           w_out, g_post_mix, g_pre_ffn, w_ff1, w_ff2, g_post_ffn):
    bsz, s, d = x.shape
    d_sgu = SGU_GROUPS * SGU_GROUP_DIM
    nq = N_HEADS * HEAD_DIM
    nkv = N_KV_HEADS * HEAD_DIM
    nqi = IDX_HEADS * IDX_DIM
    topk = min(TOPK_MAX, s // 4)

    o_q = 2 * d_sgu
    o_qi = o_q + nq + 2 * nkv
    w_uv = w_in[:, :o_q].astype(BF16)
    wt_main = w_in[:, o_q:o_qi].T.astype(BF16)
    n_idx = nqi + IDX_DIM + IDX_HEADS
    wt_idx = jnp.pad(w_in[:, o_qi:o_qi + n_idx].T, ((0, (-n_idx) % 16), (0, 0)))
    wt_idx_hi = wt_idx.astype(BF16)
    wt_idx_lo = (wt_idx - wt_idx_hi.astype(F32)).astype(BF16)
    b_sp = jnp.broadcast_to(b_spatial[:, :, None], (SGU_GROUPS, CHUNK, SGU_GROUP_DIM))

    ya, qt, kk, vt, qi3t, ki3, wt, kn = _in_proj(
        x, mod3, g_pre_mix.reshape(1, d), w_uv, wt_main, wt_idx_hi, wt_idx_lo, cos_t, sin_t,
        g_sgu_v.reshape(1, d_sgu), w_spatial, b_sp, g_out_sgu.reshape(1, d_sgu))

    g_attn_b = jnp.broadcast_to(g_out_attn[:, None], (nq, ATT_TQ))
    yb = _dsa_attn(qi3t, wt, ki3, qt, kk, vt, kn, g_attn_b, topk)

    return _out_ffn(x, ya, yb, mod3, w_out.astype(BF16), g_post_mix.reshape(1, d), g_pre_ffn.reshape(1, d),
                    w_ff1.astype(BF16), w_ff2.astype(BF16), g_post_ffn.reshape(1, d))


def kernel(x, c, positions, w_ada, b_ada, g_pre_mix, w_in, g_sgu_v, w_spatial, b_spatial, g_out_sgu, g_out_attn,
           w_out, g_post_mix, g_pre_ffn, w_ff1, w_ff2, g_post_ffn):
    bsz, s, d = x.shape
    inv_freq = ROPE_THETA ** (-jnp.arange(0, ROT_DIM, 2, dtype=F32) / ROT_DIM)
    ang = positions.astype(F32)[:, None, :] * inv_freq[None, :, None]
    cos_t, sin_t = jnp.cos(ang), jnp.sin(ang)
    for l in range(w_ada.shape[0]):
        mod3 = _ada_mod(c, w_ada[l], b_ada[l]).reshape(bsz, 6, d)
        x = _layer(x, mod3, cos_t, sin_t, g_pre_mix[l], w_in[l], g_sgu_v[l], w_spatial[l], b_spatial[l],
                   g_out_sgu[l], g_out_attn[l], w_out[l], g_post_mix[l], g_pre_ffn[l], w_ff1[l], w_ff2[l],
                   g_post_ffn[l])
    return x
```

```python
import functools

import jax
import jax.numpy as jnp
from jax import lax
from jax.experimental import pallas as pl
from jax.experimental.pallas import tpu as pltpu

F32 = jnp.float32
BF16 = jnp.bfloat16

SGU_GROUPS = 4
SGU_GROUP_DIM = 128
CHUNK = 128
N_HEADS = 8
HEAD_DIM = 64
N_KV_HEADS = 2
GQA_GROUP = N_HEADS // N_KV_HEADS
ROT_DIM = 16
ROT_HALF = ROT_DIM // 2
ROPE_THETA = 500000.0
IDX_HEADS = 8
IDX_DIM = 64
TOPK_MAX = 256
EPS = 1e-6

ADA_TN = 1024
PROJ_TM = 1024
PROJ_SUB = 256
ATT_TQ = 256
ATT_CH = 256
ATT_NB = 2
FFN_TM = 512
FFN_TF = 1024
BISECT_UNROLL = 4
BISECT_BLIND_ROUNDS = 4
BISECT_CHECKED_STEPS = 2
IDX_K = 4 * IDX_DIM
ONES_ROWS = 16
NEG = -1e30
TINY = 1e-30
F32_MAX = 3.4028235e38
LOGIT_BOUND = 96.0
VMEM_LIMIT_BYTES = 56 * 1024 * 1024
Q_SCALE = (HEAD_DIM ** -0.5) * 1.4426950408889634


def _split_bf16(a):
    hi = a.astype(BF16).astype(F32)
    lo = (a - hi).astype(BF16).astype(F32)
    return hi, lo


def _rms(a, g):
    return a * lax.rsqrt(jnp.mean(a * a, axis=-1, keepdims=True) + EPS) * g


def _ada_kernel(c_ref, w_ref, b_ref, o_ref):
    c = c_ref[...]
    ca = c * (1.0 / (1.0 + jnp.exp(-c)))
    c_hi, c_lo = _split_bf16(ca)
    w_hi, w_lo = _split_bf16(w_ref[...])
    c_hi, c_lo, w_hi, w_lo = (t.astype(BF16) for t in (c_hi, c_lo, w_hi, w_lo))
    acc = jnp.dot(c_hi, w_hi, preferred_element_type=F32)
    acc += jnp.dot(c_hi, w_lo, preferred_element_type=F32)
    acc += jnp.dot(c_lo, w_hi, preferred_element_type=F32)
    o_ref[...] = acc + b_ref[...]


def _ada_mod(c, w_ada, b_ada):
    bsz, d = c.shape
    n = w_ada.shape[1]
    return pl.pallas_call(
        _ada_kernel,
        grid=(n // ADA_TN,),
        in_specs=[
            pl.BlockSpec((bsz, d), lambda i: (0, 0)),
            pl.BlockSpec((d, ADA_TN), lambda i: (0, i)),
            pl.BlockSpec((1, ADA_TN), lambda i: (0, i)),
        ],
        out_specs=pl.BlockSpec((bsz, ADA_TN), lambda i: (0, i)),
        out_shape=jax.ShapeDtypeStruct((bsz, n), F32),
        compiler_params=pltpu.CompilerParams(dimension_semantics=("arbitrary",)),
        name="ada_mod",
    )(c, w_ada, b_ada.reshape(1, n))


def _rope_rows(p, cos, sin):
    x1, x2 = p[0:ROT_HALF], p[ROT_HALF:ROT_DIM]
    return jnp.concatenate([x1 * cos - x2 * sin, x2 * cos + x1 * sin, p[ROT_DIM:]], axis=0)


def _dot_nt(a, b):
    return lax.dot_general(a, b, (((1,), (1,)), ((), ())), preferred_element_type=F32)


def _in_proj_kernel(x_ref, mod_ref, gpre_ref, wuv_ref, wtm_ref, wti_hi_ref, wti_lo_ref, cos_ref, sin_ref,
                    gv_ref, wsp_ref, bsp_ref, gosgu_ref,
                    ya_ref, qt_ref, k_ref, vt_ref, qi_ref, ki_ref, wt_ref, kn_ref):
    d_sgu = SGU_GROUPS * SGU_GROUP_DIM
    nq = N_HEADS * HEAD_DIM
    nkv = N_KV_HEADS * HEAD_DIM
    nqi = IDX_HEADS * IDX_DIM
    sh1 = mod_ref[0, 0:1, :]
    sc1 = mod_ref[0, 1:2, :]
    row = lax.broadcasted_iota(jnp.int32, (CHUNK, CHUNK), 0)
    col = lax.broadcasted_iota(jnp.int32, (CHUNK, CHUNK), 1)
    w_mix = [jnp.where(col <= row, wsp_ref[g], 0.0).astype(BF16) for g in range(SGU_GROUPS)]

    def project(t0, tn):
        tok = slice(t0, t0 + tn)
        h = _rms(x_ref[0, tok, :], gpre_ref[...]) * (1.0 + sc1) + sh1
        h_hi32, h_lo32 = _split_bf16(h)
        h_hi, h_lo = h_hi32.astype(BF16), h_lo32.astype(BF16)
        cos, sin = cos_ref[0, :, tok], sin_ref[0, :, tok]

        puv = jnp.dot(h_hi, wuv_ref[...], preferred_element_type=F32)
        pti = _dot_nt(wti_hi_ref[...], h_hi) + _dot_nt(wti_lo_ref[...], h_hi) + _dot_nt(wti_hi_ref[...], h_lo)
        ptm = _dot_nt(wtm_ref[...], h_hi)

        zu = jax.nn.gelu(puv[:, :d_sgu])
        zv = jax.nn.gelu(puv[:, d_sgu:])
        ya_groups = []
        for g in range(SGU_GROUPS):
            sl = slice(g * SGU_GROUP_DIM, (g + 1) * SGU_GROUP_DIM)
            vn = _rms(zv[:, sl], gv_ref[:, sl]).astype(BF16)
            parts = []
            for ci in range(tn // CHUNK):
                rs = slice(ci * CHUNK, (ci + 1) * CHUNK)
                mixed = jnp.dot(w_mix[g], vn[rs], preferred_element_type=F32) + bsp_ref[g]
                parts.append(zu[rs, sl] * mixed)
            ya_groups.append(jnp.concatenate(parts, axis=0))
        ya = jnp.concatenate(ya_groups, axis=1)
        ya_ref[0, tok, :] = _rms(ya, gosgu_ref[...]).astype(BF16)

        for hh in range(N_HEADS):
            rs = slice(hh * HEAD_DIM, (hh + 1) * HEAD_DIM)
            qt_ref[0, rs, tok] = (_rope_rows(ptm[rs], cos, sin) * Q_SCALE).astype(BF16)
        kt = jnp.concatenate(
            [_rope_rows(ptm[nq + g * HEAD_DIM:nq + (g + 1) * HEAD_DIM], cos, sin) for g in range(N_KV_HEADS)],
            axis=0)
        k_ref[0, tok, :] = kt.T.astype(BF16)
        kb = kt.astype(BF16).astype(F32)
        kn_ref[0, :, tok] = jnp.concatenate(
            [jnp.sum(jnp.square(kb[g * HEAD_DIM:(g + 1) * HEAD_DIM]), axis=0, keepdims=True)
             for g in range(N_KV_HEADS)], axis=0)
        vt = ptm[nq + nkv:nq + 2 * nkv].astype(BF16)
        for ci in range(tn // ATT_CH):
            vt_ref[0, t0 // ATT_CH + ci] = vt[:, ci * ATT_CH:(ci + 1) * ATT_CH]

        for hh in range(IDX_HEADS):
            r = _rope_rows(pti[hh * IDX_DIM:(hh + 1) * IDX_DIM], cos, sin)
            r_hi, r_lo = _split_bf16(r)
            qi_ref[0, hh, :, tok] = jnp.concatenate([r_hi, r_lo], axis=0).astype(BF16)
        kr = _rope_rows(pti[nqi:nqi + IDX_DIM], cos, sin)
        k_hi, k_lo = _split_bf16(kr)
        ki_ref[0, tok, :] = jnp.concatenate([k_hi, k_lo, k_hi, k_lo], axis=0).T.astype(BF16)
        wt_ref[0, :, tok] = (pti[nqi + IDX_DIM:nqi + IDX_DIM + IDX_HEADS]
                             * ((IDX_HEADS ** -0.5) * (IDX_DIM ** -0.5)))

    for t0 in range(0, x_ref.shape[1], PROJ_SUB):
        project(t0, PROJ_SUB)


def _in_proj(x, mod3, g_pre, w_uv, wt_main, wt_idx_hi, wt_idx_lo, cos_t, sin_t, g_v, w_sp, b_sp, g_osgu):
    bsz, s, d = x.shape
    tm = PROJ_TM
    d_sgu = SGU_GROUPS * SGU_GROUP_DIM
    nq = N_HEADS * HEAD_DIM
    nkv = N_KV_HEADS * HEAD_DIM
    const2 = lambda b, i: (0, 0)
    const3 = lambda b, i: (0, 0, 0)
    return pl.pallas_call(
        _in_proj_kernel,
        grid=(bsz, s // tm),
        in_specs=[
            pl.BlockSpec((1, tm, d), lambda b, i: (b, i, 0)),
            pl.BlockSpec((1, 6, d), lambda b, i: (b, 0, 0)),
            pl.BlockSpec((1, d), const2),
            pl.BlockSpec(w_uv.shape, const2),
            pl.BlockSpec(wt_main.shape, const2),
            pl.BlockSpec(wt_idx_hi.shape, const2),
            pl.BlockSpec(wt_idx_lo.shape, const2),
            pl.BlockSpec((1, ROT_HALF, tm), lambda b, i: (b, 0, i)),
            pl.BlockSpec((1, ROT_HALF, tm), lambda b, i: (b, 0, i)),
            pl.BlockSpec((1, d_sgu), const2),
            pl.BlockSpec(w_sp.shape, const3),
            pl.BlockSpec(b_sp.shape, const3),
            pl.BlockSpec((1, d_sgu), const2),
        ],
        out_specs=[
            pl.BlockSpec((1, tm, d_sgu), lambda b, i: (b, i, 0)),
            pl.BlockSpec((1, nq, tm), lambda b, i: (b, 0, i)),
            pl.BlockSpec((1, tm, nkv), lambda b, i: (b, i, 0)),
            pl.BlockSpec((1, tm // ATT_CH, nkv, ATT_CH), lambda b, i: (b, i, 0, 0)),
            pl.BlockSpec((1, IDX_HEADS, 2 * IDX_DIM, tm), lambda b, i: (b, 0, 0, i)),
            pl.BlockSpec((1, tm, IDX_K), lambda b, i: (b, i, 0)),
            pl.BlockSpec((1, IDX_HEADS, tm), lambda b, i: (b, 0, i)),
            pl.BlockSpec((1, N_KV_HEADS, tm), lambda b, i: (b, 0, i)),
        ],
        out_shape=[
            jax.ShapeDtypeStruct((bsz, s, d_sgu), BF16),
            jax.ShapeDtypeStruct((bsz, nq, s), BF16),
            jax.ShapeDtypeStruct((bsz, s, nkv), BF16),
            jax.ShapeDtypeStruct((bsz, s // ATT_CH, nkv, ATT_CH), BF16),
            jax.ShapeDtypeStruct((bsz, IDX_HEADS, 2 * IDX_DIM, s), BF16),
            jax.ShapeDtypeStruct((bsz, s, IDX_K), BF16),
            jax.ShapeDtypeStruct((bsz, IDX_HEADS, s), F32),
            jax.ShapeDtypeStruct((bsz, N_KV_HEADS, s), F32),
        ],
        compiler_params=pltpu.CompilerParams(
            dimension_semantics=("arbitrary", "arbitrary"), vmem_limit_bytes=VMEM_LIMIT_BYTES),
        name="in_proj",
    )(x, mod3, g_pre, w_uv, wt_main, wt_idx_hi, wt_idx_lo, cos_t, sin_t, g_v, w_sp, b_sp, g_osgu)


def _attn_kernel(qi_ref, w_ref, ki_ref, q_ref, k_ref, vt_ref, kn_ref, g_ref, o_ref,
                 sc_ref, p_ref, qi4_ref, qpad_ref, m_ref, acc_ref, *, topk):
    nb = qi_ref.shape[0]
    rows = range(nb)
    tq, ch = ATT_TQ, ATT_CH
    j = pl.program_id(1)
    nch = j + 1
    q_pos = j * tq + lax.broadcasted_iota(jnp.int32, (1, tq), 1)
    row_iota = lax.broadcasted_iota(jnp.int32, (ch, tq), 0)

    def fold8(a, op):
        return op(a.reshape(ch // 8, 8, tq), axis=0)

    def chunk_rows(kc):
        return pl.ds(pl.multiple_of(kc * ch, ch), ch)

    for bi in rows:
        for hh in range(IDX_HEADS):
            q_hi, q_lo = qi_ref[bi, hh, 0:IDX_DIM, :], qi_ref[bi, hh, IDX_DIM:, :]
            qi4_ref[bi, hh] = jnp.concatenate([q_hi, q_hi, q_lo, q_lo], axis=0)

    def score_chunk(kc, carry, diagonal):
        out = []
        for bi in rows:
            mn, mx, n_pos, n_nonneg = carry[bi]
            kic = ki_ref[bi, chunk_rows(kc), :]
            acc = jnp.zeros((ch, tq), F32)
            for hh in range(IDX_HEADS):
                dd = jnp.dot(kic, qi4_ref[bi, hh], preferred_element_type=F32)
                acc = acc + w_ref[bi, hh:hh + 1, :] * jnp.maximum(dd, 0.0)
            if diagonal:
                causal = (row_iota + kc * ch) <= q_pos
                sc = jnp.where(causal, acc, -jnp.inf)
                mn = jnp.minimum(mn, fold8(jnp.where(causal, acc, jnp.inf), jnp.min))
            else:
                sc = acc
                mn = jnp.minimum(mn, fold8(acc, jnp.min))
            sc_ref[bi, kc] = sc
            mx = jnp.maximum(mx, fold8(sc, jnp.max))
            n_pos = n_pos + fold8(jnp.where(sc > 0.0, 1.0, 0.0), jnp.sum)
            n_nonneg = n_nonneg + fold8(jnp.where(sc >= 0.0, 1.0, 0.0), jnp.sum)
            out.append((mn, mx, n_pos, n_nonneg))
        return tuple(out)

    stats0 = (jnp.full((8, tq), jnp.inf, F32), jnp.full((8, tq), -jnp.inf, F32),
              jnp.zeros((8, tq), F32), jnp.zeros((8, tq), F32))
    full = functools.partial(score_chunk, diagonal=False)
    stats = lax.fori_loop(0, j // 2, lambda t, c: full(2 * t + 1, full(2 * t, c)), tuple(stats0 for _ in rows))
    stats = lax.cond(j % 2 == 1, lambda c: score_chunk(j, full(j - 1, c), diagonal=True),
                     lambda c: score_chunk(j, c, diagonal=True), stats)

    def count(preds):
        def body(kc, cs):
            return tuple(c + fold8(jnp.where(pred(sc_ref[bi, kc]), 1.0, 0.0), jnp.sum)
                         for bi, (c, pred) in enumerate(zip(cs, preds)))
        c8 = lax.fori_loop(0, nch, body, tuple(jnp.zeros((8, tq), F32) for _ in rows))
        return [jnp.sum(c, axis=0, keepdims=True) for c in c8]

    n_causal = (q_pos + 1).astype(F32)
    kk = jnp.minimum(float(topk), n_causal)

    def as_f(m):
        return jnp.where(m, 1.0, 0.0)

    def initial_state(stat):
        mn = jnp.min(stat[0], axis=0, keepdims=True)
        mx = jnp.max(stat[1], axis=0, keepdims=True)
        c0_gt = jnp.sum(stat[2], axis=0, keepdims=True)
        c0_ge = jnp.sum(stat[3], axis=0, keepdims=True)
        at_zero = (c0_gt < kk) & (c0_ge >= kk)
        take_all = n_causal == kk
        pos = c0_gt >= kk
        lo0 = jnp.where(pos, jnp.maximum(mn, 0.0), mn)
        clo0 = jnp.where(pos & (mn <= 0.0), c0_ge, n_causal)
        hi0 = jnp.where(pos, jnp.minimum(2.0 * mx + TINY, F32_MAX), 0.0)
        thr0 = jnp.where(take_all, mn, 0.0)
        tied0 = jnp.logical_not(take_all) & at_zero & (c0_ge > kk)
        done0 = at_zero | take_all
        return lo0, clo0, hi0, thr0, as_f(done0), as_f(tied0)

    def bisect_step(states):
        mids = [0.5 * st[0] + 0.5 * st[2] for st in states]
        counts = count([lambda s, mid=mid: s >= mid for mid in mids])
        out = []
        for (lo, clo, hi, thr, done, tied), mid, c in zip(states, mids, counts):
            valid = (mid > lo) & (mid < hi)
            move_lo = valid & (c >= kk)
            thr = jnp.where(done, thr, jnp.where(valid, mid, lo))
            tied = tied | (jnp.logical_not(done) & jnp.logical_not(valid) & (clo > kk))
            done = done | jnp.logical_not(valid) | (c == kk)
            hi = jnp.where(valid & (c < kk), mid, hi)
            lo = jnp.where(move_lo, mid, lo)
            clo = jnp.where(move_lo, c, clo)
            out.append((lo, clo, hi, thr, done, tied))
        return out

    def bisect_round(states, steps):
        sts = [(lo, clo, hi, thr, done_f > 0.5, tied_f > 0.5) for lo, clo, hi, thr, done_f, tied_f in states]
        for _ in range(steps):
            sts = bisect_step(sts)
        return tuple((lo, clo, hi, thr, as_f(done), as_f(tied)) for lo, clo, hi, thr, done, tied in sts)

    def bisect_body(carry):
        states = bisect_round(carry[0], BISECT_CHECKED_STEPS)
        return states, jnp.max(1.0 - functools.reduce(jnp.minimum, [st[4] for st in states]))

    blind = jnp.where((j + 1) * tq <= topk, 0, BISECT_BLIND_ROUNDS)
    states = lax.fori_loop(0, blind, lambda _, sts: bisect_round(sts, BISECT_UNROLL),
                           tuple(initial_state(stat) for stat in stats))
    states, _ = lax.while_loop(
        lambda carry: carry[1] > 0.0, bisect_body, (states, jnp.where(blind > 0, 1.0, 0.0)))
    thrs = [st[3] for st in states]
    tied_f = functools.reduce(jnp.maximum, [st[5] for st in states])

    def logit_bound_sq(bi):
        q_sq = jnp.sum(jnp.square(q_ref[bi].astype(F32)).reshape(N_HEADS, HEAD_DIM, tq), axis=1)
        q_sq = jnp.max(jnp.max(q_sq, axis=0, keepdims=True), axis=1, keepdims=True)
        k_sq = jnp.max(jnp.max(kn_ref[bi], axis=0, keepdims=True), axis=1, keepdims=True)
        return q_sq * k_sq
    worst = functools.reduce(jnp.maximum, [logit_bound_sq(bi) for bi in rows])
    bounded_f = jnp.where(worst <= LOGIT_BOUND * LOGIT_BOUND, 2.0, 0.0)

    code = jnp.max(tied_f + bounded_f)
    tie = jnp.logical_or(code == 1.0, code == 3.0)
    bounded = code >= 2.0

    @pl.when(jnp.logical_not(tie))
    def _():
        def body(kc, _):
            for bi in rows:
                sc_ref[bi, kc] = jnp.where(sc_ref[bi, kc] >= thrs[bi], 0.0, NEG)
            return 0
        lax.fori_loop(0, nch, body, 0)

    @pl.when(tie)
    def _():
        r = lax.broadcasted_iota(jnp.int32, (ch, ch), 0)
        c = lax.broadcasted_iota(jnp.int32, (ch, ch), 1)
        below = jnp.where(c < r, 1.0, 0.0).astype(BF16)
        n_gt = count([lambda s, thr=thr: s > thr for thr in thrs])
        needs = [kk - n for n in n_gt]

        def body(kc, seen):
            out = []
            for bi in rows:
                s = sc_ref[bi, kc]
                eq = jnp.where(s == thrs[bi], 1.0, 0.0)
                rank = jnp.dot(below, eq.astype(BF16), preferred_element_type=F32) + seen[bi]
                sel = (s > thrs[bi]) | ((s == thrs[bi]) & (rank < needs[bi]))
                sc_ref[bi, kc] = jnp.where(sel, 0.0, NEG)
                out.append(seen[bi] + jnp.sum(eq, axis=0, keepdims=True))
            return tuple(out)
        lax.fori_loop(0, nch, body, tuple(jnp.zeros((1, tq), F32) for _ in rows))

    zq = jnp.zeros((HEAD_DIM, tq), BF16)
    for bi in rows:
        for hh in range(N_HEADS):
            qh = q_ref[bi, hh * HEAD_DIM:(hh + 1) * HEAD_DIM, :]
            qpad_ref[bi, hh] = jnp.concatenate([qh, zq] if hh < GQA_GROUP else [zq, qh], axis=0)
    acc_ref[...] = jnp.zeros(acc_ref.shape, F32)
    m_ref[...] = jnp.zeros(m_ref.shape, F32)
    ones = jnp.ones((ONES_ROWS, ch), BF16)

    def logits(bi, kc, hh, kcs, bias):
        return jnp.dot(kcs, qpad_ref[bi, hh], preferred_element_type=F32) + bias

    @pl.when(jnp.logical_not(bounded))
    def _():
        def body(kc, ms):
            out = []
            for bi in rows:
                kcs, bias = k_ref[bi, chunk_rows(kc), :], sc_ref[bi, kc]
                out.append(tuple(jnp.maximum(ms[bi][hh], fold8(logits(bi, kc, hh, kcs, bias), jnp.max))
                                 for hh in range(N_HEADS)))
            return tuple(out)
        ms = lax.fori_loop(0, nch, body,
                           tuple(tuple(jnp.full((8, tq), NEG, F32) for _ in range(N_HEADS)) for _ in rows))
        for bi in rows:
            for hh in range(N_HEADS):
                m_ref[bi, hh] = jnp.max(ms[bi][hh], axis=0, keepdims=True)

    def prob_chunk(kc):
        for bi in rows:
            kcs, bias = k_ref[bi, chunk_rows(kc), :], sc_ref[bi, kc]
            for hh in range(N_HEADS):
                p_ref[bi, hh, kc] = jnp.exp2(logits(bi, kc, hh, kcs, bias) - m_ref[bi, hh]).astype(BF16)

    def pv_chunk(kc):
        for bi in rows:
            vt = vt_ref[bi, kc]
            for g in range(N_KV_HEADS):
                vaug = jnp.concatenate([vt[g * HEAD_DIM:(g + 1) * HEAD_DIM], ones], axis=0)
                for hh in range(g * GQA_GROUP, (g + 1) * GQA_GROUP):
                    acc_ref[bi, hh] += jnp.dot(vaug, p_ref[bi, hh, kc], preferred_element_type=F32)

    npair = nch // 2

    @pl.when(npair > 0)
    def _():
        prob_chunk(0)
        prob_chunk(1)

        def attend(t, _):
            pv_chunk(2 * t - 2)
            pv_chunk(2 * t - 1)
            prob_chunk(2 * t)
            prob_chunk(2 * t + 1)
            return 0
        lax.fori_loop(1, npair, attend, 0)
        pv_chunk(2 * npair - 2)
        pv_chunk(2 * npair - 1)

    @pl.when(nch % 2 == 1)
    def _():
        prob_chunk(nch - 1)
        pv_chunk(nch - 1)

    for bi in rows:
        outs = []
        for hh in range(N_HEADS):
            a = acc_ref[bi, hh]
            outs.append(a[0:HEAD_DIM] * (1.0 / a[HEAD_DIM:HEAD_DIM + 1]))
        y = jnp.concatenate(outs, axis=0)
        yn = y * lax.rsqrt(jnp.mean(y * y, axis=0, keepdims=True) + EPS) * g_ref[...]
        o_ref[bi] = yn.T.astype(BF16)


def _dsa_attn(qi3t, wt, ki3, qt, kk, vt, kn, g_attn_b, topk):
    bsz, s, _ = ki3.shape
    tq, ch = ATT_TQ, ATT_CH
    nb = ATT_NB if bsz % ATT_NB == 0 else 1
    nq = N_HEADS * HEAD_DIM
    nkv = N_KV_HEADS * HEAD_DIM
    return pl.pallas_call(
        functools.partial(_attn_kernel, topk=topk),
        grid=(bsz // nb, s // tq),
        in_specs=[
            pl.BlockSpec((nb, IDX_HEADS, 2 * IDX_DIM, tq), lambda b, j: (b, 0, 0, j)),
            pl.BlockSpec((nb, IDX_HEADS, tq), lambda b, j: (b, 0, j)),
            pl.BlockSpec((nb, s, IDX_K), lambda b, j: (b, 0, 0)),
            pl.BlockSpec((nb, nq, tq), lambda b, j: (b, 0, j)),
            pl.BlockSpec((nb, s, nkv), lambda b, j: (b, 0, 0)),
            pl.BlockSpec((nb, s // ch, nkv, ch), lambda b, j: (b, 0, 0, 0)),
            pl.BlockSpec((nb, N_KV_HEADS, s), lambda b, j: (b, 0, 0)),
            pl.BlockSpec((nq, tq), lambda b, j: (0, 0)),
        ],
        out_specs=pl.BlockSpec((nb, tq, nq), lambda b, j: (b, j, 0)),
        out_shape=jax.ShapeDtypeStruct((bsz, s, nq), BF16),
        scratch_shapes=[
            pltpu.VMEM((nb, s // ch, ch, tq), F32),
            pltpu.VMEM((nb, N_HEADS, s // ch, ch, tq), BF16),
            pltpu.VMEM((nb, IDX_HEADS, IDX_K, tq), BF16),
            pltpu.VMEM((nb, N_HEADS, 2 * HEAD_DIM, tq), BF16),
            pltpu.VMEM((nb, N_HEADS, 1, tq), F32),
            pltpu.VMEM((nb, N_HEADS, HEAD_DIM + ONES_ROWS, tq), F32),
        ],
        compiler_params=pltpu.CompilerParams(
            dimension_semantics=("arbitrary", "arbitrary"), vmem_limit_bytes=VMEM_LIMIT_BYTES),
        name="dsa_attn",
    )(qi3t, wt, ki3, qt, kk, vt, kn, g_attn_b)


def _out_ffn_kernel(x_ref, ya_ref, yb_ref, mod_ref, wo_ref, gpm_ref, gpf_ref, w1_ref, w2_ref, gpo_ref, o_ref):
    d_a = ya_ref.shape[2]
    tm = x_ref.shape[1]
    g1 = mod_ref[0, 2:3, :]
    sh2 = mod_ref[0, 3:4, :]
    sc2 = mod_ref[0, 4:5, :]
    g2 = mod_ref[0, 5:6, :]

    def mix_residual(rows):
        o = jnp.dot(ya_ref[0, rows, :], wo_ref[0:d_a, :], preferred_element_type=F32)
        o += jnp.dot(yb_ref[0, rows, :], wo_ref[d_a:, :], preferred_element_type=F32)
        return x_ref[0, rows, :] + g1 * _rms(o, gpm_ref[...])

    def mlp(x1):
        h2 = (_rms(x1, gpf_ref[...]) * (1.0 + sc2) + sh2).astype(BF16)
        f = jnp.zeros(x1.shape, F32)
        for c in range(w1_ref.shape[1] // FFN_TF):
            a = jnp.dot(h2, w1_ref[:, c * FFN_TF:(c + 1) * FFN_TF], preferred_element_type=F32)
            a = jnp.square(jnp.maximum(a, 0.0)).astype(BF16)
            f += jnp.dot(a, w2_ref[c * FFN_TF:(c + 1) * FFN_TF, :], preferred_element_type=F32)
        return f

    halves = [pl.ds(i * (tm // 2), tm // 2) for i in range(2)]
    x1 = [mix_residual(rows) for rows in halves]
    f = [mlp(x1h) for x1h in x1]
    for rows, x1h, fh in zip(halves, x1, f):
        o_ref[0, rows, :] = x1h + g2 * _rms(fh, gpo_ref[...])


def _out_ffn(x, ya, yb, mod3, w_out, g_pm, g_pf, w1, w2, g_po):
    bsz, s, d = x.shape
    tm = FFN_TM
    d_a = ya.shape[2]
    const2 = lambda b, i: (0, 0)
    resident = functools.partial(pl.BlockSpec, index_map=const2, pipeline_mode=pl.Buffered(1))
    return pl.pallas_call(
        _out_ffn_kernel,
        grid=(bsz, s // tm),
        in_specs=[
            pl.BlockSpec((1, tm, d), lambda b, i: (b, i, 0)),
            pl.BlockSpec((1, tm, d_a), lambda b, i: (b, i, 0)),
            pl.BlockSpec((1, tm, yb.shape[2]), lambda b, i: (b, i, 0)),
            pl.BlockSpec((1, 6, d), lambda b, i: (b, 0, 0)),
            resident(w_out.shape),
            pl.BlockSpec((1, d), const2),
            pl.BlockSpec((1, d), const2),
            resident(w1.shape),
            resident(w2.shape),
            pl.BlockSpec((1, d), const2),
        ],
        out_specs=pl.BlockSpec((1, tm, d), lambda b, i: (b, i, 0)),
        out_shape=jax.ShapeDtypeStruct((bsz, s, d), F32),
        compiler_params=pltpu.CompilerParams(
            dimension_semantics=("arbitrary", "arbitrary"), vmem_limit_bytes=VMEM_LIMIT_BYTES),
        name="out_ffn",
    )(x, ya, yb, mod3, w_out, g_pm, g_pf, w1, w2, g_po)


def _layer(x, mod3, cos_t, sin_t, g_pre_mix, w_in, g_sgu_v, w_spatial, b_spatial, g_out_sgu, g_out_attn,
           w_out, g_post_mix, g_pre_ffn, w_ff1, w_ff2, g_post_ffn):
    bsz, s, d = x.shape
    d_sgu = SGU_GROUPS * SGU_GROUP_DIM
    nq = N_HEADS * HEAD_DIM
    nkv = N_KV_HEADS * HEAD_DIM
    nqi = IDX_HEADS * IDX_DIM
    topk = min(TOPK_MAX, s // 4)

    o_q = 2 * d_sgu
    o_qi = o_q + nq + 2 * nkv
    w_uv = w_in[:, :o_q].astype(BF16)
    wt_main = w_in[:, o_q:o_qi].T.astype(BF16)
    n_idx = nqi + IDX_DIM + IDX_HEADS
    wt_idx = jnp.pad(w_in[:, o_qi:o_qi + n_idx].T, ((0, (-n_idx) % 16), (0, 0)))
    wt_idx_hi = wt_idx.astype(BF16)
    wt_idx_lo = (wt_idx - wt_idx_hi.astype(F32)).astype(BF16)
    b_sp = jnp.broadcast_to(b_spatial[:, :, None], (SGU_GROUPS, CHUNK, SGU_GROUP_DIM))

    ya, qt, kk, vt, qi3t, ki3, wt, kn = _in_proj(
        x, mod3, g_pre_mix.reshape(1, d), w_uv, wt_main, wt_idx_hi, wt_idx_lo, cos_t, sin_t,
        g_sgu_v.reshape(1, d_sgu), w_spatial, b_sp, g_out_sgu.reshape(1, d_sgu))

    g_attn_b = jnp.broadcast_to(g_out_attn[:, None], (nq, ATT_TQ))
    yb = _dsa_attn(qi3t, wt, ki3, qt, kk, vt, kn, g_attn_b, topk)

    return _out_ffn(x, ya, yb, mod3, w_out.astype(BF16), g_post_mix.reshape(1, d), g_pre_ffn.reshape(1, d),
                    w_ff1.astype(BF16), w_ff2.astype(BF16), g_post_ffn.reshape(1, d))


def kernel(x, c, positions, w_ada, b_ada, g_pre_mix, w_in, g_sgu_v, w_spatial, b_spatial, g_out_sgu, g_out_attn,
           w_out, g_post_mix, g_pre_ffn, w_ff1, w_ff2, g_post_ffn):
    bsz, s, d = x.shape
    inv_freq = ROPE_THETA ** (-jnp.arange(0, ROT_DIM, 2, dtype=F32) / ROT_DIM)
    ang = positions.astype(F32)[:, None, :] * inv_freq[None, :, None]
    cos_t, sin_t = jnp.cos(ang), jnp.sin(ang)
    for l in range(w_ada.shape[0]):
        mod3 = _ada_mod(c, w_ada[l], b_ada[l]).reshape(bsz, 6, d)
        x = _layer(x, mod3, cos_t, sin_t, g_pre_mix[l], w_in[l], g_sgu_v[l], w_spatial[l], b_spatial[l],
                   g_out_sgu[l], g_out_attn[l], w_out[l], g_post_mix[l], g_pre_ffn[l], w_ff1[l], w_ff2[l],
                   g_post_ffn[l])
    return x
```

```python
import functools

import jax
import jax.numpy as jnp
from jax import lax
from jax.experimental import pallas as pl
from jax.experimental.pallas import tpu as pltpu

F32 = jnp.float32
BF16 = jnp.bfloat16

SGU_GROUPS = 4
SGU_GROUP_DIM = 128
CHUNK = 128
N_HEADS = 8
HEAD_DIM = 64
N_KV_HEADS = 2
GQA_GROUP = N_HEADS // N_KV_HEADS
ROT_DIM = 16
ROT_HALF = ROT_DIM // 2
ROPE_THETA = 500000.0
IDX_HEADS = 8
IDX_DIM = 64
TOPK_MAX = 256
EPS = 1e-6

ADA_TN = 1024
PROJ_TM = 1024
PROJ_SUB = 256
ATT_TQ = 256
ATT_CH = 256
ATT_NB = 2
FFN_TM = 1024
FFN_SUB = 256
FFN_TF = 1024
BISECT_UNROLL = 4
BISECT_BLIND_ROUNDS = 4
BISECT_CHECKED_STEPS = 2
IDX_K = 4 * IDX_DIM
ONES_ROWS = 16
NEG = -1e30
TINY = 1e-30
F32_MAX = 3.4028235e38
LOGIT_BOUND = 96.0
VMEM_LIMIT_BYTES = 56 * 1024 * 1024
Q_SCALE = (HEAD_DIM ** -0.5) * 1.4426950408889634


def _split_bf16(a):
    hi = a.astype(BF16).astype(F32)
    lo = (a - hi).astype(BF16).astype(F32)
    return hi, lo


def _rms(a, g):
    return a * lax.rsqrt(jnp.mean(a * a, axis=-1, keepdims=True) + EPS) * g


def _ada_kernel(c_ref, w_ref, b_ref, o_ref):
    c = c_ref[...]
    ca = c * (1.0 / (1.0 + jnp.exp(-c)))
    c_hi, c_lo = _split_bf16(ca)
    w_hi, w_lo = _split_bf16(w_ref[...])
    c_hi, c_lo, w_hi, w_lo = (t.astype(BF16) for t in (c_hi, c_lo, w_hi, w_lo))
    acc = jnp.dot(c_hi, w_hi, preferred_element_type=F32)
    acc += jnp.dot(c_hi, w_lo, preferred_element_type=F32)
    acc += jnp.dot(c_lo, w_hi, preferred_element_type=F32)
    o_ref[...] = acc + b_ref[...]


def _ada_mod(c, w_ada, b_ada):
    bsz, d = c.shape
    n = w_ada.shape[1]
    return pl.pallas_call(
        _ada_kernel,
        grid=(n // ADA_TN,),
        in_specs=[
            pl.BlockSpec((bsz, d), lambda i: (0, 0)),
            pl.BlockSpec((d, ADA_TN), lambda i: (0, i)),
            pl.BlockSpec((1, ADA_TN), lambda i: (0, i)),
        ],
        out_specs=pl.BlockSpec((bsz, ADA_TN), lambda i: (0, i)),
        out_shape=jax.ShapeDtypeStruct((bsz, n), F32),
        compiler_params=pltpu.CompilerParams(dimension_semantics=("arbitrary",)),
        name="ada_mod",
    )(c, w_ada, b_ada.reshape(1, n))


def _rope_rows(p, cos, sin):
    x1, x2 = p[0:ROT_HALF], p[ROT_HALF:ROT_DIM]
    return jnp.concatenate([x1 * cos - x2 * sin, x2 * cos + x1 * sin, p[ROT_DIM:]], axis=0)


def _dot_nt(a, b):
    return lax.dot_general(a, b, (((1,), (1,)), ((), ())), preferred_element_type=F32)


def _in_proj_kernel(x_ref, mod_ref, gpre_ref, wuv_ref, wtm_ref, wti_hi_ref, wti_lo_ref, cos_ref, sin_ref,
                    gv_ref, wsp_ref, bsp_ref, gosgu_ref,
                    ya_ref, qt_ref, k_ref, vt_ref, qi_ref, ki_ref, wt_ref, kn_ref):
    d_sgu = SGU_GROUPS * SGU_GROUP_DIM
    nq = N_HEADS * HEAD_DIM
    nkv = N_KV_HEADS * HEAD_DIM
    nqi = IDX_HEADS * IDX_DIM
    sh1 = mod_ref[0, 0:1, :]
    sc1 = mod_ref[0, 1:2, :]
    row = lax.broadcasted_iota(jnp.int32, (CHUNK, CHUNK), 0)
    col = lax.broadcasted_iota(jnp.int32, (CHUNK, CHUNK), 1)
    w_mix = [jnp.where(col <= row, wsp_ref[g], 0.0).astype(BF16) for g in range(SGU_GROUPS)]

    def project(t0, tn):
        tok = slice(t0, t0 + tn)
        h = _rms(x_ref[0, tok, :], gpre_ref[...]) * (1.0 + sc1) + sh1
        h_hi32, h_lo32 = _split_bf16(h)
        h_hi, h_lo = h_hi32.astype(BF16), h_lo32.astype(BF16)
        cos, sin = cos_ref[0, :, tok], sin_ref[0, :, tok]

        puv = jnp.dot(h_hi, wuv_ref[...], preferred_element_type=F32)
        pti = _dot_nt(wti_hi_ref[...], h_hi) + _dot_nt(wti_lo_ref[...], h_hi) + _dot_nt(wti_hi_ref[...], h_lo)
        ptm = _dot_nt(wtm_ref[...], h_hi)

        zu = jax.nn.gelu(puv[:, :d_sgu])
        zv = jax.nn.gelu(puv[:, d_sgu:])
        ya_groups = []
        for g in range(SGU_GROUPS):
            sl = slice(g * SGU_GROUP_DIM, (g + 1) * SGU_GROUP_DIM)
            vn = _rms(zv[:, sl], gv_ref[:, sl]).astype(BF16)
            parts = []
            for ci in range(tn // CHUNK):
                rs = slice(ci * CHUNK, (ci + 1) * CHUNK)
                mixed = jnp.dot(w_mix[g], vn[rs], preferred_element_type=F32) + bsp_ref[g]
                parts.append(zu[rs, sl] * mixed)
            ya_groups.append(jnp.concatenate(parts, axis=0))
        ya = jnp.concatenate(ya_groups, axis=1)
        ya_ref[0, tok, :] = _rms(ya, gosgu_ref[...]).astype(BF16)

        for hh in range(N_HEADS):
            rs = slice(hh * HEAD_DIM, (hh + 1) * HEAD_DIM)
            qt_ref[0, rs, tok] = (_rope_rows(ptm[rs], cos, sin) * Q_SCALE).astype(BF16)
        kt = jnp.concatenate(
            [_rope_rows(ptm[nq + g * HEAD_DIM:nq + (g + 1) * HEAD_DIM], cos, sin) for g in range(N_KV_HEADS)],
            axis=0)
        k_ref[0, tok, :] = kt.T.astype(BF16)
        kb = kt.astype(BF16).astype(F32)
        kn_ref[0, :, tok] = jnp.concatenate(
            [jnp.sum(jnp.square(kb[g * HEAD_DIM:(g + 1) * HEAD_DIM]), axis=0, keepdims=True)
             for g in range(N_KV_HEADS)], axis=0)
        vt = ptm[nq + nkv:nq + 2 * nkv].astype(BF16)
        for ci in range(tn // ATT_CH):
            vt_ref[0, t0 // ATT_CH + ci] = vt[:, ci * ATT_CH:(ci + 1) * ATT_CH]

        for hh in range(IDX_HEADS):
            r = _rope_rows(pti[hh * IDX_DIM:(hh + 1) * IDX_DIM], cos, sin)
            r_hi, r_lo = _split_bf16(r)
            qi_ref[0, hh, :, tok] = jnp.concatenate([r_hi, r_lo], axis=0).astype(BF16)
        kr = _rope_rows(pti[nqi:nqi + IDX_DIM], cos, sin)
        k_hi, k_lo = _split_bf16(kr)
        ki_ref[0, tok, :] = jnp.concatenate([k_hi, k_lo, k_hi, k_lo], axis=0).T.astype(BF16)
        wt_ref[0, :, tok] = (pti[nqi + IDX_DIM:nqi + IDX_DIM + IDX_HEADS]
                             * ((IDX_HEADS ** -0.5) * (IDX_DIM ** -0.5)))

    for t0 in range(0, x_ref.shape[1], PROJ_SUB):
        project(t0, PROJ_SUB)


def _in_proj(x, mod3, g_pre, w_uv, wt_main, wt_idx_hi, wt_idx_lo, cos_t, sin_t, g_v, w_sp, b_sp, g_osgu):
    bsz, s, d = x.shape
    tm = PROJ_TM
    d_sgu = SGU_GROUPS * SGU_GROUP_DIM
    nq = N_HEADS * HEAD_DIM
    nkv = N_KV_HEADS * HEAD_DIM
    const2 = lambda b, i: (0, 0)
    const3 = lambda b, i: (0, 0, 0)
    return pl.pallas_call(
        _in_proj_kernel,
        grid=(bsz, s // tm),
        in_specs=[
            pl.BlockSpec((1, tm, d), lambda b, i: (b, i, 0)),
            pl.BlockSpec((1, 6, d), lambda b, i: (b, 0, 0)),
            pl.BlockSpec((1, d), const2),
            pl.BlockSpec(w_uv.shape, const2),
            pl.BlockSpec(wt_main.shape, const2),
            pl.BlockSpec(wt_idx_hi.shape, const2),
            pl.BlockSpec(wt_idx_lo.shape, const2),
            pl.BlockSpec((1, ROT_HALF, tm), lambda b, i: (b, 0, i)),
            pl.BlockSpec((1, ROT_HALF, tm), lambda b, i: (b, 0, i)),
            pl.BlockSpec((1, d_sgu), const2),
            pl.BlockSpec(w_sp.shape, const3),
            pl.BlockSpec(b_sp.shape, const3),
            pl.BlockSpec((1, d_sgu), const2),
        ],
        out_specs=[
            pl.BlockSpec((1, tm, d_sgu), lambda b, i: (b, i, 0)),
            pl.BlockSpec((1, nq, tm), lambda b, i: (b, 0, i)),
            pl.BlockSpec((1, tm, nkv), lambda b, i: (b, i, 0)),
            pl.BlockSpec((1, tm // ATT_CH, nkv, ATT_CH), lambda b, i: (b, i, 0, 0)),
            pl.BlockSpec((1, IDX_HEADS, 2 * IDX_DIM, tm), lambda b, i: (b, 0, 0, i)),
            pl.BlockSpec((1, tm, IDX_K), lambda b, i: (b, i, 0)),
            pl.BlockSpec((1, IDX_HEADS, tm), lambda b, i: (b, 0, i)),
            pl.BlockSpec((1, N_KV_HEADS, tm), lambda b, i: (b, 0, i)),
        ],
        out_shape=[
            jax.ShapeDtypeStruct((bsz, s, d_sgu), BF16),
            jax.ShapeDtypeStruct((bsz, nq, s), BF16),
            jax.ShapeDtypeStruct((bsz, s, nkv), BF16),
            jax.ShapeDtypeStruct((bsz, s // ATT_CH, nkv, ATT_CH), BF16),
            jax.ShapeDtypeStruct((bsz, IDX_HEADS, 2 * IDX_DIM, s), BF16),
            jax.ShapeDtypeStruct((bsz, s, IDX_K), BF16),
            jax.ShapeDtypeStruct((bsz, IDX_HEADS, s), F32),
            jax.ShapeDtypeStruct((bsz, N_KV_HEADS, s), F32),
        ],
        compiler_params=pltpu.CompilerParams(
            dimension_semantics=("arbitrary", "arbitrary"), vmem_limit_bytes=VMEM_LIMIT_BYTES),
        name="in_proj",
    )(x, mod3, g_pre, w_uv, wt_main, wt_idx_hi, wt_idx_lo, cos_t, sin_t, g_v, w_sp, b_sp, g_osgu)


def _attn_kernel(qi_ref, w_ref, ki_ref, q_ref, k_ref, vt_ref, kn_ref, g_ref, o_ref,
                 sc_ref, p_ref, qi4_ref, qpad_ref, m_ref, acc_ref, *, topk):
    nb = qi_ref.shape[0]
    rows = range(nb)
    tq, ch = ATT_TQ, ATT_CH
    j = pl.program_id(1)
    nch = j + 1
    q_pos = j * tq + lax.broadcasted_iota(jnp.int32, (1, tq), 1)
    row_iota = lax.broadcasted_iota(jnp.int32, (ch, tq), 0)

    def fold8(a, op):
        return op(a.reshape(ch // 8, 8, tq), axis=0)

    def chunk_rows(kc):
        return pl.ds(pl.multiple_of(kc * ch, ch), ch)

    for bi in rows:
        for hh in range(IDX_HEADS):
            q_hi, q_lo = qi_ref[bi, hh, 0:IDX_DIM, :], qi_ref[bi, hh, IDX_DIM:, :]
            qi4_ref[bi, hh] = jnp.concatenate([q_hi, q_hi, q_lo, q_lo], axis=0)

    def score_chunk(kc, carry, diagonal):
        out = []
        for bi in rows:
            mn, mx, n_pos, n_nonneg = carry[bi]
            kic = ki_ref[bi, chunk_rows(kc), :]
            acc = jnp.zeros((ch, tq), F32)
            for hh in range(IDX_HEADS):
                dd = jnp.dot(kic, qi4_ref[bi, hh], preferred_element_type=F32)
                acc = acc + w_ref[bi, hh:hh + 1, :] * jnp.maximum(dd, 0.0)
            if diagonal:
                causal = (row_iota + kc * ch) <= q_pos
                sc = jnp.where(causal, acc, -jnp.inf)
                mn = jnp.minimum(mn, fold8(jnp.where(causal, acc, jnp.inf), jnp.min))
            else:
                sc = acc
                mn = jnp.minimum(mn, fold8(acc, jnp.min))
            sc_ref[bi, kc] = sc
            mx = jnp.maximum(mx, fold8(sc, jnp.max))
            n_pos = n_pos + fold8(jnp.where(sc > 0.0, 1.0, 0.0), jnp.sum)
            n_nonneg = n_nonneg + fold8(jnp.where(sc >= 0.0, 1.0, 0.0), jnp.sum)
            out.append((mn, mx, n_pos, n_nonneg))
        return tuple(out)

    stats0 = (jnp.full((8, tq), jnp.inf, F32), jnp.full((8, tq), -jnp.inf, F32),
              jnp.zeros((8, tq), F32), jnp.zeros((8, tq), F32))
    full = functools.partial(score_chunk, diagonal=False)
    stats = lax.fori_loop(0, j // 2, lambda t, c: full(2 * t + 1, full(2 * t, c)), tuple(stats0 for _ in rows))
    stats = lax.cond(j % 2 == 1, lambda c: score_chunk(j, full(j - 1, c), diagonal=True),
                     lambda c: score_chunk(j, c, diagonal=True), stats)

    def count(preds):
        def body(kc, cs):
            return tuple(c + fold8(jnp.where(pred(sc_ref[bi, kc]), 1.0, 0.0), jnp.sum)
                         for bi, (c, pred) in enumerate(zip(cs, preds)))
        c8 = lax.fori_loop(0, nch, body, tuple(jnp.zeros((8, tq), F32) for _ in rows))
        return [jnp.sum(c, axis=0, keepdims=True) for c in c8]

    n_causal = (q_pos + 1).astype(F32)
    kk = jnp.minimum(float(topk), n_causal)

    def as_f(m):
        return jnp.where(m, 1.0, 0.0)

    def initial_state(stat):
        mn = jnp.min(stat[0], axis=0, keepdims=True)
        mx = jnp.max(stat[1], axis=0, keepdims=True)
        c0_gt = jnp.sum(stat[2], axis=0, keepdims=True)
        c0_ge = jnp.sum(stat[3], axis=0, keepdims=True)
        at_zero = (c0_gt < kk) & (c0_ge >= kk)
        take_all = n_causal == kk
        pos = c0_gt >= kk
        lo0 = jnp.where(pos, jnp.maximum(mn, 0.0), mn)
        clo0 = jnp.where(pos & (mn <= 0.0), c0_ge, n_causal)
        hi0 = jnp.where(pos, jnp.minimum(2.0 * mx + TINY, F32_MAX), 0.0)
        thr0 = jnp.where(take_all, mn, 0.0)
        tied0 = jnp.logical_not(take_all) & at_zero & (c0_ge > kk)
        done0 = at_zero | take_all
        return lo0, clo0, hi0, thr0, as_f(done0), as_f(tied0)

    def bisect_step(states):
        mids = [0.5 * st[0] + 0.5 * st[2] for st in states]
        counts = count([lambda s, mid=mid: s >= mid for mid in mids])
        out = []
        for (lo, clo, hi, thr, done, tied), mid, c in zip(states, mids, counts):
            valid = (mid > lo) & (mid < hi)
            move_lo = valid & (c >= kk)
            thr = jnp.where(done, thr, jnp.where(valid, mid, lo))
            tied = tied | (jnp.logical_not(done) & jnp.logical_not(valid) & (clo > kk))
            done = done | jnp.logical_not(valid) | (c == kk)
            hi = jnp.where(valid & (c < kk), mid, hi)
            lo = jnp.where(move_lo, mid, lo)
            clo = jnp.where(move_lo, c, clo)
            out.append((lo, clo, hi, thr, done, tied))
        return out

    def bisect_round(states, steps):
        sts = [(lo, clo, hi, thr, done_f > 0.5, tied_f > 0.5) for lo, clo, hi, thr, done_f, tied_f in states]
        for _ in range(steps):
            sts = bisect_step(sts)
        return tuple((lo, clo, hi, thr, as_f(done), as_f(tied)) for lo, clo, hi, thr, done, tied in sts)

    def bisect_body(carry):
        states = bisect_round(carry[0], BISECT_CHECKED_STEPS)
        return states, jnp.max(1.0 - functools.reduce(jnp.minimum, [st[4] for st in states]))

    blind = jnp.where((j + 1) * tq <= topk, 0, BISECT_BLIND_ROUNDS)
    states = lax.fori_loop(0, blind, lambda _, sts: bisect_round(sts, BISECT_UNROLL),
                           tuple(initial_state(stat) for stat in stats))
    states, _ = lax.while_loop(
        lambda carry: carry[1] > 0.0, bisect_body, (states, jnp.where(blind > 0, 1.0, 0.0)))
    thrs = [st[3] for st in states]
    tied_f = functools.reduce(jnp.maximum, [st[5] for st in states])

    def logit_bound_sq(bi):
        q_sq = jnp.sum(jnp.square(q_ref[bi].astype(F32)).reshape(N_HEADS, HEAD_DIM, tq), axis=1)
        q_sq = jnp.max(jnp.max(q_sq, axis=0, keepdims=True), axis=1, keepdims=True)
        k_sq = jnp.max(jnp.max(kn_ref[bi], axis=0, keepdims=True), axis=1, keepdims=True)
        return q_sq * k_sq
    worst = functools.reduce(jnp.maximum, [logit_bound_sq(bi) for bi in rows])
    bounded_f = jnp.where(worst <= LOGIT_BOUND * LOGIT_BOUND, 2.0, 0.0)

    code = jnp.max(tied_f + bounded_f)
    tie = jnp.logical_or(code == 1.0, code == 3.0)
    bounded = code >= 2.0

    @pl.when(jnp.logical_not(tie))
    def _():
        def body(kc, _):
            for bi in rows:
                sc_ref[bi, kc] = jnp.where(sc_ref[bi, kc] >= thrs[bi], 0.0, NEG)
            return 0
        lax.fori_loop(0, nch, body, 0)

    @pl.when(tie)
    def _():
        r = lax.broadcasted_iota(jnp.int32, (ch, ch), 0)
        c = lax.broadcasted_iota(jnp.int32, (ch, ch), 1)
        below = jnp.where(c < r, 1.0, 0.0).astype(BF16)
        n_gt = count([lambda s, thr=thr: s > thr for thr in thrs])
        needs = [kk - n for n in n_gt]

        def body(kc, seen):
            out = []
            for bi in rows:
                s = sc_ref[bi, kc]
                eq = jnp.where(s == thrs[bi], 1.0, 0.0)
                rank = jnp.dot(below, eq.astype(BF16), preferred_element_type=F32) + seen[bi]
                sel = (s > thrs[bi]) | ((s == thrs[bi]) & (rank < needs[bi]))
                sc_ref[bi, kc] = jnp.where(sel, 0.0, NEG)
                out.append(seen[bi] + jnp.sum(eq, axis=0, keepdims=True))
            return tuple(out)
        lax.fori_loop(0, nch, body, tuple(jnp.zeros((1, tq), F32) for _ in rows))

    zq = jnp.zeros((HEAD_DIM, tq), BF16)
    for bi in rows:
        for hh in range(N_HEADS):
            qh = q_ref[bi, hh * HEAD_DIM:(hh + 1) * HEAD_DIM, :]
            qpad_ref[bi, hh] = jnp.concatenate([qh, zq] if hh < GQA_GROUP else [zq, qh], axis=0)
    acc_ref[...] = jnp.zeros(acc_ref.shape, F32)
    m_ref[...] = jnp.zeros(m_ref.shape, F32)
    ones = jnp.ones((ONES_ROWS, ch), BF16)

    def logits(bi, kc, hh, kcs, bias):
        return jnp.dot(kcs, qpad_ref[bi, hh], preferred_element_type=F32) + bias

    @pl.when(jnp.logical_not(bounded))
    def _():
        def body(kc, ms):
            out = []
            for bi in rows:
                kcs, bias = k_ref[bi, chunk_rows(kc), :], sc_ref[bi, kc]
                out.append(tuple(jnp.maximum(ms[bi][hh], fold8(logits(bi, kc, hh, kcs, bias), jnp.max))
                                 for hh in range(N_HEADS)))
            return tuple(out)
        ms = lax.fori_loop(0, nch, body,
                           tuple(tuple(jnp.full((8, tq), NEG, F32) for _ in range(N_HEADS)) for _ in rows))
        for bi in rows:
            for hh in range(N_HEADS):
                m_ref[bi, hh] = jnp.max(ms[bi][hh], axis=0, keepdims=True)

    def prob_chunk(kc):
        for bi in rows:
            kcs, bias = k_ref[bi, chunk_rows(kc), :], sc_ref[bi, kc]
            for hh in range(N_HEADS):
                p_ref[bi, hh, kc] = jnp.exp2(logits(bi, kc, hh, kcs, bias) - m_ref[bi, hh]).astype(BF16)

    def pv_chunk(kc):
        for bi in rows:
            vt = vt_ref[bi, kc]
            for g in range(N_KV_HEADS):
                vaug = jnp.concatenate([vt[g * HEAD_DIM:(g + 1) * HEAD_DIM], ones], axis=0)
                for hh in range(g * GQA_GROUP, (g + 1) * GQA_GROUP):
                    acc_ref[bi, hh] += jnp.dot(vaug, p_ref[bi, hh, kc], preferred_element_type=F32)

    npair = nch // 2

    @pl.when(npair > 0)
    def _():
        prob_chunk(0)
        prob_chunk(1)

        def attend(t, _):
            pv_chunk(2 * t - 2)
            pv_chunk(2 * t - 1)
            prob_chunk(2 * t)
            prob_chunk(2 * t + 1)
            return 0
        lax.fori_loop(1, npair, attend, 0)
        pv_chunk(2 * npair - 2)
        pv_chunk(2 * npair - 1)

    @pl.when(nch % 2 == 1)
    def _():
        prob_chunk(nch - 1)
        pv_chunk(nch - 1)

    for bi in rows:
        outs = []
        for hh in range(N_HEADS):
            a = acc_ref[bi, hh]
            outs.append(a[0:HEAD_DIM] * (1.0 / a[HEAD_DIM:HEAD_DIM + 1]))
        y = jnp.concatenate(outs, axis=0)
        yn = y * lax.rsqrt(jnp.mean(y * y, axis=0, keepdims=True) + EPS) * g_ref[...]
        o_ref[bi] = yn.T.astype(BF16)


def _dsa_attn(qi3t, wt, ki3, qt, kk, vt, kn, g_attn_b, topk):
    bsz, s, _ = ki3.shape
    tq, ch = ATT_TQ, ATT_CH
    nb = ATT_NB if bsz % ATT_NB == 0 else 1
    nq = N_HEADS * HEAD_DIM
    nkv = N_KV_HEADS * HEAD_DIM
    return pl.pallas_call(
        functools.partial(_attn_kernel, topk=topk),
        grid=(bsz // nb, s // tq),
        in_specs=[
            pl.BlockSpec((nb, IDX_HEADS, 2 * IDX_DIM, tq), lambda b, j: (b, 0, 0, j)),
            pl.BlockSpec((nb, IDX_HEADS, tq), lambda b, j: (b, 0, j)),
            pl.BlockSpec((nb, s, IDX_K), lambda b, j: (b, 0, 0)),
            pl.BlockSpec((nb, nq, tq), lambda b, j: (b, 0, j)),
            pl.BlockSpec((nb, s, nkv), lambda b, j: (b, 0, 0)),
            pl.BlockSpec((nb, s // ch, nkv, ch), lambda b, j: (b, 0, 0, 0)),
            pl.BlockSpec((nb, N_KV_HEADS, s), lambda b, j: (b, 0, 0)),
            pl.BlockSpec((nq, tq), lambda b, j: (0, 0)),
        ],
        out_specs=pl.BlockSpec((nb, tq, nq), lambda b, j: (b, j, 0)),
        out_shape=jax.ShapeDtypeStruct((bsz, s, nq), BF16),
        scratch_shapes=[
            pltpu.VMEM((nb, s // ch, ch, tq), F32),
            pltpu.VMEM((nb, N_HEADS, s // ch, ch, tq), BF16),
            pltpu.VMEM((nb, IDX_HEADS, IDX_K, tq), BF16),
            pltpu.VMEM((nb, N_HEADS, 2 * HEAD_DIM, tq), BF16),
            pltpu.VMEM((nb, N_HEADS, 1, tq), F32),
            pltpu.VMEM((nb, N_HEADS, HEAD_DIM + ONES_ROWS, tq), F32),
        ],
        compiler_params=pltpu.CompilerParams(
            dimension_semantics=("arbitrary", "arbitrary"), vmem_limit_bytes=VMEM_LIMIT_BYTES),
        name="dsa_attn",
    )(qi3t, wt, ki3, qt, kk, vt, kn, g_attn_b)


def _out_ffn_kernel(x_ref, ya_ref, yb_ref, mod_ref, wo_ref, gpm_ref, gpf_ref, w1_ref, w2_ref, gpo_ref, o_ref):
    d_a = ya_ref.shape[2]
    tm = x_ref.shape[1]
    g1 = mod_ref[0, 2:3, :]
    sh2 = mod_ref[0, 3:4, :]
    sc2 = mod_ref[0, 4:5, :]
    g2 = mod_ref[0, 5:6, :]

    def mix_residual(rows):
        o = jnp.dot(ya_ref[0, rows, :], wo_ref[0:d_a, :], preferred_element_type=F32)
        o += jnp.dot(yb_ref[0, rows, :], wo_ref[d_a:, :], preferred_element_type=F32)
        return x_ref[0, rows, :] + g1 * _rms(o, gpm_ref[...])

    def mlp(x1):
        h2 = (_rms(x1, gpf_ref[...]) * (1.0 + sc2) + sh2).astype(BF16)
        f = jnp.zeros(x1.shape, F32)
        for c in range(w1_ref.shape[1] // FFN_TF):
            a = jnp.dot(h2, w1_ref[:, c * FFN_TF:(c + 1) * FFN_TF], preferred_element_type=F32)
            a = jnp.square(jnp.maximum(a, 0.0)).astype(BF16)
            f += jnp.dot(a, w2_ref[c * FFN_TF:(c + 1) * FFN_TF, :], preferred_element_type=F32)
        return f

    subs = [pl.ds(t0, FFN_SUB) for t0 in range(0, tm, FFN_SUB)]
    x1_next = mix_residual(subs[0])
    for i, rows in enumerate(subs):
        x1 = x1_next
        if i + 1 < len(subs):
            x1_next = mix_residual(subs[i + 1])
        o_ref[0, rows, :] = x1 + g2 * _rms(mlp(x1), gpo_ref[...])


def _out_ffn(x, ya, yb, mod3, w_out, g_pm, g_pf, w1, w2, g_po):
    bsz, s, d = x.shape
    tm = FFN_TM
    d_a = ya.shape[2]
    const2 = lambda b, i: (0, 0)
    resident = functools.partial(pl.BlockSpec, index_map=const2, pipeline_mode=pl.Buffered(1))
    return pl.pallas_call(
        _out_ffn_kernel,
        grid=(bsz, s // tm),
        in_specs=[
            pl.BlockSpec((1, tm, d), lambda b, i: (b, i, 0)),
            pl.BlockSpec((1, tm, d_a), lambda b, i: (b, i, 0)),
            pl.BlockSpec((1, tm, yb.shape[2]), lambda b, i: (b, i, 0)),
            pl.BlockSpec((1, 6, d), lambda b, i: (b, 0, 0)),
            resident(w_out.shape),
            pl.BlockSpec((1, d), const2),
            pl.BlockSpec((1, d), const2),
            resident(w1.shape),
            resident(w2.shape),
            pl.BlockSpec((1, d), const2),
        ],
        out_specs=pl.BlockSpec((1, tm, d), lambda b, i: (b, i, 0)),
        out_shape=jax.ShapeDtypeStruct((bsz, s, d), F32),
        compiler_params=pltpu.CompilerParams(
            dimension_semantics=("arbitrary", "arbitrary"), vmem_limit_bytes=VMEM_LIMIT_BYTES),
        name="out_ffn",
    )(x, ya, yb, mod3, w_out, g_pm, g_pf, w1, w2, g_po)


def _layer(x, mod3, cos_t, sin_t, g_pre_mix, w_in, g_sgu_v, w_spatial, b_spatial, g_out_sgu, g_out_attn,
           w_out, g_post_mix, g_pre_ffn, w_ff1, w_ff2, g_post_ffn):
    bsz, s, d = x.shape
    d_sgu = SGU_GROUPS * SGU_GROUP_DIM
    nq = N_HEADS * HEAD_DIM
    nkv = N_KV_HEADS * HEAD_DIM
    nqi = IDX_HEADS * IDX_DIM
    topk = min(TOPK_MAX, s // 4)

    o_q = 2 * d_sgu
    o_qi = o_q + nq + 2 * nkv
    w_uv = w_in[:, :o_q].astype(BF16)
    wt_main = w_in[:, o_q:o_qi].T.astype(BF16)
    n_idx = nqi + IDX_DIM + IDX_HEADS
    wt_idx = jnp.pad(w_in[:, o_qi:o_qi + n_idx].T, ((0, (-n_idx) % 16), (0, 0)))
    wt_idx_hi = wt_idx.astype(BF16)
    wt_idx_lo = (wt_idx - wt_idx_hi.astype(F32)).astype(BF16)
    b_sp = jnp.broadcast_to(b_spatial[:, :, None], (SGU_GROUPS, CHUNK, SGU_GROUP_DIM))

    ya, qt, kk, vt, qi3t, ki3, wt, kn = _in_proj(
        x, mod3, g_pre_mix.reshape(1, d), w_uv, wt_main, wt_idx_hi, wt_idx_lo, cos_t, sin_t,
        g_sgu_v.reshape(1, d_sgu), w_spatial, b_sp, g_out_sgu.reshape(1, d_sgu))

    g_attn_b = jnp.broadcast_to(g_out_attn[:, None], (nq, ATT_TQ))
    yb = _dsa_attn(qi3t, wt, ki3, qt, kk, vt, kn, g_attn_b, topk)

    return _out_ffn(x, ya, yb, mod3, w_out.astype(BF16), g_post_mix.reshape(1, d), g_pre_ffn.reshape(1, d),
                    w_ff1.astype(BF16), w_ff2.astype(BF16), g_post_ffn.reshape(1, d))


def kernel(x, c, positions, w_ada, b_ada, g_pre_mix, w_in, g_sgu_v, w_spatial, b_spatial, g_out_sgu, g_out_attn,
           w_out, g_post_mix, g_pre_ffn, w_ff1, w_ff2, g_post_ffn):
    bsz, s, d = x.shape
    inv_freq = ROPE_THETA ** (-jnp.arange(0, ROT_DIM, 2, dtype=F32) / ROT_DIM)
    ang = positions.astype(F32)[:, None, :] * inv_freq[None, :, None]
    cos_t, sin_t = jnp.cos(ang), jnp.sin(ang)
    for l in range(w_ada.shape[0]):
        mod3 = _ada_mod(c, w_ada[l], b_ada[l]).reshape(bsz, 6, d)
        x = _layer(x, mod3, cos_t, sin_t, g_pre_mix[l], w_in[l], g_sgu_v[l], w_spatial[l], b_spatial[l],
                   g_out_sgu[l], g_out_attn[l], w_out[l], g_post_mix[l], g_pre_ffn[l], w_ff1[l], w_ff2[l],
                   g_post_ffn[l])
    return x
```

```python
import functools

import jax
import jax.numpy as jnp
from jax import lax
from jax.experimental import pallas as pl
from jax.experimental.pallas import tpu as pltpu

F32 = jnp.float32
BF16 = jnp.bfloat16

SGU_GROUPS = 4
SGU_GROUP_DIM = 128
CHUNK = 128
N_HEADS = 8
HEAD_DIM = 64
N_KV_HEADS = 2
GQA_GROUP = N_HEADS // N_KV_HEADS
ROT_DIM = 16
ROT_HALF = ROT_DIM // 2
ROPE_THETA = 500000.0
IDX_HEADS = 8
IDX_DIM = 64
TOPK_MAX = 256
EPS = 1e-6

ADA_TN = 1024
PROJ_TM = 1024
PROJ_SUB = 256
ATT_TQ = 256
ATT_CH = 256
ATT_NB = 2
FFN_TM = 512
FFN_TF = 1024
BISECT_UNROLL = 4
BISECT_BLIND_ROUNDS = 4
BISECT_CHECKED_STEPS = 2
IDX_K = 4 * IDX_DIM
ONES_ROWS = 16
NEG = -1e30
TINY = 1e-30
JUST_ABOVE = 1.0 + 2.0 ** -20
F32_MAX = 3.4028235e38
LOGIT_BOUND = 96.0
VMEM_LIMIT_BYTES = 56 * 1024 * 1024
Q_SCALE = (HEAD_DIM ** -0.5) * 1.4426950408889634


def _split_bf16(a):
    hi = a.astype(BF16).astype(F32)
    lo = (a - hi).astype(BF16).astype(F32)
    return hi, lo


def _rms(a, g):
    return a * lax.rsqrt(jnp.mean(a * a, axis=-1, keepdims=True) + EPS) * g


def _ada_kernel(c_ref, w_ref, b_ref, o_ref):
    c = c_ref[...]
    ca = c * (1.0 / (1.0 + jnp.exp(-c)))
    c_hi, c_lo = _split_bf16(ca)
    w_hi, w_lo = _split_bf16(w_ref[...])
    c_hi, c_lo, w_hi, w_lo = (t.astype(BF16) for t in (c_hi, c_lo, w_hi, w_lo))
    acc = jnp.dot(c_hi, w_hi, preferred_element_type=F32)
    acc += jnp.dot(c_hi, w_lo, preferred_element_type=F32)
    acc += jnp.dot(c_lo, w_hi, preferred_element_type=F32)
    o_ref[...] = acc + b_ref[...]


def _ada_mod(c, w_ada, b_ada):
    bsz, d = c.shape
    n = w_ada.shape[1]
    return pl.pallas_call(
        _ada_kernel,
        grid=(n // ADA_TN,),
        in_specs=[
            pl.BlockSpec((bsz, d), lambda i: (0, 0)),
            pl.BlockSpec((d, ADA_TN), lambda i: (0, i)),
            pl.BlockSpec((1, ADA_TN), lambda i: (0, i)),
        ],
        out_specs=pl.BlockSpec((bsz, ADA_TN), lambda i: (0, i)),
        out_shape=jax.ShapeDtypeStruct((bsz, n), F32),
        compiler_params=pltpu.CompilerParams(dimension_semantics=("arbitrary",)),
        name="ada_mod",
    )(c, w_ada, b_ada.reshape(1, n))


def _rope_rows(p, cos, sin):
    x1, x2 = p[0:ROT_HALF], p[ROT_HALF:ROT_DIM]
    return jnp.concatenate([x1 * cos - x2 * sin, x2 * cos + x1 * sin, p[ROT_DIM:]], axis=0)


def _dot_nt(a, b):
    return lax.dot_general(a, b, (((1,), (1,)), ((), ())), preferred_element_type=F32)


def _in_proj_kernel(x_ref, mod_ref, gpre_ref, wuv_ref, wtm_ref, wti_hi_ref, wti_lo_ref, cos_ref, sin_ref,
                    gv_ref, wsp_ref, bsp_ref, gosgu_ref,
                    ya_ref, qt_ref, k_ref, vt_ref, qi_ref, ki_ref, wt_ref, kn_ref):
    d_sgu = SGU_GROUPS * SGU_GROUP_DIM
    nq = N_HEADS * HEAD_DIM
    nkv = N_KV_HEADS * HEAD_DIM
    nqi = IDX_HEADS * IDX_DIM
    sh1 = mod_ref[0, 0:1, :]
    sc1 = mod_ref[0, 1:2, :]
    row = lax.broadcasted_iota(jnp.int32, (CHUNK, CHUNK), 0)
    col = lax.broadcasted_iota(jnp.int32, (CHUNK, CHUNK), 1)
    w_mix = [jnp.where(col <= row, wsp_ref[g], 0.0).astype(BF16) for g in range(SGU_GROUPS)]

    def project(t0, tn):
        tok = slice(t0, t0 + tn)
        h = _rms(x_ref[0, tok, :], gpre_ref[...]) * (1.0 + sc1) + sh1
        h_hi32, h_lo32 = _split_bf16(h)
        h_hi, h_lo = h_hi32.astype(BF16), h_lo32.astype(BF16)
        cos, sin = cos_ref[0, :, tok], sin_ref[0, :, tok]

        puv = jnp.dot(h_hi, wuv_ref[...], preferred_element_type=F32)
        pti = _dot_nt(wti_hi_ref[...], h_hi) + _dot_nt(wti_lo_ref[...], h_hi) + _dot_nt(wti_hi_ref[...], h_lo)
        ptm = _dot_nt(wtm_ref[...], h_hi)

        zu = jax.nn.gelu(puv[:, :d_sgu])
        zv = jax.nn.gelu(puv[:, d_sgu:])
        ya_groups = []
        for g in range(SGU_GROUPS):
            sl = slice(g * SGU_GROUP_DIM, (g + 1) * SGU_GROUP_DIM)
            vn = _rms(zv[:, sl], gv_ref[:, sl]).astype(BF16)
            parts = []
            for ci in range(tn // CHUNK):
                rs = slice(ci * CHUNK, (ci + 1) * CHUNK)
                mixed = jnp.dot(w_mix[g], vn[rs], preferred_element_type=F32) + bsp_ref[g]
                parts.append(zu[rs, sl] * mixed)
            ya_groups.append(jnp.concatenate(parts, axis=0))
        ya = jnp.concatenate(ya_groups, axis=1)
        ya_ref[0, tok, :] = _rms(ya, gosgu_ref[...]).astype(BF16)

        for hh in range(N_HEADS):
            rs = slice(hh * HEAD_DIM, (hh + 1) * HEAD_DIM)
            qt_ref[0, rs, tok] = (_rope_rows(ptm[rs], cos, sin) * Q_SCALE).astype(BF16)
        kt = jnp.concatenate(
            [_rope_rows(ptm[nq + g * HEAD_DIM:nq + (g + 1) * HEAD_DIM], cos, sin) for g in range(N_KV_HEADS)],
            axis=0)
        k_ref[0, tok, :] = kt.T.astype(BF16)
        kb = kt.astype(BF16).astype(F32)
        kn_ref[0, :, tok] = jnp.concatenate(
            [jnp.sum(jnp.square(kb[g * HEAD_DIM:(g + 1) * HEAD_DIM]), axis=0, keepdims=True)
             for g in range(N_KV_HEADS)], axis=0)
        vt = ptm[nq + nkv:nq + 2 * nkv].astype(BF16)
        for ci in range(tn // ATT_CH):
            vt_ref[0, t0 // ATT_CH + ci] = vt[:, ci * ATT_CH:(ci + 1) * ATT_CH]

        for hh in range(IDX_HEADS):
            r = _rope_rows(pti[hh * IDX_DIM:(hh + 1) * IDX_DIM], cos, sin)
            r_hi, r_lo = _split_bf16(r)
            qi_ref[0, hh, :, tok] = jnp.concatenate([r_hi, r_lo], axis=0).astype(BF16)
        kr = _rope_rows(pti[nqi:nqi + IDX_DIM], cos, sin)
        k_hi, k_lo = _split_bf16(kr)
        ki_ref[0, tok, :] = jnp.concatenate([k_hi, k_lo, k_hi, k_lo], axis=0).T.astype(BF16)
        wt_ref[0, :, tok] = (pti[nqi + IDX_DIM:nqi + IDX_DIM + IDX_HEADS]
                             * ((IDX_HEADS ** -0.5) * (IDX_DIM ** -0.5)))

    for t0 in range(0, x_ref.shape[1], PROJ_SUB):
        project(t0, PROJ_SUB)


def _in_proj(x, mod3, g_pre, w_uv, wt_main, wt_idx_hi, wt_idx_lo, cos_t, sin_t, g_v, w_sp, b_sp, g_osgu):
    bsz, s, d = x.shape
    tm = PROJ_TM
    d_sgu = SGU_GROUPS * SGU_GROUP_DIM
    nq = N_HEADS * HEAD_DIM
    nkv = N_KV_HEADS * HEAD_DIM
    const2 = lambda b, i: (0, 0)
    const3 = lambda b, i: (0, 0, 0)
    return pl.pallas_call(
        _in_proj_kernel,
        grid=(bsz, s // tm),
        in_specs=[
            pl.BlockSpec((1, tm, d), lambda b, i: (b, i, 0)),
            pl.BlockSpec((1, 6, d), lambda b, i: (b, 0, 0)),
            pl.BlockSpec((1, d), const2),
            pl.BlockSpec(w_uv.shape, const2),
            pl.BlockSpec(wt_main.shape, const2),
            pl.BlockSpec(wt_idx_hi.shape, const2),
            pl.BlockSpec(wt_idx_lo.shape, const2),
            pl.BlockSpec((1, ROT_HALF, tm), lambda b, i: (b, 0, i)),
            pl.BlockSpec((1, ROT_HALF, tm), lambda b, i: (b, 0, i)),
            pl.BlockSpec((1, d_sgu), const2),
            pl.BlockSpec(w_sp.shape, const3),
            pl.BlockSpec(b_sp.shape, const3),
            pl.BlockSpec((1, d_sgu), const2),
        ],
        out_specs=[
            pl.BlockSpec((1, tm, d_sgu), lambda b, i: (b, i, 0)),
            pl.BlockSpec((1, nq, tm), lambda b, i: (b, 0, i)),
            pl.BlockSpec((1, tm, nkv), lambda b, i: (b, i, 0)),
            pl.BlockSpec((1, tm // ATT_CH, nkv, ATT_CH), lambda b, i: (b, i, 0, 0)),
            pl.BlockSpec((1, IDX_HEADS, 2 * IDX_DIM, tm), lambda b, i: (b, 0, 0, i)),
            pl.BlockSpec((1, tm, IDX_K), lambda b, i: (b, i, 0)),
            pl.BlockSpec((1, IDX_HEADS, tm), lambda b, i: (b, 0, i)),
            pl.BlockSpec((1, N_KV_HEADS, tm), lambda b, i: (b, 0, i)),
        ],
        out_shape=[
            jax.ShapeDtypeStruct((bsz, s, d_sgu), BF16),
            jax.ShapeDtypeStruct((bsz, nq, s), BF16),
            jax.ShapeDtypeStruct((bsz, s, nkv), BF16),
            jax.ShapeDtypeStruct((bsz, s // ATT_CH, nkv, ATT_CH), BF16),
            jax.ShapeDtypeStruct((bsz, IDX_HEADS, 2 * IDX_DIM, s), BF16),
            jax.ShapeDtypeStruct((bsz, s, IDX_K), BF16),
            jax.ShapeDtypeStruct((bsz, IDX_HEADS, s), F32),
            jax.ShapeDtypeStruct((bsz, N_KV_HEADS, s), F32),
        ],
        compiler_params=pltpu.CompilerParams(
            dimension_semantics=("arbitrary", "arbitrary"), vmem_limit_bytes=VMEM_LIMIT_BYTES),
        name="in_proj",
    )(x, mod3, g_pre, w_uv, wt_main, wt_idx_hi, wt_idx_lo, cos_t, sin_t, g_v, w_sp, b_sp, g_osgu)


def _attn_kernel(qi_ref, w_ref, ki_ref, q_ref, k_ref, vt_ref, kn_ref, g_ref, o_ref,
                 sc_ref, p_ref, qi4_ref, qpad_ref, m_ref, acc_ref, *, topk):
    nb = qi_ref.shape[0]
    rows = range(nb)
    tq, ch = ATT_TQ, ATT_CH
    j = pl.program_id(1)
    nch = j + 1
    q_pos = j * tq + lax.broadcasted_iota(jnp.int32, (1, tq), 1)
    row_iota = lax.broadcasted_iota(jnp.int32, (ch, tq), 0)

    def fold8(a, op):
        return op(a.reshape(ch // 8, 8, tq), axis=0)

    def chunk_rows(kc):
        return pl.ds(pl.multiple_of(kc * ch, ch), ch)

    for bi in rows:
        for hh in range(IDX_HEADS):
            q_hi, q_lo = qi_ref[bi, hh, 0:IDX_DIM, :], qi_ref[bi, hh, IDX_DIM:, :]
            qi4_ref[bi, hh] = jnp.concatenate([q_hi, q_hi, q_lo, q_lo], axis=0)

    def score_chunk(kc, carry, diagonal):
        out = []
        for bi in rows:
            mn, mx, n_pos, n_nonneg = carry[bi]
            kic = ki_ref[bi, chunk_rows(kc), :]
            acc = jnp.zeros((ch, tq), F32)
            for hh in range(IDX_HEADS):
                dd = jnp.dot(kic, qi4_ref[bi, hh], preferred_element_type=F32)
                acc = acc + w_ref[bi, hh:hh + 1, :] * jnp.maximum(dd, 0.0)
            if diagonal:
                causal = (row_iota + kc * ch) <= q_pos
                sc = jnp.where(causal, acc, -jnp.inf)
                mn = jnp.minimum(mn, fold8(jnp.where(causal, acc, jnp.inf), jnp.min))
            else:
                sc = acc
                mn = jnp.minimum(mn, fold8(acc, jnp.min))
            sc_ref[bi, kc] = sc
            mx = jnp.maximum(mx, fold8(sc, jnp.max))
            n_pos = n_pos + fold8(jnp.where(sc > 0.0, 1.0, 0.0), jnp.sum)
            n_nonneg = n_nonneg + fold8(jnp.where(sc >= 0.0, 1.0, 0.0), jnp.sum)
            out.append((mn, mx, n_pos, n_nonneg))
        return tuple(out)

    stats0 = (jnp.full((8, tq), jnp.inf, F32), jnp.full((8, tq), -jnp.inf, F32),
              jnp.zeros((8, tq), F32), jnp.zeros((8, tq), F32))
    full = functools.partial(score_chunk, diagonal=False)
    stats = lax.fori_loop(0, j // 2, lambda t, c: full(2 * t + 1, full(2 * t, c)), tuple(stats0 for _ in rows))
    stats = lax.cond(j % 2 == 1, lambda c: score_chunk(j, full(j - 1, c), diagonal=True),
                     lambda c: score_chunk(j, c, diagonal=True), stats)

    def count(preds):
        def body(kc, cs):
            return tuple(c + fold8(jnp.where(pred(sc_ref[bi, kc]), 1.0, 0.0), jnp.sum)
                         for bi, (c, pred) in enumerate(zip(cs, preds)))
        c8 = lax.fori_loop(0, nch, body, tuple(jnp.zeros((8, tq), F32) for _ in rows))
        return [jnp.sum(c, axis=0, keepdims=True) for c in c8]

    n_causal = (q_pos + 1).astype(F32)
    kk = jnp.minimum(float(topk), n_causal)

    def as_f(m):
        return jnp.where(m, 1.0, 0.0)

    def initial_state(stat):
        mn = jnp.min(stat[0], axis=0, keepdims=True)
        mx = jnp.max(stat[1], axis=0, keepdims=True)
        c0_gt = jnp.sum(stat[2], axis=0, keepdims=True)
        c0_ge = jnp.sum(stat[3], axis=0, keepdims=True)
        at_zero = (c0_gt < kk) & (c0_ge >= kk)
        take_all = n_causal == kk
        pos = c0_gt >= kk
        lo0 = jnp.where(pos, jnp.maximum(mn, 0.0), mn)
        clo0 = jnp.where(pos & (mn <= 0.0), c0_ge, n_causal)
        hi0 = jnp.where(pos, jnp.minimum(mx * JUST_ABOVE + TINY, F32_MAX), 0.0)
        thr0 = jnp.where(take_all, mn, 0.0)
        tied0 = jnp.logical_not(take_all) & at_zero & (c0_ge > kk)
        done0 = at_zero | take_all
        return lo0, clo0, hi0, thr0, as_f(done0), as_f(tied0)

    def bisect_step(states):
        mids = [0.5 * st[0] + 0.5 * st[2] for st in states]
        counts = count([lambda s, mid=mid: s >= mid for mid in mids])
        out = []
        for (lo, clo, hi, thr, done, tied), mid, c in zip(states, mids, counts):
            valid = (mid > lo) & (mid < hi)
            move_lo = valid & (c >= kk)
            thr = jnp.where(done, thr, jnp.where(valid, mid, lo))
            tied = tied | (jnp.logical_not(done) & jnp.logical_not(valid) & (clo > kk))
            done = done | jnp.logical_not(valid) | (c == kk)
            hi = jnp.where(valid & (c < kk), mid, hi)
            lo = jnp.where(move_lo, mid, lo)
            clo = jnp.where(move_lo, c, clo)
            out.append((lo, clo, hi, thr, done, tied))
        return out

    def bisect_round(states, steps):
        sts = [(lo, clo, hi, thr, done_f > 0.5, tied_f > 0.5) for lo, clo, hi, thr, done_f, tied_f in states]
        for _ in range(steps):
            sts = bisect_step(sts)
        return tuple((lo, clo, hi, thr, as_f(done), as_f(tied)) for lo, clo, hi, thr, done, tied in sts)

    def bisect_body(carry):
        states = bisect_round(carry[0], BISECT_CHECKED_STEPS)
        return states, jnp.max(1.0 - functools.reduce(jnp.minimum, [st[4] for st in states]))

    blind = jnp.where((j + 1) * tq <= topk, 0, BISECT_BLIND_ROUNDS)
    states = lax.fori_loop(0, blind, lambda _, sts: bisect_round(sts, BISECT_UNROLL),
                           tuple(initial_state(stat) for stat in stats))
    states, _ = lax.while_loop(
        lambda carry: carry[1] > 0.0, bisect_body, (states, jnp.where(blind > 0, 1.0, 0.0)))
    thrs = [st[3] for st in states]
    tied_f = functools.reduce(jnp.maximum, [st[5] for st in states])

    def logit_bound_sq(bi):
        q_sq = jnp.sum(jnp.square(q_ref[bi].astype(F32)).reshape(N_HEADS, HEAD_DIM, tq), axis=1)
        q_sq = jnp.max(jnp.max(q_sq, axis=0, keepdims=True), axis=1, keepdims=True)
        k_sq = jnp.max(jnp.max(kn_ref[bi], axis=0, keepdims=True), axis=1, keepdims=True)
        return q_sq * k_sq
    worst = functools.reduce(jnp.maximum, [logit_bound_sq(bi) for bi in rows])
    bounded_f = jnp.where(worst <= LOGIT_BOUND * LOGIT_BOUND, 2.0, 0.0)

    code = jnp.max(tied_f + bounded_f)
    tie = jnp.logical_or(code == 1.0, code == 3.0)
    bounded = code >= 2.0

    @pl.when(jnp.logical_not(tie))
    def _():
        def body(kc, _):
            for bi in rows:
                sc_ref[bi, kc] = jnp.where(sc_ref[bi, kc] >= thrs[bi], 0.0, NEG)
            return 0
        lax.fori_loop(0, nch, body, 0)

    @pl.when(tie)
    def _():
        r = lax.broadcasted_iota(jnp.int32, (ch, ch), 0)
        c = lax.broadcasted_iota(jnp.int32, (ch, ch), 1)
        below = jnp.where(c < r, 1.0, 0.0).astype(BF16)
        n_gt = count([lambda s, thr=thr: s > thr for thr in thrs])
        needs = [kk - n for n in n_gt]

        def body(kc, seen):
            out = []
            for bi in rows:
                s = sc_ref[bi, kc]
                eq = jnp.where(s == thrs[bi], 1.0, 0.0)
                rank = jnp.dot(below, eq.astype(BF16), preferred_element_type=F32) + seen[bi]
                sel = (s > thrs[bi]) | ((s == thrs[bi]) & (rank < needs[bi]))
                sc_ref[bi, kc] = jnp.where(sel, 0.0, NEG)
                out.append(seen[bi] + jnp.sum(eq, axis=0, keepdims=True))
            return tuple(out)
        lax.fori_loop(0, nch, body, tuple(jnp.zeros((1, tq), F32) for _ in rows))

    zq = jnp.zeros((HEAD_DIM, tq), BF16)
    for bi in rows:
        for hh in range(N_HEADS):
            qh = q_ref[bi, hh * HEAD_DIM:(hh + 1) * HEAD_DIM, :]
            qpad_ref[bi, hh] = jnp.concatenate([qh, zq] if hh < GQA_GROUP else [zq, qh], axis=0)
    acc_ref[...] = jnp.zeros(acc_ref.shape, F32)
    m_ref[...] = jnp.zeros(m_ref.shape, F32)
    ones = jnp.ones((ONES_ROWS, ch), BF16)

    def logits(bi, kc, hh, kcs, bias):
        return jnp.dot(kcs, qpad_ref[bi, hh], preferred_element_type=F32) + bias

    @pl.when(jnp.logical_not(bounded))
    def _():
        def body(kc, ms):
            out = []
            for bi in rows:
                kcs, bias = k_ref[bi, chunk_rows(kc), :], sc_ref[bi, kc]
                out.append(tuple(jnp.maximum(ms[bi][hh], fold8(logits(bi, kc, hh, kcs, bias), jnp.max))
                                 for hh in range(N_HEADS)))
            return tuple(out)
        ms = lax.fori_loop(0, nch, body,
                           tuple(tuple(jnp.full((8, tq), NEG, F32) for _ in range(N_HEADS)) for _ in rows))
        for bi in rows:
            for hh in range(N_HEADS):
                m_ref[bi, hh] = jnp.max(ms[bi][hh], axis=0, keepdims=True)

    def prob_chunk(kc):
        for bi in rows:
            kcs, bias = k_ref[bi, chunk_rows(kc), :], sc_ref[bi, kc]
            for hh in range(N_HEADS):
                p_ref[bi, hh, kc] = jnp.exp2(logits(bi, kc, hh, kcs, bias) - m_ref[bi, hh]).astype(BF16)

    def pv_chunk(kc):
        for bi in rows:
            vt = vt_ref[bi, kc]
            for g in range(N_KV_HEADS):
                vaug = jnp.concatenate([vt[g * HEAD_DIM:(g + 1) * HEAD_DIM], ones], axis=0)
                for hh in range(g * GQA_GROUP, (g + 1) * GQA_GROUP):
                    acc_ref[bi, hh] += jnp.dot(vaug, p_ref[bi, hh, kc], preferred_element_type=F32)

    npair = nch // 2

    @pl.when(npair > 0)
    def _():
        prob_chunk(0)
        prob_chunk(1)

        def attend(t, _):
            pv_chunk(2 * t - 2)
            pv_chunk(2 * t - 1)
            prob_chunk(2 * t)
            prob_chunk(2 * t + 1)
            return 0
        lax.fori_loop(1, npair, attend, 0)
        pv_chunk(2 * npair - 2)
        pv_chunk(2 * npair - 1)

    @pl.when(nch % 2 == 1)
    def _():
        prob_chunk(nch - 1)
        pv_chunk(nch - 1)

    for bi in rows:
        outs = []
        for hh in range(N_HEADS):
            a = acc_ref[bi, hh]
            outs.append(a[0:HEAD_DIM] * (1.0 / a[HEAD_DIM:HEAD_DIM + 1]))
        y = jnp.concatenate(outs, axis=0)
        yn = y * lax.rsqrt(jnp.mean(y * y, axis=0, keepdims=True) + EPS) * g_ref[...]
        o_ref[bi] = yn.T.astype(BF16)


def _dsa_attn(qi3t, wt, ki3, qt, kk, vt, kn, g_attn_b, topk):
    bsz, s, _ = ki3.shape
    tq, ch = ATT_TQ, ATT_CH
    nb = ATT_NB if bsz % ATT_NB == 0 else 1
    nq = N_HEADS * HEAD_DIM
    nkv = N_KV_HEADS * HEAD_DIM
    return pl.pallas_call(
        functools.partial(_attn_kernel, topk=topk),
        grid=(bsz // nb, s // tq),
        in_specs=[
            pl.BlockSpec((nb, IDX_HEADS, 2 * IDX_DIM, tq), lambda b, j: (b, 0, 0, j)),
            pl.BlockSpec((nb, IDX_HEADS, tq), lambda b, j: (b, 0, j)),
            pl.BlockSpec((nb, s, IDX_K), lambda b, j: (b, 0, 0)),
            pl.BlockSpec((nb, nq, tq), lambda b, j: (b, 0, j)),
            pl.BlockSpec((nb, s, nkv), lambda b, j: (b, 0, 0)),
            pl.BlockSpec((nb, s // ch, nkv, ch), lambda b, j: (b, 0, 0, 0)),
            pl.BlockSpec((nb, N_KV_HEADS, s), lambda b, j: (b, 0, 0)),
            pl.BlockSpec((nq, tq), lambda b, j: (0, 0)),
        ],
        out_specs=pl.BlockSpec((nb, tq, nq), lambda b, j: (b, j, 0)),
        out_shape=jax.ShapeDtypeStruct((bsz, s, nq), BF16),
        scratch_shapes=[
            pltpu.VMEM((nb, s // ch, ch, tq), F32),
            pltpu.VMEM((nb, N_HEADS, s // ch, ch, tq), BF16),
            pltpu.VMEM((nb, IDX_HEADS, IDX_K, tq), BF16),
            pltpu.VMEM((nb, N_HEADS, 2 * HEAD_DIM, tq), BF16),
            pltpu.VMEM((nb, N_HEADS, 1, tq), F32),
            pltpu.VMEM((nb, N_HEADS, HEAD_DIM + ONES_ROWS, tq), F32),
        ],
        compiler_params=pltpu.CompilerParams(
            dimension_semantics=("arbitrary", "arbitrary"), vmem_limit_bytes=VMEM_LIMIT_BYTES),
        name="dsa_attn",
    )(qi3t, wt, ki3, qt, kk, vt, kn, g_attn_b)


def _out_ffn_kernel(x_ref, ya_ref, yb_ref, mod_ref, wo_ref, gpm_ref, gpf_ref, w1_ref, w2_ref, gpo_ref, o_ref):
    d_a = ya_ref.shape[2]
    tm = x_ref.shape[1]
    g1 = mod_ref[0, 2:3, :]
    sh2 = mod_ref[0, 3:4, :]
    sc2 = mod_ref[0, 4:5, :]
    g2 = mod_ref[0, 5:6, :]

    def mix_residual(rows):
        o = jnp.dot(ya_ref[0, rows, :], wo_ref[0:d_a, :], preferred_element_type=F32)
        o += jnp.dot(yb_ref[0, rows, :], wo_ref[d_a:, :], preferred_element_type=F32)
        return x_ref[0, rows, :] + g1 * _rms(o, gpm_ref[...])

    def mlp(x1):
        h2 = (_rms(x1, gpf_ref[...]) * (1.0 + sc2) + sh2).astype(BF16)
        f = jnp.zeros(x1.shape, F32)
        for c in range(w1_ref.shape[1] // FFN_TF):
            a = jnp.dot(h2, w1_ref[:, c * FFN_TF:(c + 1) * FFN_TF], preferred_element_type=F32)
            a = jnp.square(jnp.maximum(a, 0.0)).astype(BF16)
            f += jnp.dot(a, w2_ref[c * FFN_TF:(c + 1) * FFN_TF, :], preferred_element_type=F32)
        return f

    halves = [pl.ds(i * (tm // 2), tm // 2) for i in range(2)]
    x1 = [mix_residual(rows) for rows in halves]
    f = [mlp(x1h) for x1h in x1]
    for rows, x1h, fh in zip(halves, x1, f):
        o_ref[0, rows, :] = x1h + g2 * _rms(fh, gpo_ref[...])


def _out_ffn(x, ya, yb, mod3, w_out, g_pm, g_pf, w1, w2, g_po):
    bsz, s, d = x.shape
    tm = FFN_TM
    d_a = ya.shape[2]
    const2 = lambda b, i: (0, 0)
    resident = functools.partial(pl.BlockSpec, index_map=const2, pipeline_mode=pl.Buffered(1))
    return pl.pallas_call(
        _out_ffn_kernel,
        grid=(bsz, s // tm),
        in_specs=[
            pl.BlockSpec((1, tm, d), lambda b, i: (b, i, 0)),
            pl.BlockSpec((1, tm, d_a), lambda b, i: (b, i, 0)),
            pl.BlockSpec((1, tm, yb.shape[2]), lambda b, i: (b, i, 0)),
            pl.BlockSpec((1, 6, d), lambda b, i: (b, 0, 0)),
            resident(w_out.shape),
            pl.BlockSpec((1, d), const2),
            pl.BlockSpec((1, d), const2),
            resident(w1.shape),
            resident(w2.shape),
            pl.BlockSpec((1, d), const2),
        ],
        out_specs=pl.BlockSpec((1, tm, d), lambda b, i: (b, i, 0)),
        out_shape=jax.ShapeDtypeStruct((bsz, s, d), F32),
        compiler_params=pltpu.CompilerParams(
            dimension_semantics=("arbitrary", "arbitrary"), vmem_limit_bytes=VMEM_LIMIT_BYTES),
        name="out_ffn",
    )(x, ya, yb, mod3, w_out, g_pm, g_pf, w1, w2, g_po)


def _layer(x, mod3, cos_t, sin_t, g_pre_mix, w_in, g_sgu_v, w_spatial, b_spatial, g_out_sgu, g_out_attn,
           w_out, g_post_mix, g_pre_ffn, w_ff1, w_ff2, g_post_ffn):
    bsz, s, d = x.shape
    d_sgu = SGU_GROUPS * SGU_GROUP_DIM
    nq = N_HEADS * HEAD_DIM
    nkv = N_KV_HEADS * HEAD_DIM
    nqi = IDX_HEADS * IDX_DIM
    topk = min(TOPK_MAX, s // 4)

    o_q = 2 * d_sgu
    o_qi = o_q + nq + 2 * nkv
    w_uv = w_in[:, :o_q].astype(BF16)
    wt_main = w_in[:, o_q:o_qi].T.astype(BF16)
    n_idx = nqi + IDX_DIM + IDX_HEADS
    wt_idx = jnp.pad(w_in[:, o_qi:o_qi + n_idx].T, ((0, (-n_idx) % 16), (0, 0)))
    wt_idx_hi = wt_idx.astype(BF16)
    wt_idx_lo = (wt_idx - wt_idx_hi.astype(F32)).astype(BF16)
    b_sp = jnp.broadcast_to(b_spatial[:, :, None], (SGU_GROUPS, CHUNK, SGU_GROUP_DIM))

    ya, qt, kk, vt, qi3t, ki3, wt, kn = _in_proj(
        x, mod3, g_pre_mix.reshape(1, d), w_uv, wt_main, wt_idx_hi, wt_idx_lo, cos_t, sin_t,
        g_sgu_v.reshape(1, d_sgu), w_spatial, b_sp, g_out_sgu.reshape(1, d_sgu))

    g_attn_b = jnp.broadcast_to(g_out_attn[:, None], (nq, ATT_TQ))
    yb = _dsa_attn(qi3t, wt, ki3, qt, kk, vt, kn, g_attn_b, topk)

    return _out_ffn(x, ya, yb, mod3, w_out.astype(BF16), g_post_mix.reshape(1, d), g_pre_ffn.reshape(1, d),
                    w_ff1.astype(BF16), w_ff2.astype(BF16), g_post_ffn.reshape(1, d))


def kernel(x, c, positions, w_ada, b_ada, g_pre_mix, w_in, g_sgu_v, w_spatial, b_spatial, g_out_sgu, g_out_attn,
           w_out, g_post_mix, g_pre_ffn, w_ff1, w_ff2, g_post_ffn):
    bsz, s, d = x.shape
    inv_freq = ROPE_THETA ** (-jnp.arange(0, ROT_DIM, 2, dtype=F32) / ROT_DIM)
    ang = positions.astype(F32)[:, None, :] * inv_freq[None, :, None]
    cos_t, sin_t = jnp.cos(ang), jnp.sin(ang)
    for l in range(w_ada.shape[0]):
        mod3 = _ada_mod(c, w_ada[l], b_ada[l]).reshape(bsz, 6, d)
        x = _layer(x, mod3, cos_t, sin_t, g_pre_mix[l], w_in[l], g_sgu_v[l], w_spatial[l], b_spatial[l],
                   g_out_sgu[l], g_out_attn[l], w_out[l], g_post_mix[l], g_pre_ffn[l], w_ff1[l], w_ff2[l],
                   g_post_ffn[l])
    return x
```

```python
import functools

import jax
import jax.numpy as jnp
from jax import lax
from jax.experimental import pallas as pl
from jax.experimental.pallas import tpu as pltpu

F32 = jnp.float32
BF16 = jnp.bfloat16

SGU_GROUPS = 4
SGU_GROUP_DIM = 128
CHUNK = 128
N_HEADS = 8
HEAD_DIM = 64
N_KV_HEADS = 2
GQA_GROUP = N_HEADS // N_KV_HEADS
ROT_DIM = 16
ROT_HALF = ROT_DIM // 2
ROPE_THETA = 500000.0
IDX_HEADS = 8
IDX_DIM = 64
TOPK_MAX = 256
EPS = 1e-6

ADA_TN = 1024
PROJ_TM = 1024
PROJ_SUB = 256
ATT_TQ = 256
ATT_CH = 256
ATT_NB = 2
FFN_TM = 512
FFN_TF = 1024
BISECT_UNROLL = 4
BISECT_BLIND_ROUNDS = 5
BISECT_CHECKED_STEPS = 2
IDX_K = 4 * IDX_DIM
ONES_ROWS = 16
NEG = -1e30
TINY = 1e-30
JUST_ABOVE = 1.0 + 2.0 ** -20
F32_MAX = 3.4028235e38
LOGIT_BOUND = 96.0
VMEM_LIMIT_BYTES = 56 * 1024 * 1024
Q_SCALE = (HEAD_DIM ** -0.5) * 1.4426950408889634


def _split_bf16(a):
    hi = a.astype(BF16).astype(F32)
    lo = (a - hi).astype(BF16).astype(F32)
    return hi, lo


def _rms(a, g):
    return a * lax.rsqrt(jnp.mean(a * a, axis=-1, keepdims=True) + EPS) * g


def _ada_kernel(c_ref, w_ref, b_ref, o_ref):
    c = c_ref[...]
    ca = c * (1.0 / (1.0 + jnp.exp(-c)))
    c_hi, c_lo = _split_bf16(ca)
    w_hi, w_lo = _split_bf16(w_ref[...])
    c_hi, c_lo, w_hi, w_lo = (t.astype(BF16) for t in (c_hi, c_lo, w_hi, w_lo))
    acc = jnp.dot(c_hi, w_hi, preferred_element_type=F32)
    acc += jnp.dot(c_hi, w_lo, preferred_element_type=F32)
    acc += jnp.dot(c_lo, w_hi, preferred_element_type=F32)
    o_ref[...] = acc + b_ref[...]


def _ada_mod(c, w_ada, b_ada):
    bsz, d = c.shape
    n = w_ada.shape[1]
    return pl.pallas_call(
        _ada_kernel,
        grid=(n // ADA_TN,),
        in_specs=[
            pl.BlockSpec((bsz, d), lambda i: (0, 0)),
            pl.BlockSpec((d, ADA_TN), lambda i: (0, i)),
            pl.BlockSpec((1, ADA_TN), lambda i: (0, i)),
        ],
        out_specs=pl.BlockSpec((bsz, ADA_TN), lambda i: (0, i)),
        out_shape=jax.ShapeDtypeStruct((bsz, n), F32),
        compiler_params=pltpu.CompilerParams(dimension_semantics=("arbitrary",)),
        name="ada_mod",
    )(c, w_ada, b_ada.reshape(1, n))


def _rope_rows(p, cos, sin):
    x1, x2 = p[0:ROT_HALF], p[ROT_HALF:ROT_DIM]
    return jnp.concatenate([x1 * cos - x2 * sin, x2 * cos + x1 * sin, p[ROT_DIM:]], axis=0)


def _dot_nt(a, b):
    return lax.dot_general(a, b, (((1,), (1,)), ((), ())), preferred_element_type=F32)


def _in_proj_kernel(x_ref, mod_ref, gpre_ref, wuv_ref, wtm_ref, wti_hi_ref, wti_lo_ref, cos_ref, sin_ref,
                    gv_ref, wsp_ref, bsp_ref, gosgu_ref,
                    ya_ref, qt_ref, k_ref, vt_ref, qi_ref, ki_ref, wt_ref, kn_ref):
    d_sgu = SGU_GROUPS * SGU_GROUP_DIM
    nq = N_HEADS * HEAD_DIM
    nkv = N_KV_HEADS * HEAD_DIM
    nqi = IDX_HEADS * IDX_DIM
    sh1 = mod_ref[0, 0:1, :]
    sc1 = mod_ref[0, 1:2, :]
    row = lax.broadcasted_iota(jnp.int32, (CHUNK, CHUNK), 0)
    col = lax.broadcasted_iota(jnp.int32, (CHUNK, CHUNK), 1)
    w_mix = [jnp.where(col <= row, wsp_ref[g], 0.0).astype(BF16) for g in range(SGU_GROUPS)]

    def project(t0, tn):
        tok = slice(t0, t0 + tn)
        h = _rms(x_ref[0, tok, :], gpre_ref[...]) * (1.0 + sc1) + sh1
        h_hi32, h_lo32 = _split_bf16(h)
        h_hi, h_lo = h_hi32.astype(BF16), h_lo32.astype(BF16)
        cos, sin = cos_ref[0, :, tok], sin_ref[0, :, tok]

        puv = jnp.dot(h_hi, wuv_ref[...], preferred_element_type=F32)
        pti = _dot_nt(wti_hi_ref[...], h_hi) + _dot_nt(wti_lo_ref[...], h_hi) + _dot_nt(wti_hi_ref[...], h_lo)
        ptm = _dot_nt(wtm_ref[...], h_hi)

        zu = jax.nn.gelu(puv[:, :d_sgu])
        zv = jax.nn.gelu(puv[:, d_sgu:])
        ya_groups = []
        for g in range(SGU_GROUPS):
            sl = slice(g * SGU_GROUP_DIM, (g + 1) * SGU_GROUP_DIM)
            vn = _rms(zv[:, sl], gv_ref[:, sl]).astype(BF16)
            parts = []
            for ci in range(tn // CHUNK):
                rs = slice(ci * CHUNK, (ci + 1) * CHUNK)
                mixed = jnp.dot(w_mix[g], vn[rs], preferred_element_type=F32) + bsp_ref[g]
                parts.append(zu[rs, sl] * mixed)
            ya_groups.append(jnp.concatenate(parts, axis=0))
        ya = jnp.concatenate(ya_groups, axis=1)
        ya_ref[0, tok, :] = _rms(ya, gosgu_ref[...]).astype(BF16)

        for hh in range(N_HEADS):
            rs = slice(hh * HEAD_DIM, (hh + 1) * HEAD_DIM)
            qt_ref[0, rs, tok] = (_rope_rows(ptm[rs], cos, sin) * Q_SCALE).astype(BF16)
        kt = jnp.concatenate(
            [_rope_rows(ptm[nq + g * HEAD_DIM:nq + (g + 1) * HEAD_DIM], cos, sin) for g in range(N_KV_HEADS)],
            axis=0)
        k_ref[0, tok, :] = kt.T.astype(BF16)
        kb = kt.astype(BF16).astype(F32)
        kn_ref[0, :, tok] = jnp.concatenate(
            [jnp.sum(jnp.square(kb[g * HEAD_DIM:(g + 1) * HEAD_DIM]), axis=0, keepdims=True)
             for g in range(N_KV_HEADS)], axis=0)
        vt = ptm[nq + nkv:nq + 2 * nkv].astype(BF16)
        for ci in range(tn // ATT_CH):
            vt_ref[0, t0 // ATT_CH + ci] = vt[:, ci * ATT_CH:(ci + 1) * ATT_CH]

        for hh in range(IDX_HEADS):
            r = _rope_rows(pti[hh * IDX_DIM:(hh + 1) * IDX_DIM], cos, sin)
            r_hi, r_lo = _split_bf16(r)
            qi_ref[0, hh, :, tok] = jnp.concatenate([r_hi, r_lo], axis=0).astype(BF16)
        kr = _rope_rows(pti[nqi:nqi + IDX_DIM], cos, sin)
        k_hi, k_lo = _split_bf16(kr)
        ki_ref[0, tok, :] = jnp.concatenate([k_hi, k_lo, k_hi, k_lo], axis=0).T.astype(BF16)
        wt_ref[0, :, tok] = (pti[nqi + IDX_DIM:nqi + IDX_DIM + IDX_HEADS]
                             * ((IDX_HEADS ** -0.5) * (IDX_DIM ** -0.5)))

    for t0 in range(0, x_ref.shape[1], PROJ_SUB):
        project(t0, PROJ_SUB)


def _in_proj(x, mod3, g_pre, w_uv, wt_main, wt_idx_hi, wt_idx_lo, cos_t, sin_t, g_v, w_sp, b_sp, g_osgu):
    bsz, s, d = x.shape
    tm = PROJ_TM
    d_sgu = SGU_GROUPS * SGU_GROUP_DIM
    nq = N_HEADS * HEAD_DIM
    nkv = N_KV_HEADS * HEAD_DIM
    const2 = lambda b, i: (0, 0)
    const3 = lambda b, i: (0, 0, 0)
    return pl.pallas_call(
        _in_proj_kernel,
        grid=(bsz, s // tm),
        in_specs=[
            pl.BlockSpec((1, tm, d), lambda b, i: (b, i, 0)),
            pl.BlockSpec((1, 6, d), lambda b, i: (b, 0, 0)),
            pl.BlockSpec((1, d), const2),
            pl.BlockSpec(w_uv.shape, const2),
            pl.BlockSpec(wt_main.shape, const2),
            pl.BlockSpec(wt_idx_hi.shape, const2),
            pl.BlockSpec(wt_idx_lo.shape, const2),
            pl.BlockSpec((1, ROT_HALF, tm), lambda b, i: (b, 0, i)),
            pl.BlockSpec((1, ROT_HALF, tm), lambda b, i: (b, 0, i)),
            pl.BlockSpec((1, d_sgu), const2),
            pl.BlockSpec(w_sp.shape, const3),
            pl.BlockSpec(b_sp.shape, const3),
            pl.BlockSpec((1, d_sgu), const2),
        ],
        out_specs=[
            pl.BlockSpec((1, tm, d_sgu), lambda b, i: (b, i, 0)),
            pl.BlockSpec((1, nq, tm), lambda b, i: (b, 0, i)),
            pl.BlockSpec((1, tm, nkv), lambda b, i: (b, i, 0)),
            pl.BlockSpec((1, tm // ATT_CH, nkv, ATT_CH), lambda b, i: (b, i, 0, 0)),
            pl.BlockSpec((1, IDX_HEADS, 2 * IDX_DIM, tm), lambda b, i: (b, 0, 0, i)),
            pl.BlockSpec((1, tm, IDX_K), lambda b, i: (b, i, 0)),
            pl.BlockSpec((1, IDX_HEADS, tm), lambda b, i: (b, 0, i)),
            pl.BlockSpec((1, N_KV_HEADS, tm), lambda b, i: (b, 0, i)),
        ],
        out_shape=[
            jax.ShapeDtypeStruct((bsz, s, d_sgu), BF16),
            jax.ShapeDtypeStruct((bsz, nq, s), BF16),
            jax.ShapeDtypeStruct((bsz, s, nkv), BF16),
            jax.ShapeDtypeStruct((bsz, s // ATT_CH, nkv, ATT_CH), BF16),
            jax.ShapeDtypeStruct((bsz, IDX_HEADS, 2 * IDX_DIM, s), BF16),
            jax.ShapeDtypeStruct((bsz, s, IDX_K), BF16),
            jax.ShapeDtypeStruct((bsz, IDX_HEADS, s), F32),
            jax.ShapeDtypeStruct((bsz, N_KV_HEADS, s), F32),
        ],
        compiler_params=pltpu.CompilerParams(
            dimension_semantics=("arbitrary", "arbitrary"), vmem_limit_bytes=VMEM_LIMIT_BYTES),
        name="in_proj",
    )(x, mod3, g_pre, w_uv, wt_main, wt_idx_hi, wt_idx_lo, cos_t, sin_t, g_v, w_sp, b_sp, g_osgu)


def _attn_kernel(qi_ref, w_ref, ki_ref, q_ref, k_ref, vt_ref, kn_ref, g_ref, o_ref,
                 sc_ref, p_ref, qi4_ref, qpad_ref, m_ref, acc_ref, *, topk):
    nb = qi_ref.shape[0]
    rows = range(nb)
    tq, ch = ATT_TQ, ATT_CH
    j = pl.program_id(1)
    nch = j + 1
    q_pos = j * tq + lax.broadcasted_iota(jnp.int32, (1, tq), 1)
    row_iota = lax.broadcasted_iota(jnp.int32, (ch, tq), 0)

    def fold8(a, op):
        return op(a.reshape(ch // 8, 8, tq), axis=0)

    def chunk_rows(kc):
        return pl.ds(pl.multiple_of(kc * ch, ch), ch)

    for bi in rows:
        for hh in range(IDX_HEADS):
            q_hi, q_lo = qi_ref[bi, hh, 0:IDX_DIM, :], qi_ref[bi, hh, IDX_DIM:, :]
            qi4_ref[bi, hh] = jnp.concatenate([q_hi, q_hi, q_lo, q_lo], axis=0)

    def score_chunk(kc, carry, diagonal):
        out = []
        for bi in rows:
            mn, mx, n_pos, n_nonneg = carry[bi]
            kic = ki_ref[bi, chunk_rows(kc), :]
            acc = jnp.zeros((ch, tq), F32)
            for hh in range(IDX_HEADS):
                dd = jnp.dot(kic, qi4_ref[bi, hh], preferred_element_type=F32)
                acc = acc + w_ref[bi, hh:hh + 1, :] * jnp.maximum(dd, 0.0)
            if diagonal:
                causal = (row_iota + kc * ch) <= q_pos
                sc = jnp.where(causal, acc, -jnp.inf)
                mn = jnp.minimum(mn, fold8(jnp.where(causal, acc, jnp.inf), jnp.min))
            else:
                sc = acc
                mn = jnp.minimum(mn, fold8(acc, jnp.min))
            sc_ref[bi, kc] = sc
            mx = jnp.maximum(mx, fold8(sc, jnp.max))
            n_pos = n_pos + fold8(jnp.where(sc > 0.0, 1.0, 0.0), jnp.sum)
            n_nonneg = n_nonneg + fold8(jnp.where(sc >= 0.0, 1.0, 0.0), jnp.sum)
            out.append((mn, mx, n_pos, n_nonneg))
        return tuple(out)

    stats0 = (jnp.full((8, tq), jnp.inf, F32), jnp.full((8, tq), -jnp.inf, F32),
              jnp.zeros((8, tq), F32), jnp.zeros((8, tq), F32))
    full = functools.partial(score_chunk, diagonal=False)
    stats = lax.fori_loop(0, j // 2, lambda t, c: full(2 * t + 1, full(2 * t, c)), tuple(stats0 for _ in rows))
    stats = lax.cond(j % 2 == 1, lambda c: score_chunk(j, full(j - 1, c), diagonal=True),
                     lambda c: score_chunk(j, c, diagonal=True), stats)

    def count(preds):
        def body(kc, cs):
            return tuple(c + fold8(jnp.where(pred(sc_ref[bi, kc]), 1.0, 0.0), jnp.sum)
                         for bi, (c, pred) in enumerate(zip(cs, preds)))
        c8 = lax.fori_loop(0, nch, body, tuple(jnp.zeros((8, tq), F32) for _ in rows))
        return [jnp.sum(c, axis=0, keepdims=True) for c in c8]

    n_causal = (q_pos + 1).astype(F32)
    kk = jnp.minimum(float(topk), n_causal)

    def as_f(m):
        return jnp.where(m, 1.0, 0.0)

    def initial_state(stat):
        mn = jnp.min(stat[0], axis=0, keepdims=True)
        mx = jnp.max(stat[1], axis=0, keepdims=True)
        c0_gt = jnp.sum(stat[2], axis=0, keepdims=True)
        c0_ge = jnp.sum(stat[3], axis=0, keepdims=True)
        at_zero = (c0_gt < kk) & (c0_ge >= kk)
        take_all = n_causal == kk
        pos = c0_gt >= kk
        lo0 = jnp.where(pos, jnp.maximum(mn, 0.0), mn)
        clo0 = jnp.where(pos & (mn <= 0.0), c0_ge, n_causal)
        hi0 = jnp.where(pos, jnp.minimum(mx * JUST_ABOVE + TINY, F32_MAX), 0.0)
        thr0 = jnp.where(take_all, mn, 0.0)
        tied0 = jnp.logical_not(take_all) & at_zero & (c0_ge > kk)
        done0 = at_zero | take_all
        return lo0, clo0, hi0, thr0, as_f(done0), as_f(tied0)

    def bisect_step(states):
        mids = [0.5 * st[0] + 0.5 * st[2] for st in states]
        counts = count([lambda s, mid=mid: s >= mid for mid in mids])
        out = []
        for (lo, clo, hi, thr, done, tied), mid, c in zip(states, mids, counts):
            valid = (mid > lo) & (mid < hi)
            move_lo = valid & (c >= kk)
            thr = jnp.where(done, thr, jnp.where(valid, mid, lo))
            tied = tied | (jnp.logical_not(done) & jnp.logical_not(valid) & (clo > kk))
            done = done | jnp.logical_not(valid) | (c == kk)
            hi = jnp.where(valid & (c < kk), mid, hi)
            lo = jnp.where(move_lo, mid, lo)
            clo = jnp.where(move_lo, c, clo)
            out.append((lo, clo, hi, thr, done, tied))
        return out

    def bisect_round(states, steps):
        sts = [(lo, clo, hi, thr, done_f > 0.5, tied_f > 0.5) for lo, clo, hi, thr, done_f, tied_f in states]
        for _ in range(steps):
            sts = bisect_step(sts)
        return tuple((lo, clo, hi, thr, as_f(done), as_f(tied)) for lo, clo, hi, thr, done, tied in sts)

    def bisect_body(carry):
        states = bisect_round(carry[0], BISECT_CHECKED_STEPS)
        return states, jnp.max(1.0 - functools.reduce(jnp.minimum, [st[4] for st in states]))

    blind = jnp.where((j + 1) * tq <= topk, 0, BISECT_BLIND_ROUNDS)
    states = lax.fori_loop(0, blind, lambda _, sts: bisect_round(sts, BISECT_UNROLL),
                           tuple(initial_state(stat) for stat in stats))
    states, _ = lax.while_loop(
        lambda carry: carry[1] > 0.0, bisect_body, (states, jnp.where(blind > 0, 1.0, 0.0)))
    thrs = [st[3] for st in states]
    tied_f = functools.reduce(jnp.maximum, [st[5] for st in states])

    def logit_bound_sq(bi):
        q_sq = jnp.sum(jnp.square(q_ref[bi].astype(F32)).reshape(N_HEADS, HEAD_DIM, tq), axis=1)
        q_sq = jnp.max(jnp.max(q_sq, axis=0, keepdims=True), axis=1, keepdims=True)
        k_sq = jnp.max(jnp.max(kn_ref[bi], axis=0, keepdims=True), axis=1, keepdims=True)
        return q_sq * k_sq
    worst = functools.reduce(jnp.maximum, [logit_bound_sq(bi) for bi in rows])
    bounded_f = jnp.where(worst <= LOGIT_BOUND * LOGIT_BOUND, 2.0, 0.0)

    code = jnp.max(tied_f + bounded_f)
    tie = jnp.logical_or(code == 1.0, code == 3.0)
    bounded = code >= 2.0

    @pl.when(jnp.logical_not(tie))
    def _():
        def body(kc, _):
            for bi in rows:
                sc_ref[bi, kc] = jnp.where(sc_ref[bi, kc] >= thrs[bi], 0.0, NEG)
            return 0
        lax.fori_loop(0, nch, body, 0)

    @pl.when(tie)
    def _():
        r = lax.broadcasted_iota(jnp.int32, (ch, ch), 0)
        c = lax.broadcasted_iota(jnp.int32, (ch, ch), 1)
        below = jnp.where(c < r, 1.0, 0.0).astype(BF16)
        n_gt = count([lambda s, thr=thr: s > thr for thr in thrs])
        needs = [kk - n for n in n_gt]

        def body(kc, seen):
            out = []
            for bi in rows:
                s = sc_ref[bi, kc]
                eq = jnp.where(s == thrs[bi], 1.0, 0.0)
                rank = jnp.dot(below, eq.astype(BF16), preferred_element_type=F32) + seen[bi]
                sel = (s > thrs[bi]) | ((s == thrs[bi]) & (rank < needs[bi]))
                sc_ref[bi, kc] = jnp.where(sel, 0.0, NEG)
                out.append(seen[bi] + jnp.sum(eq, axis=0, keepdims=True))
            return tuple(out)
        lax.fori_loop(0, nch, body, tuple(jnp.zeros((1, tq), F32) for _ in rows))

    zq = jnp.zeros((HEAD_DIM, tq), BF16)
    for bi in rows:
        for hh in range(N_HEADS):
            qh = q_ref[bi, hh * HEAD_DIM:(hh + 1) * HEAD_DIM, :]
            qpad_ref[bi, hh] = jnp.concatenate([qh, zq] if hh < GQA_GROUP else [zq, qh], axis=0)
    acc_ref[...] = jnp.zeros(acc_ref.shape, F32)
    m_ref[...] = jnp.zeros(m_ref.shape, F32)
    ones = jnp.ones((ONES_ROWS, ch), BF16)

    def logits(bi, kc, hh, kcs, bias):
        return jnp.dot(kcs, qpad_ref[bi, hh], preferred_element_type=F32) + bias

    @pl.when(jnp.logical_not(bounded))
    def _():
        def body(kc, ms):
            out = []
            for bi in rows:
                kcs, bias = k_ref[bi, chunk_rows(kc), :], sc_ref[bi, kc]
                out.append(tuple(jnp.maximum(ms[bi][hh], fold8(logits(bi, kc, hh, kcs, bias), jnp.max))
                                 for hh in range(N_HEADS)))
            return tuple(out)
        ms = lax.fori_loop(0, nch, body,
                           tuple(tuple(jnp.full((8, tq), NEG, F32) for _ in range(N_HEADS)) for _ in rows))
        for bi in rows:
            for hh in range(N_HEADS):
                m_ref[bi, hh] = jnp.max(ms[bi][hh], axis=0, keepdims=True)

    def prob_chunk(kc):
        for bi in rows:
            kcs, bias = k_ref[bi, chunk_rows(kc), :], sc_ref[bi, kc]
            for hh in range(N_HEADS):
                p_ref[bi, hh, kc] = jnp.exp2(logits(bi, kc, hh, kcs, bias) - m_ref[bi, hh]).astype(BF16)

    def pv_chunk(kc):
        for bi in rows:
            vt = vt_ref[bi, kc]
            for g in range(N_KV_HEADS):
                vaug = jnp.concatenate([vt[g * HEAD_DIM:(g + 1) * HEAD_DIM], ones], axis=0)
                for hh in range(g * GQA_GROUP, (g + 1) * GQA_GROUP):
                    acc_ref[bi, hh] += jnp.dot(vaug, p_ref[bi, hh, kc], preferred_element_type=F32)

    npair = nch // 2

    @pl.when(npair > 0)
    def _():
        prob_chunk(0)
        prob_chunk(1)

        def attend(t, _):
            pv_chunk(2 * t - 2)
            pv_chunk(2 * t - 1)
            prob_chunk(2 * t)
            prob_chunk(2 * t + 1)
            return 0
        lax.fori_loop(1, npair, attend, 0)
        pv_chunk(2 * npair - 2)
        pv_chunk(2 * npair - 1)

    @pl.when(nch % 2 == 1)
    def _():
        prob_chunk(nch - 1)
        pv_chunk(nch - 1)

    for bi in rows:
        outs = []
        for hh in range(N_HEADS):
            a = acc_ref[bi, hh]
            outs.append(a[0:HEAD_DIM] * (1.0 / a[HEAD_DIM:HEAD_DIM + 1]))
        y = jnp.concatenate(outs, axis=0)
        yn = y * lax.rsqrt(jnp.mean(y * y, axis=0, keepdims=True) + EPS) * g_ref[...]
        o_ref[bi] = yn.T.astype(BF16)


def _dsa_attn(qi3t, wt, ki3, qt, kk, vt, kn, g_attn_b, topk):
    bsz, s, _ = ki3.shape
    tq, ch = ATT_TQ, ATT_CH
    nb = ATT_NB if bsz % ATT_NB == 0 else 1
    nq = N_HEADS * HEAD_DIM
    nkv = N_KV_HEADS * HEAD_DIM
    return pl.pallas_call(
        functools.partial(_attn_kernel, topk=topk),
        grid=(bsz // nb, s // tq),
        in_specs=[
            pl.BlockSpec((nb, IDX_HEADS, 2 * IDX_DIM, tq), lambda b, j: (b, 0, 0, j)),
            pl.BlockSpec((nb, IDX_HEADS, tq), lambda b, j: (b, 0, j)),
            pl.BlockSpec((nb, s, IDX_K), lambda b, j: (b, 0, 0)),
            pl.BlockSpec((nb, nq, tq), lambda b, j: (b, 0, j)),
            pl.BlockSpec((nb, s, nkv), lambda b, j: (b, 0, 0)),
            pl.BlockSpec((nb, s // ch, nkv, ch), lambda b, j: (b, 0, 0, 0)),
            pl.BlockSpec((nb, N_KV_HEADS, s), lambda b, j: (b, 0, 0)),
            pl.BlockSpec((nq, tq), lambda b, j: (0, 0)),
        ],
        out_specs=pl.BlockSpec((nb, tq, nq), lambda b, j: (b, j, 0)),
        out_shape=jax.ShapeDtypeStruct((bsz, s, nq), BF16),
        scratch_shapes=[
            pltpu.VMEM((nb, s // ch, ch, tq), F32),
            pltpu.VMEM((nb, N_HEADS, s // ch, ch, tq), BF16),
            pltpu.VMEM((nb, IDX_HEADS, IDX_K, tq), BF16),
            pltpu.VMEM((nb, N_HEADS, 2 * HEAD_DIM, tq), BF16),
            pltpu.VMEM((nb, N_HEADS, 1, tq), F32),
            pltpu.VMEM((nb, N_HEADS, HEAD_DIM + ONES_ROWS, tq), F32),
        ],
        compiler_params=pltpu.CompilerParams(
            dimension_semantics=("arbitrary", "arbitrary"), vmem_limit_bytes=VMEM_LIMIT_BYTES),
        name="dsa_attn",
    )(qi3t, wt, ki3, qt, kk, vt, kn, g_attn_b)


def _out_ffn_kernel(x_ref, ya_ref, yb_ref, mod_ref, wo_ref, gpm_ref, gpf_ref, w1_ref, w2_ref, gpo_ref, o_ref):
    d_a = ya_ref.shape[2]
    tm = x_ref.shape[1]
    g1 = mod_ref[0, 2:3, :]
    sh2 = mod_ref[0, 3:4, :]
    sc2 = mod_ref[0, 4:5, :]
    g2 = mod_ref[0, 5:6, :]

    def mix_residual(rows):
        o = jnp.dot(ya_ref[0, rows, :], wo_ref[0:d_a, :], preferred_element_type=F32)
        o += jnp.dot(yb_ref[0, rows, :], wo_ref[d_a:, :], preferred_element_type=F32)
        return x_ref[0, rows, :] + g1 * _rms(o, gpm_ref[...])

    def mlp(x1):
        h2 = (_rms(x1, gpf_ref[...]) * (1.0 + sc2) + sh2).astype(BF16)
        f = jnp.zeros(x1.shape, F32)
        for c in range(w1_ref.shape[1] // FFN_TF):
            a = jnp.dot(h2, w1_ref[:, c * FFN_TF:(c + 1) * FFN_TF], preferred_element_type=F32)
            a = jnp.square(jnp.maximum(a, 0.0)).astype(BF16)
            f += jnp.dot(a, w2_ref[c * FFN_TF:(c + 1) * FFN_TF, :], preferred_element_type=F32)
        return f

    halves = [pl.ds(i * (tm // 2), tm // 2) for i in range(2)]
    x1 = [mix_residual(rows) for rows in halves]
    f = [mlp(x1h) for x1h in x1]
    for rows, x1h, fh in zip(halves, x1, f):
        o_ref[0, rows, :] = x1h + g2 * _rms(fh, gpo_ref[...])


def _out_ffn(x, ya, yb, mod3, w_out, g_pm, g_pf, w1, w2, g_po):
    bsz, s, d = x.shape
    tm = FFN_TM
    d_a = ya.shape[2]
    const2 = lambda b, i: (0, 0)
    resident = functools.partial(pl.BlockSpec, index_map=const2, pipeline_mode=pl.Buffered(1))
    return pl.pallas_call(
        _out_ffn_kernel,
        grid=(bsz, s // tm),
        in_specs=[
            pl.BlockSpec((1, tm, d), lambda b, i: (b, i, 0)),
            pl.BlockSpec((1, tm, d_a), lambda b, i: (b, i, 0)),
            pl.BlockSpec((1, tm, yb.shape[2]), lambda b, i: (b, i, 0)),
            pl.BlockSpec((1, 6, d), lambda b, i: (b, 0, 0)),
            resident(w_out.shape),
            pl.BlockSpec((1, d), const2),
            pl.BlockSpec((1, d), const2),
            resident(w1.shape),
            resident(w2.shape),
            pl.BlockSpec((1, d), const2),
        ],
        out_specs=pl.BlockSpec((1, tm, d), lambda b, i: (b, i, 0)),
        out_shape=jax.ShapeDtypeStruct((bsz, s, d), F32),
        compiler_params=pltpu.CompilerParams(
            dimension_semantics=("arbitrary", "arbitrary"), vmem_limit_bytes=VMEM_LIMIT_BYTES),
        name="out_ffn",
    )(x, ya, yb, mod3, w_out, g_pm, g_pf, w1, w2, g_po)


def _layer(x, mod3, cos_t, sin_t, g_pre_mix, w_in, g_sgu_v, w_spatial, b_spatial, g_out_sgu, g_out_attn,
           w_out, g_post_mix, g_pre_ffn, w_ff1, w_ff2, g_post_ffn):
    bsz, s, d = x.shape
    d_sgu = SGU_GROUPS * SGU_GROUP_DIM
    nq = N_HEADS * HEAD_DIM
    nkv = N_KV_HEADS * HEAD_DIM
    nqi = IDX_HEADS * IDX_DIM
    topk = min(TOPK_MAX, s // 4)

    o_q = 2 * d_sgu
    o_qi = o_q + nq + 2 * nkv
    w_uv = w_in[:, :o_q].astype(BF16)
    wt_main = w_in[:, o_q:o_qi].T.astype(BF16)
    n_idx = nqi + IDX_DIM + IDX_HEADS
    wt_idx = jnp.pad(w_in[:, o_qi:o_qi + n_idx].T, ((0, (-n_idx) % 16), (0, 0)))
    wt_idx_hi = wt_idx.astype(BF16)
    wt_idx_lo = (wt_idx - wt_idx_hi.astype(F32)).astype(BF16)
    b_sp = jnp.broadcast_to(b_spatial[:, :, None], (SGU_GROUPS, CHUNK, SGU_GROUP_DIM))

    ya, qt, kk, vt, qi3t, ki3, wt, kn = _in_proj(
        x, mod3, g_pre_mix.reshape(1, d), w_uv, wt_main, wt_idx_hi, wt_idx_lo, cos_t, sin_t,
        g_sgu_v.reshape(1, d_sgu), w_spatial, b_sp, g_out_sgu.reshape(1, d_sgu))

    g_attn_b = jnp.broadcast_to(g_out_attn[:, None], (nq, ATT_TQ))
    yb = _dsa_attn(qi3t, wt, ki3, qt, kk, vt, kn, g_attn_b, topk)

    return _out_ffn(x, ya, yb, mod3, w_out.astype(BF16), g_post_mix.reshape(1, d), g_pre_ffn.reshape(1, d),
                    w_ff1.astype(BF16), w_ff2.astype(BF16), g_post_ffn.reshape(1, d))


def kernel(x, c, positions, w_ada, b_ada, g_pre_mix, w_in, g_sgu_v, w_spatial, b_spatial, g_out_sgu, g_out_attn,
           w_out, g_post_mix, g_pre_ffn, w_ff1, w_ff2, g_post_ffn):
    bsz, s, d = x.shape
    inv_freq = ROPE_THETA ** (-jnp.arange(0, ROT_DIM, 2, dtype=F32) / ROT_DIM)
    ang = positions.astype(F32)[:, None, :] * inv_freq[None, :, None]
    cos_t, sin_t = jnp.cos(ang), jnp.sin(ang)
    for l in range(w_ada.shape[0]):
        mod3 = _ada_mod(c, w_ada[l], b_ada[l]).reshape(bsz, 6, d)
        x = _layer(x, mod3, cos_t, sin_t, g_pre_mix[l], w_in[l], g_sgu_v[l], w_spatial[l], b_spatial[l],
                   g_out_sgu[l], g_out_attn[l], w_out[l], g_post_mix[l], g_pre_ffn[l], w_ff1[l], w_ff2[l],
                   g_post_ffn[l])
    return x
```

```python
import functools

import jax
import jax.numpy as jnp
from jax import lax
from jax.experimental import pallas as pl
from jax.experimental.pallas import tpu as pltpu

F32 = jnp.float32
BF16 = jnp.bfloat16

SGU_GROUPS = 4
SGU_GROUP_DIM = 128
CHUNK = 128
N_HEADS = 8
HEAD_DIM = 64
N_KV_HEADS = 2
GQA_GROUP = N_HEADS // N_KV_HEADS
ROT_DIM = 16
ROT_HALF = ROT_DIM // 2
ROPE_THETA = 500000.0
IDX_HEADS = 8
IDX_DIM = 64
TOPK_MAX = 256
EPS = 1e-6

ADA_TN = 1024
PROJ_TM = 1024
PROJ_SUB = 256
ATT_TQ = 256
ATT_CH = 256
ATT_NB = 2
FFN_TM = 512
FFN_TF = 1024
BISECT_UNROLL = 4
BISECT_BLIND_ROUNDS = 4
BISECT_CHECKED_STEPS = 2
COUNT_LANES = 32
IDX_K = 4 * IDX_DIM
ONES_ROWS = 16
NEG = -1e30
TINY = 1e-30
JUST_ABOVE = 1.0 + 2.0 ** -20
F32_MAX = 3.4028235e38
LOGIT_BOUND = 96.0
VMEM_LIMIT_BYTES = 56 * 1024 * 1024
Q_SCALE = (HEAD_DIM ** -0.5) * 1.4426950408889634


def _split_bf16(a):
    hi = a.astype(BF16).astype(F32)
    lo = (a - hi).astype(BF16).astype(F32)
    return hi, lo


def _rms(a, g):
    return a * lax.rsqrt(jnp.mean(a * a, axis=-1, keepdims=True) + EPS) * g


def _ada_kernel(c_ref, w_ref, b_ref, o_ref):
    c = c_ref[...]
    ca = c * (1.0 / (1.0 + jnp.exp(-c)))
    c_hi, c_lo = _split_bf16(ca)
    w_hi, w_lo = _split_bf16(w_ref[...])
    c_hi, c_lo, w_hi, w_lo = (t.astype(BF16) for t in (c_hi, c_lo, w_hi, w_lo))
    acc = jnp.dot(c_hi, w_hi, preferred_element_type=F32)
    acc += jnp.dot(c_hi, w_lo, preferred_element_type=F32)
    acc += jnp.dot(c_lo, w_hi, preferred_element_type=F32)
    o_ref[...] = acc + b_ref[...]


def _ada_mod(c, w_ada, b_ada):
    bsz, d = c.shape
    n = w_ada.shape[1]
    return pl.pallas_call(
        _ada_kernel,
        grid=(n // ADA_TN,),
        in_specs=[
            pl.BlockSpec((bsz, d), lambda i: (0, 0)),
            pl.BlockSpec((d, ADA_TN), lambda i: (0, i)),
            pl.BlockSpec((1, ADA_TN), lambda i: (0, i)),
        ],
        out_specs=pl.BlockSpec((bsz, ADA_TN), lambda i: (0, i)),
        out_shape=jax.ShapeDtypeStruct((bsz, n), F32),
        compiler_params=pltpu.CompilerParams(dimension_semantics=("arbitrary",)),
        name="ada_mod",
    )(c, w_ada, b_ada.reshape(1, n))


def _rope_rows(p, cos, sin):
    x1, x2 = p[0:ROT_HALF], p[ROT_HALF:ROT_DIM]
    return jnp.concatenate([x1 * cos - x2 * sin, x2 * cos + x1 * sin, p[ROT_DIM:]], axis=0)


def _dot_nt(a, b):
    return lax.dot_general(a, b, (((1,), (1,)), ((), ())), preferred_element_type=F32)


def _in_proj_kernel(x_ref, mod_ref, gpre_ref, wuv_ref, wtm_ref, wti_hi_ref, wti_lo_ref, cos_ref, sin_ref,
                    gv_ref, wsp_ref, bsp_ref, gosgu_ref,
                    ya_ref, qt_ref, k_ref, vt_ref, qi_ref, ki_ref, wt_ref, kn_ref):
    d_sgu = SGU_GROUPS * SGU_GROUP_DIM
    nq = N_HEADS * HEAD_DIM
    nkv = N_KV_HEADS * HEAD_DIM
    nqi = IDX_HEADS * IDX_DIM
    sh1 = mod_ref[0, 0:1, :]
    sc1 = mod_ref[0, 1:2, :]
    row = lax.broadcasted_iota(jnp.int32, (CHUNK, CHUNK), 0)
    col = lax.broadcasted_iota(jnp.int32, (CHUNK, CHUNK), 1)
    w_mix = [jnp.where(col <= row, wsp_ref[g], 0.0).astype(BF16) for g in range(SGU_GROUPS)]

    def project(t0, tn):
        tok = slice(t0, t0 + tn)
        h = _rms(x_ref[0, tok, :], gpre_ref[...]) * (1.0 + sc1) + sh1
        h_hi32, h_lo32 = _split_bf16(h)
        h_hi, h_lo = h_hi32.astype(BF16), h_lo32.astype(BF16)
        cos, sin = cos_ref[0, :, tok], sin_ref[0, :, tok]

        puv = jnp.dot(h_hi, wuv_ref[...], preferred_element_type=F32)
        pti = _dot_nt(wti_hi_ref[...], h_hi) + _dot_nt(wti_lo_ref[...], h_hi) + _dot_nt(wti_hi_ref[...], h_lo)
        ptm = _dot_nt(wtm_ref[...], h_hi)

        zu = jax.nn.gelu(puv[:, :d_sgu])
        zv = jax.nn.gelu(puv[:, d_sgu:])
        ya_groups = []
        for g in range(SGU_GROUPS):
            sl = slice(g * SGU_GROUP_DIM, (g + 1) * SGU_GROUP_DIM)
            vn = _rms(zv[:, sl], gv_ref[:, sl]).astype(BF16)
            parts = []
            for ci in range(tn // CHUNK):
                rs = slice(ci * CHUNK, (ci + 1) * CHUNK)
                mixed = jnp.dot(w_mix[g], vn[rs], preferred_element_type=F32) + bsp_ref[g]
                parts.append(zu[rs, sl] * mixed)
            ya_groups.append(jnp.concatenate(parts, axis=0))
        ya = jnp.concatenate(ya_groups, axis=1)
        ya_ref[0, tok, :] = _rms(ya, gosgu_ref[...]).astype(BF16)

        for hh in range(N_HEADS):
            rs = slice(hh * HEAD_DIM, (hh + 1) * HEAD_DIM)
            qt_ref[0, rs, tok] = (_rope_rows(ptm[rs], cos, sin) * Q_SCALE).astype(BF16)
        kt = jnp.concatenate(
            [_rope_rows(ptm[nq + g * HEAD_DIM:nq + (g + 1) * HEAD_DIM], cos, sin) for g in range(N_KV_HEADS)],
            axis=0)
        k_ref[0, tok, :] = kt.T.astype(BF16)
        kb = kt.astype(BF16).astype(F32)
        kn_ref[0, :, tok] = jnp.concatenate(
            [jnp.sum(jnp.square(kb[g * HEAD_DIM:(g + 1) * HEAD_DIM]), axis=0, keepdims=True)
             for g in range(N_KV_HEADS)], axis=0)
        vt = ptm[nq + nkv:nq + 2 * nkv].astype(BF16)
        for ci in range(tn // ATT_CH):
            vt_ref[0, t0 // ATT_CH + ci] = vt[:, ci * ATT_CH:(ci + 1) * ATT_CH]

        for hh in range(IDX_HEADS):
            r = _rope_rows(pti[hh * IDX_DIM:(hh + 1) * IDX_DIM], cos, sin)
            r_hi, r_lo = _split_bf16(r)
            qi_ref[0, hh, :, tok] = jnp.concatenate([r_hi, r_lo], axis=0).astype(BF16)
        kr = _rope_rows(pti[nqi:nqi + IDX_DIM], cos, sin)
        k_hi, k_lo = _split_bf16(kr)
        ki_ref[0, tok, :] = jnp.concatenate([k_hi, k_lo, k_hi, k_lo], axis=0).T.astype(BF16)
        wt_ref[0, :, tok] = (pti[nqi + IDX_DIM:nqi + IDX_DIM + IDX_HEADS]
                             * ((IDX_HEADS ** -0.5) * (IDX_DIM ** -0.5)))

    for t0 in range(0, x_ref.shape[1], PROJ_SUB):
        project(t0, PROJ_SUB)


def _in_proj(x, mod3, g_pre, w_uv, wt_main, wt_idx_hi, wt_idx_lo, cos_t, sin_t, g_v, w_sp, b_sp, g_osgu):
    bsz, s, d = x.shape
    tm = PROJ_TM
    d_sgu = SGU_GROUPS * SGU_GROUP_DIM
    nq = N_HEADS * HEAD_DIM
    nkv = N_KV_HEADS * HEAD_DIM
    const2 = lambda b, i: (0, 0)
    const3 = lambda b, i: (0, 0, 0)
    return pl.pallas_call(
        _in_proj_kernel,
        grid=(bsz, s // tm),
        in_specs=[
            pl.BlockSpec((1, tm, d), lambda b, i: (b, i, 0)),
            pl.BlockSpec((1, 6, d), lambda b, i: (b, 0, 0)),
            pl.BlockSpec((1, d), const2),
            pl.BlockSpec(w_uv.shape, const2),
            pl.BlockSpec(wt_main.shape, const2),
            pl.BlockSpec(wt_idx_hi.shape, const2),
            pl.BlockSpec(wt_idx_lo.shape, const2),
            pl.BlockSpec((1, ROT_HALF, tm), lambda b, i: (b, 0, i)),
            pl.BlockSpec((1, ROT_HALF, tm), lambda b, i: (b, 0, i)),
            pl.BlockSpec((1, d_sgu), const2),
            pl.BlockSpec(w_sp.shape, const3),
            pl.BlockSpec(b_sp.shape, const3),
            pl.BlockSpec((1, d_sgu), const2),
        ],
        out_specs=[
            pl.BlockSpec((1, tm, d_sgu), lambda b, i: (b, i, 0)),
            pl.BlockSpec((1, nq, tm), lambda b, i: (b, 0, i)),
            pl.BlockSpec((1, tm, nkv), lambda b, i: (b, i, 0)),
            pl.BlockSpec((1, tm // ATT_CH, nkv, ATT_CH), lambda b, i: (b, i, 0, 0)),
            pl.BlockSpec((1, IDX_HEADS, 2 * IDX_DIM, tm), lambda b, i: (b, 0, 0, i)),
            pl.BlockSpec((1, tm, IDX_K), lambda b, i: (b, i, 0)),
            pl.BlockSpec((1, IDX_HEADS, tm), lambda b, i: (b, 0, i)),
            pl.BlockSpec((1, N_KV_HEADS, tm), lambda b, i: (b, 0, i)),
        ],
        out_shape=[
            jax.ShapeDtypeStruct((bsz, s, d_sgu), BF16),
            jax.ShapeDtypeStruct((bsz, nq, s), BF16),
            jax.ShapeDtypeStruct((bsz, s, nkv), BF16),
            jax.ShapeDtypeStruct((bsz, s // ATT_CH, nkv, ATT_CH), BF16),
            jax.ShapeDtypeStruct((bsz, IDX_HEADS, 2 * IDX_DIM, s), BF16),
            jax.ShapeDtypeStruct((bsz, s, IDX_K), BF16),
            jax.ShapeDtypeStruct((bsz, IDX_HEADS, s), F32),
            jax.ShapeDtypeStruct((bsz, N_KV_HEADS, s), F32),
        ],
        compiler_params=pltpu.CompilerParams(
            dimension_semantics=("arbitrary", "arbitrary"), vmem_limit_bytes=VMEM_LIMIT_BYTES),
        name="in_proj",
    )(x, mod3, g_pre, w_uv, wt_main, wt_idx_hi, wt_idx_lo, cos_t, sin_t, g_v, w_sp, b_sp, g_osgu)


def _attn_kernel(qi_ref, w_ref, ki_ref, q_ref, k_ref, vt_ref, kn_ref, g_ref, o_ref,
                 sc_ref, p_ref, qi4_ref, qpad_ref, m_ref, acc_ref, *, topk):
    nb = qi_ref.shape[0]
    rows = range(nb)
    tq, ch = ATT_TQ, ATT_CH
    j = pl.program_id(1)
    nch = j + 1
    q_pos = j * tq + lax.broadcasted_iota(jnp.int32, (1, tq), 1)
    row_iota = lax.broadcasted_iota(jnp.int32, (ch, tq), 0)

    def fold8(a, op):
        return op(a.reshape(ch // 8, 8, tq), axis=0)

    def chunk_rows(kc):
        return pl.ds(pl.multiple_of(kc * ch, ch), ch)

    for bi in rows:
        for hh in range(IDX_HEADS):
            q_hi, q_lo = qi_ref[bi, hh, 0:IDX_DIM, :], qi_ref[bi, hh, IDX_DIM:, :]
            qi4_ref[bi, hh] = jnp.concatenate([q_hi, q_hi, q_lo, q_lo], axis=0)

    def score_chunk(kc, carry, diagonal):
        out = []
        for bi in rows:
            mn, mx, n_pos, n_nonneg = carry[bi]
            kic = ki_ref[bi, chunk_rows(kc), :]
            acc = jnp.zeros((ch, tq), F32)
            for hh in range(IDX_HEADS):
                dd = jnp.dot(kic, qi4_ref[bi, hh], preferred_element_type=F32)
                acc = acc + w_ref[bi, hh:hh + 1, :] * jnp.maximum(dd, 0.0)
            if diagonal:
                causal = (row_iota + kc * ch) <= q_pos
                sc = jnp.where(causal, acc, -jnp.inf)
                mn = jnp.minimum(mn, fold8(jnp.where(causal, acc, jnp.inf), jnp.min))
            else:
                sc = acc
                mn = jnp.minimum(mn, fold8(acc, jnp.min))
            sc_ref[bi, kc] = sc
            mx = jnp.maximum(mx, fold8(sc, jnp.max))
            n_pos = n_pos + fold8(jnp.where(sc > 0.0, 1.0, 0.0), jnp.sum)
            n_nonneg = n_nonneg + fold8(jnp.where(sc >= 0.0, 1.0, 0.0), jnp.sum)
            out.append((mn, mx, n_pos, n_nonneg))
        return tuple(out)

    stats0 = (jnp.full((8, tq), jnp.inf, F32), jnp.full((8, tq), -jnp.inf, F32),
              jnp.zeros((8, tq), F32), jnp.zeros((8, tq), F32))
    full = functools.partial(score_chunk, diagonal=False)
    stats = lax.fori_loop(0, j // 2, lambda t, c: full(2 * t + 1, full(2 * t, c)), tuple(stats0 for _ in rows))
    stats = lax.cond(j % 2 == 1, lambda c: score_chunk(j, full(j - 1, c), diagonal=True),
                     lambda c: score_chunk(j, c, diagonal=True), stats)

    def count(preds):
        def body(kc, cs):
            return tuple(c + jnp.sum(jnp.where(pred(sc_ref[bi, kc]), 1.0, 0.0).reshape(-1, COUNT_LANES, tq), axis=0)
                         for bi, (c, pred) in enumerate(zip(cs, preds)))
        cs = lax.fori_loop(0, nch, body, tuple(jnp.zeros((COUNT_LANES, tq), F32) for _ in rows))
        return [jnp.sum(c, axis=0, keepdims=True) for c in cs]

    n_causal = (q_pos + 1).astype(F32)
    kk = jnp.minimum(float(topk), n_causal)

    def as_f(m):
        return jnp.where(m, 1.0, 0.0)

    def initial_state(stat):
        mn = jnp.min(stat[0], axis=0, keepdims=True)
        mx = jnp.max(stat[1], axis=0, keepdims=True)
        c0_gt = jnp.sum(stat[2], axis=0, keepdims=True)
        c0_ge = jnp.sum(stat[3], axis=0, keepdims=True)
        at_zero = (c0_gt < kk) & (c0_ge >= kk)
        take_all = n_causal == kk
        pos = c0_gt >= kk
        lo0 = jnp.where(pos, jnp.maximum(mn, 0.0), mn)
        clo0 = jnp.where(pos & (mn <= 0.0), c0_ge, n_causal)
        hi0 = jnp.where(pos, jnp.minimum(mx * JUST_ABOVE + TINY, F32_MAX), 0.0)
        thr0 = jnp.where(take_all, mn, 0.0)
        tied0 = jnp.logical_not(take_all) & at_zero & (c0_ge > kk)
        done0 = at_zero | take_all
        return lo0, clo0, hi0, thr0, as_f(done0), as_f(tied0)

    def bisect_step(states):
        mids = [0.5 * st[0] + 0.5 * st[2] for st in states]
        counts = count([lambda s, mid=mid: s >= mid for mid in mids])
        out = []
        for (lo, clo, hi, thr, done, tied), mid, c in zip(states, mids, counts):
            valid = (mid > lo) & (mid < hi)
            move_lo = valid & (c >= kk)
            thr = jnp.where(done, thr, jnp.where(valid, mid, lo))
            tied = tied | (jnp.logical_not(done) & jnp.logical_not(valid) & (clo > kk))
            done = done | jnp.logical_not(valid) | (c == kk)
            hi = jnp.where(valid & (c < kk), mid, hi)
            lo = jnp.where(move_lo, mid, lo)
            clo = jnp.where(move_lo, c, clo)
            out.append((lo, clo, hi, thr, done, tied))
        return out

    def bisect_round(states, steps):
        sts = [(lo, clo, hi, thr, done_f > 0.5, tied_f > 0.5) for lo, clo, hi, thr, done_f, tied_f in states]
        for _ in range(steps):
            sts = bisect_step(sts)
        return tuple((lo, clo, hi, thr, as_f(done), as_f(tied)) for lo, clo, hi, thr, done, tied in sts)

    def bisect_body(carry):
        states = bisect_round(carry[0], BISECT_CHECKED_STEPS)
        return states, jnp.max(1.0 - functools.reduce(jnp.minimum, [st[4] for st in states]))

    blind = jnp.where((j + 1) * tq <= topk, 0, BISECT_BLIND_ROUNDS)
    states = lax.fori_loop(0, blind, lambda _, sts: bisect_round(sts, BISECT_UNROLL),
                           tuple(initial_state(stat) for stat in stats))
    states, _ = lax.while_loop(
        lambda carry: carry[1] > 0.0, bisect_body, (states, jnp.where(blind > 0, 1.0, 0.0)))
    thrs = [st[3] for st in states]
    tied_f = functools.reduce(jnp.maximum, [st[5] for st in states])

    def logit_bound_sq(bi):
        q_sq = jnp.sum(jnp.square(q_ref[bi].astype(F32)).reshape(N_HEADS, HEAD_DIM, tq), axis=1)
        q_sq = jnp.max(jnp.max(q_sq, axis=0, keepdims=True), axis=1, keepdims=True)
        k_sq = jnp.max(jnp.max(kn_ref[bi], axis=0, keepdims=True), axis=1, keepdims=True)
        return q_sq * k_sq
    worst = functools.reduce(jnp.maximum, [logit_bound_sq(bi) for bi in rows])
    bounded_f = jnp.where(worst <= LOGIT_BOUND * LOGIT_BOUND, 2.0, 0.0)

    code = jnp.max(tied_f + bounded_f)
    tie = jnp.logical_or(code == 1.0, code == 3.0)
    bounded = code >= 2.0

    @pl.when(jnp.logical_not(tie))
    def _():
        def body(kc, _):
            for bi in rows:
                sc_ref[bi, kc] = jnp.where(sc_ref[bi, kc] >= thrs[bi], 0.0, NEG)
            return 0
        lax.fori_loop(0, nch, body, 0)

    @pl.when(tie)
    def _():
        r = lax.broadcasted_iota(jnp.int32, (ch, ch), 0)
        c = lax.broadcasted_iota(jnp.int32, (ch, ch), 1)
        below = jnp.where(c < r, 1.0, 0.0).astype(BF16)
        n_gt = count([lambda s, thr=thr: s > thr for thr in thrs])
        needs = [kk - n for n in n_gt]

        def body(kc, seen):
            out = []
            for bi in rows:
                s = sc_ref[bi, kc]
                eq = jnp.where(s == thrs[bi], 1.0, 0.0)
                rank = jnp.dot(below, eq.astype(BF16), preferred_element_type=F32) + seen[bi]
                sel = (s > thrs[bi]) | ((s == thrs[bi]) & (rank < needs[bi]))
                sc_ref[bi, kc] = jnp.where(sel, 0.0, NEG)
                out.append(seen[bi] + jnp.sum(eq, axis=0, keepdims=True))
            return tuple(out)
        lax.fori_loop(0, nch, body, tuple(jnp.zeros((1, tq), F32) for _ in rows))

    zq = jnp.zeros((HEAD_DIM, tq), BF16)
    for bi in rows:
        for hh in range(N_HEADS):
            qh = q_ref[bi, hh * HEAD_DIM:(hh + 1) * HEAD_DIM, :]
            qpad_ref[bi, hh] = jnp.concatenate([qh, zq] if hh < GQA_GROUP else [zq, qh], axis=0)
    acc_ref[...] = jnp.zeros(acc_ref.shape, F32)
    m_ref[...] = jnp.zeros(m_ref.shape, F32)
    ones = jnp.ones((ONES_ROWS, ch), BF16)

    def logits(bi, kc, hh, kcs, bias):
        return jnp.dot(kcs, qpad_ref[bi, hh], preferred_element_type=F32) + bias

    @pl.when(jnp.logical_not(bounded))
    def _():
        def body(kc, ms):
            out = []
            for bi in rows:
                kcs, bias = k_ref[bi, chunk_rows(kc), :], sc_ref[bi, kc]
                out.append(tuple(jnp.maximum(ms[bi][hh], fold8(logits(bi, kc, hh, kcs, bias), jnp.max))
                                 for hh in range(N_HEADS)))
            return tuple(out)
        ms = lax.fori_loop(0, nch, body,
                           tuple(tuple(jnp.full((8, tq), NEG, F32) for _ in range(N_HEADS)) for _ in rows))
        for bi in rows:
            for hh in range(N_HEADS):
                m_ref[bi, hh] = jnp.max(ms[bi][hh], axis=0, keepdims=True)

    def prob_chunk(kc):
        for bi in rows:
            kcs, bias = k_ref[bi, chunk_rows(kc), :], sc_ref[bi, kc]
            for hh in range(N_HEADS):
                p_ref[bi, hh, kc] = jnp.exp2(logits(bi, kc, hh, kcs, bias) - m_ref[bi, hh]).astype(BF16)

    def pv_chunk(kc):
        for bi in rows:
            vt = vt_ref[bi, kc]
            for g in range(N_KV_HEADS):
                vaug = jnp.concatenate([vt[g * HEAD_DIM:(g + 1) * HEAD_DIM], ones], axis=0)
                for hh in range(g * GQA_GROUP, (g + 1) * GQA_GROUP):
                    acc_ref[bi, hh] += jnp.dot(vaug, p_ref[bi, hh, kc], preferred_element_type=F32)

    npair = nch // 2

    @pl.when(npair > 0)
    def _():
        prob_chunk(0)
        prob_chunk(1)

        def attend(t, _):
            pv_chunk(2 * t - 2)
            pv_chunk(2 * t - 1)
            prob_chunk(2 * t)
            prob_chunk(2 * t + 1)
            return 0
        lax.fori_loop(1, npair, attend, 0)
        pv_chunk(2 * npair - 2)
        pv_chunk(2 * npair - 1)

    @pl.when(nch % 2 == 1)
    def _():
        prob_chunk(nch - 1)
        pv_chunk(nch - 1)

    for bi in rows:
        outs = []
        for hh in range(N_HEADS):
            a = acc_ref[bi, hh]
            outs.append(a[0:HEAD_DIM] * (1.0 / a[HEAD_DIM:HEAD_DIM + 1]))
        y = jnp.concatenate(outs, axis=0)
        yn = y * lax.rsqrt(jnp.mean(y * y, axis=0, keepdims=True) + EPS) * g_ref[...]
        o_ref[bi] = yn.T.astype(BF16)


def _dsa_attn(qi3t, wt, ki3, qt, kk, vt, kn, g_attn_b, topk):
    bsz, s, _ = ki3.shape
    tq, ch = ATT_TQ, ATT_CH
    nb = ATT_NB if bsz % ATT_NB == 0 else 1
    nq = N_HEADS * HEAD_DIM
    nkv = N_KV_HEADS * HEAD_DIM
    return pl.pallas_call(
        functools.partial(_attn_kernel, topk=topk),
        grid=(bsz // nb, s // tq),
        in_specs=[
            pl.BlockSpec((nb, IDX_HEADS, 2 * IDX_DIM, tq), lambda b, j: (b, 0, 0, j)),
            pl.BlockSpec((nb, IDX_HEADS, tq), lambda b, j: (b, 0, j)),
            pl.BlockSpec((nb, s, IDX_K), lambda b, j: (b, 0, 0)),
            pl.BlockSpec((nb, nq, tq), lambda b, j: (b, 0, j)),
            pl.BlockSpec((nb, s, nkv), lambda b, j: (b, 0, 0)),
            pl.BlockSpec((nb, s // ch, nkv, ch), lambda b, j: (b, 0, 0, 0)),
            pl.BlockSpec((nb, N_KV_HEADS, s), lambda b, j: (b, 0, 0)),
            pl.BlockSpec((nq, tq), lambda b, j: (0, 0)),
        ],
        out_specs=pl.BlockSpec((nb, tq, nq), lambda b, j: (b, j, 0)),
        out_shape=jax.ShapeDtypeStruct((bsz, s, nq), BF16),
        scratch_shapes=[
            pltpu.VMEM((nb, s // ch, ch, tq), F32),
            pltpu.VMEM((nb, N_HEADS, s // ch, ch, tq), BF16),
            pltpu.VMEM((nb, IDX_HEADS, IDX_K, tq), BF16),
            pltpu.VMEM((nb, N_HEADS, 2 * HEAD_DIM, tq), BF16),
            pltpu.VMEM((nb, N_HEADS, 1, tq), F32),
            pltpu.VMEM((nb, N_HEADS, HEAD_DIM + ONES_ROWS, tq), F32),
        ],
        compiler_params=pltpu.CompilerParams(
            dimension_semantics=("arbitrary", "arbitrary"), vmem_limit_bytes=VMEM_LIMIT_BYTES),
        name="dsa_attn",
    )(qi3t, wt, ki3, qt, kk, vt, kn, g_attn_b)


def _out_ffn_kernel(x_ref, ya_ref, yb_ref, mod_ref, wo_ref, gpm_ref, gpf_ref, w1_ref, w2_ref, gpo_ref, o_ref):
    d_a = ya_ref.shape[2]
    tm = x_ref.shape[1]
    g1 = mod_ref[0, 2:3, :]
    sh2 = mod_ref[0, 3:4, :]
    sc2 = mod_ref[0, 4:5, :]
    g2 = mod_ref[0, 5:6, :]

    def mix_residual(rows):
        o = jnp.dot(ya_ref[0, rows, :], wo_ref[0:d_a, :], preferred_element_type=F32)
        o += jnp.dot(yb_ref[0, rows, :], wo_ref[d_a:, :], preferred_element_type=F32)
        return x_ref[0, rows, :] + g1 * _rms(o, gpm_ref[...])

    def mlp(x1):
        h2 = (_rms(x1, gpf_ref[...]) * (1.0 + sc2) + sh2).astype(BF16)
        f = jnp.zeros(x1.shape, F32)
        for c in range(w1_ref.shape[1] // FFN_TF):
            a = jnp.dot(h2, w1_ref[:, c * FFN_TF:(c + 1) * FFN_TF], preferred_element_type=F32)
            a = jnp.square(jnp.maximum(a, 0.0)).astype(BF16)
            f += jnp.dot(a, w2_ref[c * FFN_TF:(c + 1) * FFN_TF, :], preferred_element_type=F32)
        return f

    halves = [pl.ds(i * (tm // 2), tm // 2) for i in range(2)]
    x1 = [mix_residual(rows) for rows in halves]
    f = [mlp(x1h) for x1h in x1]
    for rows, x1h, fh in zip(halves, x1, f):
        o_ref[0, rows, :] = x1h + g2 * _rms(fh, gpo_ref[...])


def _out_ffn(x, ya, yb, mod3, w_out, g_pm, g_pf, w1, w2, g_po):
    bsz, s, d = x.shape
    tm = FFN_TM
    d_a = ya.shape[2]
    const2 = lambda b, i: (0, 0)
    resident = functools.partial(pl.BlockSpec, index_map=const2, pipeline_mode=pl.Buffered(1))
    return pl.pallas_call(
        _out_ffn_kernel,
        grid=(bsz, s // tm),
        in_specs=[
            pl.BlockSpec((1, tm, d), lambda b, i: (b, i, 0)),
            pl.BlockSpec((1, tm, d_a), lambda b, i: (b, i, 0)),
            pl.BlockSpec((1, tm, yb.shape[2]), lambda b, i: (b, i, 0)),
            pl.BlockSpec((1, 6, d), lambda b, i: (b, 0, 0)),
            resident(w_out.shape),
            pl.BlockSpec((1, d), const2),
            pl.BlockSpec((1, d), const2),
            resident(w1.shape),
            resident(w2.shape),
            pl.BlockSpec((1, d), const2),
        ],
        out_specs=pl.BlockSpec((1, tm, d), lambda b, i: (b, i, 0)),
        out_shape=jax.ShapeDtypeStruct((bsz, s, d), F32),
        compiler_params=pltpu.CompilerParams(
            dimension_semantics=("arbitrary", "arbitrary"), vmem_limit_bytes=VMEM_LIMIT_BYTES),
        name="out_ffn",
    )(x, ya, yb, mod3, w_out, g_pm, g_pf, w1, w2, g_po)


def _layer(x, mod3, cos_t, sin_t, g_pre_mix, w_in, g_sgu_v, w_spatial, b_spatial, g_out_sgu, g_out_attn,
           w_out, g_post_mix, g_pre_ffn, w_ff1, w_ff2, g_post_ffn):
    bsz, s, d = x.shape
    d_sgu = SGU_GROUPS * SGU_GROUP_DIM
    nq = N_HEADS * HEAD_DIM
    nkv = N_KV_HEADS * HEAD_DIM
    nqi = IDX_HEADS * IDX_DIM
    topk = min(TOPK_MAX, s // 4)

    o_q = 2 * d_sgu
    o_qi = o_q + nq + 2 * nkv
    w_uv = w_in[:, :o_q].astype(BF16)
    wt_main = w_in[:, o_q:o_qi].T.astype(BF16)
    n_idx = nqi + IDX_DIM + IDX_HEADS
    wt_idx = jnp.pad(w_in[:, o_qi:o_qi + n_idx].T, ((0, (-n_idx) % 16), (0, 0)))
    wt_idx_hi = wt_idx.astype(BF16)
    wt_idx_lo = (wt_idx - wt_idx_hi.astype(F32)).astype(BF16)
    b_sp = jnp.broadcast_to(b_spatial[:, :, None], (SGU_GROUPS, CHUNK, SGU_GROUP_DIM))

    ya, qt, kk, vt, qi3t, ki3, wt, kn = _in_proj(
        x, mod3, g_pre_mix.reshape(1, d), w_uv, wt_main, wt_idx_hi, wt_idx_lo, cos_t, sin_t,
        g_sgu_v.reshape(1, d_sgu), w_spatial, b_sp, g_out_sgu.reshape(1, d_sgu))

    g_attn_b = jnp.broadcast_to(g_out_attn[:, None], (nq, ATT_TQ))
    yb = _dsa_attn(qi3t, wt, ki3, qt, kk, vt, kn, g_attn_b, topk)

    return _out_ffn(x, ya, yb, mod3, w_out.astype(BF16), g_post_mix.reshape(1, d), g_pre_ffn.reshape(1, d),
                    w_ff1.astype(BF16), w_ff2.astype(BF16), g_post_ffn.reshape(1, d))


def kernel(x, c, positions, w_ada, b_ada, g_pre_mix, w_in, g_sgu_v, w_spatial, b_spatial, g_out_sgu, g_out_attn,
           w_out, g_post_mix, g_pre_ffn, w_ff1, w_ff2, g_post_ffn):
    bsz, s, d = x.shape
    inv_freq = ROPE_THETA ** (-jnp.arange(0, ROT_DIM, 2, dtype=F32) / ROT_DIM)
    ang = positions.astype(F32)[:, None, :] * inv_freq[None, :, None]
    cos_t, sin_t = jnp.cos(ang), jnp.sin(ang)
    for l in range(w_ada.shape[0]):
        mod3 = _ada_mod(c, w_ada[l], b_ada[l]).reshape(bsz, 6, d)
        x = _layer(x, mod3, cos_t, sin_t, g_pre_mix[l], w_in[l], g_sgu_v[l], w_spatial[l], b_spatial[l],
                   g_out_sgu[l], g_out_attn[l], w_out[l], g_post_mix[l], g_pre_ffn[l], w_ff1[l], w_ff2[l],
                   g_post_ffn[l])
    return x
```

```python
import functools

import jax
import jax.numpy as jnp
from jax import lax
from jax.experimental import pallas as pl
from jax.experimental.pallas import tpu as pltpu

F32 = jnp.float32
BF16 = jnp.bfloat16

SGU_GROUPS = 4
SGU_GROUP_DIM = 128
CHUNK = 128
N_HEADS = 8
HEAD_DIM = 64
N_KV_HEADS = 2
GQA_GROUP = N_HEADS // N_KV_HEADS
ROT_DIM = 16
ROT_HALF = ROT_DIM // 2
ROPE_THETA = 500000.0
IDX_HEADS = 8
IDX_DIM = 64
TOPK_MAX = 256
EPS = 1e-6

ADA_TN = 1024
PROJ_TM = 1024
PROJ_SUB = 256
ATT_TQ = 256
ATT_CH = 256
ATT_NB = 2
FFN_TM = 512
FFN_TF = 1024
BISECT_UNROLL = 4
BISECT_BLIND_ROUNDS = 4
BISECT_CHECKED_STEPS = 2
COUNT_LANES = 16
IDX_K = 4 * IDX_DIM
ONES_ROWS = 16
NEG = -1e30
TINY = 1e-30
JUST_ABOVE = 1.0 + 2.0 ** -20
F32_MAX = 3.4028235e38
LOGIT_BOUND = 96.0
VMEM_LIMIT_BYTES = 56 * 1024 * 1024
Q_SCALE = (HEAD_DIM ** -0.5) * 1.4426950408889634


def _split_bf16(a):
    hi = a.astype(BF16).astype(F32)
    lo = (a - hi).astype(BF16).astype(F32)
    return hi, lo


def _rms(a, g):
    return a * lax.rsqrt(jnp.mean(a * a, axis=-1, keepdims=True) + EPS) * g


def _ada_kernel(c_ref, w_ref, b_ref, o_ref):
    c = c_ref[...]
    ca = c * (1.0 / (1.0 + jnp.exp(-c)))
    c_hi, c_lo = _split_bf16(ca)
    w_hi, w_lo = _split_bf16(w_ref[...])
    c_hi, c_lo, w_hi, w_lo = (t.astype(BF16) for t in (c_hi, c_lo, w_hi, w_lo))
    acc = jnp.dot(c_hi, w_hi, preferred_element_type=F32)
    acc += jnp.dot(c_hi, w_lo, preferred_element_type=F32)
    acc += jnp.dot(c_lo, w_hi, preferred_element_type=F32)
    o_ref[...] = acc + b_ref[...]


def _ada_mod(c, w_ada, b_ada):
    bsz, d = c.shape
    n = w_ada.shape[1]
    return pl.pallas_call(
        _ada_kernel,
        grid=(n // ADA_TN,),
        in_specs=[
            pl.BlockSpec((bsz, d), lambda i: (0, 0)),
            pl.BlockSpec((d, ADA_TN), lambda i: (0, i)),
            pl.BlockSpec((1, ADA_TN), lambda i: (0, i)),
        ],
        out_specs=pl.BlockSpec((bsz, ADA_TN), lambda i: (0, i)),
        out_shape=jax.ShapeDtypeStruct((bsz, n), F32),
        compiler_params=pltpu.CompilerParams(dimension_semantics=("arbitrary",)),
        name="ada_mod",
    )(c, w_ada, b_ada.reshape(1, n))


def _rope_rows(p, cos, sin):
    x1, x2 = p[0:ROT_HALF], p[ROT_HALF:ROT_DIM]
    return jnp.concatenate([x1 * cos - x2 * sin, x2 * cos + x1 * sin, p[ROT_DIM:]], axis=0)


def _dot_nt(a, b):
    return lax.dot_general(a, b, (((1,), (1,)), ((), ())), preferred_element_type=F32)


def _in_proj_kernel(x_ref, mod_ref, gpre_ref, wuv_ref, wtm_ref, wti_hi_ref, wti_lo_ref, cos_ref, sin_ref,
                    gv_ref, wsp_ref, bsp_ref, gosgu_ref,
                    ya_ref, qt_ref, k_ref, vt_ref, qi_ref, ki_ref, wt_ref, kn_ref):
    d_sgu = SGU_GROUPS * SGU_GROUP_DIM
    nq = N_HEADS * HEAD_DIM
    nkv = N_KV_HEADS * HEAD_DIM
    nqi = IDX_HEADS * IDX_DIM
    sh1 = mod_ref[0, 0:1, :]
    sc1 = mod_ref[0, 1:2, :]
    row = lax.broadcasted_iota(jnp.int32, (CHUNK, CHUNK), 0)
    col = lax.broadcasted_iota(jnp.int32, (CHUNK, CHUNK), 1)
    w_mix = [jnp.where(col <= row, wsp_ref[g], 0.0).astype(BF16) for g in range(SGU_GROUPS)]

    def project(t0, tn):
        tok = slice(t0, t0 + tn)
        h = _rms(x_ref[0, tok, :], gpre_ref[...]) * (1.0 + sc1) + sh1
        h_hi32, h_lo32 = _split_bf16(h)
        h_hi, h_lo = h_hi32.astype(BF16), h_lo32.astype(BF16)
        cos, sin = cos_ref[0, :, tok], sin_ref[0, :, tok]

        puv = jnp.dot(h_hi, wuv_ref[...], preferred_element_type=F32)
        pti = _dot_nt(wti_hi_ref[...], h_hi) + _dot_nt(wti_lo_ref[...], h_hi) + _dot_nt(wti_hi_ref[...], h_lo)
        ptm = _dot_nt(wtm_ref[...], h_hi)

        zu = jax.nn.gelu(puv[:, :d_sgu])
        zv = jax.nn.gelu(puv[:, d_sgu:])
        ya_groups = []
        for g in range(SGU_GROUPS):
            sl = slice(g * SGU_GROUP_DIM, (g + 1) * SGU_GROUP_DIM)
            vn = _rms(zv[:, sl], gv_ref[:, sl]).astype(BF16)
            parts = []
            for ci in range(tn // CHUNK):
                rs = slice(ci * CHUNK, (ci + 1) * CHUNK)
                mixed = jnp.dot(w_mix[g], vn[rs], preferred_element_type=F32) + bsp_ref[g]
                parts.append(zu[rs, sl] * mixed)
            ya_groups.append(jnp.concatenate(parts, axis=0))
        ya = jnp.concatenate(ya_groups, axis=1)
        ya_ref[0, tok, :] = _rms(ya, gosgu_ref[...]).astype(BF16)

        for hh in range(N_HEADS):
            rs = slice(hh * HEAD_DIM, (hh + 1) * HEAD_DIM)
            qt_ref[0, rs, tok] = (_rope_rows(ptm[rs], cos, sin) * Q_SCALE).astype(BF16)
        kt = jnp.concatenate(
            [_rope_rows(ptm[nq + g * HEAD_DIM:nq + (g + 1) * HEAD_DIM], cos, sin) for g in range(N_KV_HEADS)],
            axis=0)
        k_ref[0, tok, :] = kt.T.astype(BF16)
        kb = kt.astype(BF16).astype(F32)
        kn_ref[0, :, tok] = jnp.concatenate(
            [jnp.sum(jnp.square(kb[g * HEAD_DIM:(g + 1) * HEAD_DIM]), axis=0, keepdims=True)
             for g in range(N_KV_HEADS)], axis=0)
        vt = ptm[nq + nkv:nq + 2 * nkv].astype(BF16)
        for ci in range(tn // ATT_CH):
            vt_ref[0, t0 // ATT_CH + ci] = vt[:, ci * ATT_CH:(ci + 1) * ATT_CH]

        for hh in range(IDX_HEADS):
            r = _rope_rows(pti[hh * IDX_DIM:(hh + 1) * IDX_DIM], cos, sin)
            r_hi, r_lo = _split_bf16(r)
            qi_ref[0, hh, :, tok] = jnp.concatenate([r_hi, r_lo], axis=0).astype(BF16)
        kr = _rope_rows(pti[nqi:nqi + IDX_DIM], cos, sin)
        k_hi, k_lo = _split_bf16(kr)
        ki_ref[0, tok, :] = jnp.concatenate([k_hi, k_lo, k_hi, k_lo], axis=0).T.astype(BF16)
        wt_ref[0, :, tok] = (pti[nqi + IDX_DIM:nqi + IDX_DIM + IDX_HEADS]
                             * ((IDX_HEADS ** -0.5) * (IDX_DIM ** -0.5)))

    for t0 in range(0, x_ref.shape[1], PROJ_SUB):
        project(t0, PROJ_SUB)


def _in_proj(x, mod3, g_pre, w_uv, wt_main, wt_idx_hi, wt_idx_lo, cos_t, sin_t, g_v, w_sp, b_sp, g_osgu):
    bsz, s, d = x.shape
    tm = PROJ_TM
    d_sgu = SGU_GROUPS * SGU_GROUP_DIM
    nq = N_HEADS * HEAD_DIM
    nkv = N_KV_HEADS * HEAD_DIM
    const2 = lambda b, i: (0, 0)
    const3 = lambda b, i: (0, 0, 0)
    return pl.pallas_call(
        _in_proj_kernel,
        grid=(bsz, s // tm),
        in_specs=[
            pl.BlockSpec((1, tm, d), lambda b, i: (b, i, 0)),
            pl.BlockSpec((1, 6, d), lambda b, i: (b, 0, 0)),
            pl.BlockSpec((1, d), const2),
            pl.BlockSpec(w_uv.shape, const2),
            pl.BlockSpec(wt_main.shape, const2),
            pl.BlockSpec(wt_idx_hi.shape, const2),
            pl.BlockSpec(wt_idx_lo.shape, const2),
            pl.BlockSpec((1, ROT_HALF, tm), lambda b, i: (b, 0, i)),
            pl.BlockSpec((1, ROT_HALF, tm), lambda b, i: (b, 0, i)),
            pl.BlockSpec((1, d_sgu), const2),
            pl.BlockSpec(w_sp.shape, const3),
            pl.BlockSpec(b_sp.shape, const3),
            pl.BlockSpec((1, d_sgu), const2),
        ],
        out_specs=[
            pl.BlockSpec((1, tm, d_sgu), lambda b, i: (b, i, 0)),
            pl.BlockSpec((1, nq, tm), lambda b, i: (b, 0, i)),
            pl.BlockSpec((1, tm, nkv), lambda b, i: (b, i, 0)),
            pl.BlockSpec((1, tm // ATT_CH, nkv, ATT_CH), lambda b, i: (b, i, 0, 0)),
            pl.BlockSpec((1, IDX_HEADS, 2 * IDX_DIM, tm), lambda b, i: (b, 0, 0, i)),
            pl.BlockSpec((1, tm, IDX_K), lambda b, i: (b, i, 0)),
            pl.BlockSpec((1, IDX_HEADS, tm), lambda b, i: (b, 0, i)),
            pl.BlockSpec((1, N_KV_HEADS, tm), lambda b, i: (b, 0, i)),
        ],
        out_shape=[
            jax.ShapeDtypeStruct((bsz, s, d_sgu), BF16),
            jax.ShapeDtypeStruct((bsz, nq, s), BF16),
            jax.ShapeDtypeStruct((bsz, s, nkv), BF16),
            jax.ShapeDtypeStruct((bsz, s // ATT_CH, nkv, ATT_CH), BF16),
            jax.ShapeDtypeStruct((bsz, IDX_HEADS, 2 * IDX_DIM, s), BF16),
            jax.ShapeDtypeStruct((bsz, s, IDX_K), BF16),
            jax.ShapeDtypeStruct((bsz, IDX_HEADS, s), F32),
            jax.ShapeDtypeStruct((bsz, N_KV_HEADS, s), F32),
        ],
        compiler_params=pltpu.CompilerParams(
            dimension_semantics=("arbitrary", "arbitrary"), vmem_limit_bytes=VMEM_LIMIT_BYTES),
        name="in_proj",
    )(x, mod3, g_pre, w_uv, wt_main, wt_idx_hi, wt_idx_lo, cos_t, sin_t, g_v, w_sp, b_sp, g_osgu)


def _attn_kernel(qi_ref, w_ref, ki_ref, q_ref, k_ref, vt_ref, kn_ref, g_ref, o_ref,
                 sc_ref, p_ref, qi4_ref, qpad_ref, m_ref, acc_ref, *, topk):
    nb = qi_ref.shape[0]
    rows = range(nb)
    tq, ch = ATT_TQ, ATT_CH
    j = pl.program_id(1)
    nch = j + 1
    q_pos = j * tq + lax.broadcasted_iota(jnp.int32, (1, tq), 1)
    row_iota = lax.broadcasted_iota(jnp.int32, (ch, tq), 0)

    def fold8(a, op):
        return op(a.reshape(ch // 8, 8, tq), axis=0)

    def chunk_rows(kc):
        return pl.ds(pl.multiple_of(kc * ch, ch), ch)

    for bi in rows:
        for hh in range(IDX_HEADS):
            q_hi, q_lo = qi_ref[bi, hh, 0:IDX_DIM, :], qi_ref[bi, hh, IDX_DIM:, :]
            qi4_ref[bi, hh] = jnp.concatenate([q_hi, q_hi, q_lo, q_lo], axis=0)

    def score_chunk(kc, carry, diagonal):
        out = []
        for bi in rows:
            mn, mx, n_pos, n_nonneg = carry[bi]
            kic = ki_ref[bi, chunk_rows(kc), :]
            acc = jnp.zeros((ch, tq), F32)
            for hh in range(IDX_HEADS):
                dd = jnp.dot(kic, qi4_ref[bi, hh], preferred_element_type=F32)
                acc = acc + w_ref[bi, hh:hh + 1, :] * jnp.maximum(dd, 0.0)
            if diagonal:
                causal = (row_iota + kc * ch) <= q_pos
                sc = jnp.where(causal, acc, -jnp.inf)
                mn = jnp.minimum(mn, fold8(jnp.where(causal, acc, jnp.inf), jnp.min))
            else:
                sc = acc
                mn = jnp.minimum(mn, fold8(acc, jnp.min))
            sc_ref[bi, kc] = sc
            mx = jnp.maximum(mx, fold8(sc, jnp.max))
            n_pos = n_pos + fold8(jnp.where(sc > 0.0, 1.0, 0.0), jnp.sum)
            n_nonneg = n_nonneg + fold8(jnp.where(sc >= 0.0, 1.0, 0.0), jnp.sum)
            out.append((mn, mx, n_pos, n_nonneg))
        return tuple(out)

    stats0 = (jnp.full((8, tq), jnp.inf, F32), jnp.full((8, tq), -jnp.inf, F32),
              jnp.zeros((8, tq), F32), jnp.zeros((8, tq), F32))
    full = functools.partial(score_chunk, diagonal=False)
    stats = lax.fori_loop(0, j // 2, lambda t, c: full(2 * t + 1, full(2 * t, c)), tuple(stats0 for _ in rows))
    stats = lax.cond(j % 2 == 1, lambda c: score_chunk(j, full(j - 1, c), diagonal=True),
                     lambda c: score_chunk(j, c, diagonal=True), stats)

    def count(preds):
        def body(kc, cs):
            return tuple(c + jnp.sum(jnp.where(pred(sc_ref[bi, kc]), 1.0, 0.0).reshape(-1, COUNT_LANES, tq), axis=0)
                         for bi, (c, pred) in enumerate(zip(cs, preds)))
        cs = lax.fori_loop(0, nch, body, tuple(jnp.zeros((COUNT_LANES, tq), F32) for _ in rows))
        return [jnp.sum(c, axis=0, keepdims=True) for c in cs]

    n_causal = (q_pos + 1).astype(F32)
    kk = jnp.minimum(float(topk), n_causal)

    def as_f(m):
        return jnp.where(m, 1.0, 0.0)

    def initial_state(stat):
        mn = jnp.min(stat[0], axis=0, keepdims=True)
        mx = jnp.max(stat[1], axis=0, keepdims=True)
        c0_gt = jnp.sum(stat[2], axis=0, keepdims=True)
        c0_ge = jnp.sum(stat[3], axis=0, keepdims=True)
        at_zero = (c0_gt < kk) & (c0_ge >= kk)
        take_all = n_causal == kk
        pos = c0_gt >= kk
        lo0 = jnp.where(pos, jnp.maximum(mn, 0.0), mn)
        clo0 = jnp.where(pos & (mn <= 0.0), c0_ge, n_causal)
        hi0 = jnp.where(pos, jnp.minimum(mx * JUST_ABOVE + TINY, F32_MAX), 0.0)
        thr0 = jnp.where(take_all, mn, 0.0)
        tied0 = jnp.logical_not(take_all) & at_zero & (c0_ge > kk)
        done0 = at_zero | take_all
        return lo0, clo0, hi0, thr0, as_f(done0), as_f(tied0)

    def bisect_step(states):
        mids = [0.5 * st[0] + 0.5 * st[2] for st in states]
        counts = count([lambda s, mid=mid: s >= mid for mid in mids])
        out = []
        for (lo, clo, hi, thr, done, tied), mid, c in zip(states, mids, counts):
            valid = (mid > lo) & (mid < hi)
            move_lo = valid & (c >= kk)
            thr = jnp.where(done, thr, jnp.where(valid, mid, lo))
            tied = tied | (jnp.logical_not(done) & jnp.logical_not(valid) & (clo > kk))
            done = done | jnp.logical_not(valid) | (c == kk)
            hi = jnp.where(valid & (c < kk), mid, hi)
            lo = jnp.where(move_lo, mid, lo)
            clo = jnp.where(move_lo, c, clo)
            out.append((lo, clo, hi, thr, done, tied))
        return out

    def bisect_round(states, steps):
        sts = [(lo, clo, hi, thr, done_f > 0.5, tied_f > 0.5) for lo, clo, hi, thr, done_f, tied_f in states]
        for _ in range(steps):
            sts = bisect_step(sts)
        return tuple((lo, clo, hi, thr, as_f(done), as_f(tied)) for lo, clo, hi, thr, done, tied in sts)

    def bisect_body(carry):
        states = bisect_round(carry[0], BISECT_CHECKED_STEPS)
        return states, jnp.max(1.0 - functools.reduce(jnp.minimum, [st[4] for st in states]))

    blind = jnp.where((j + 1) * tq <= topk, 0, BISECT_BLIND_ROUNDS)
    states = lax.fori_loop(0, blind, lambda _, sts: bisect_round(sts, BISECT_UNROLL),
                           tuple(initial_state(stat) for stat in stats))
    states, _ = lax.while_loop(
        lambda carry: carry[1] > 0.0, bisect_body, (states, jnp.where(blind > 0, 1.0, 0.0)))
    thrs = [st[3] for st in states]
    tied_f = functools.reduce(jnp.maximum, [st[5] for st in states])

    def logit_bound_sq(bi):
        q_sq = jnp.sum(jnp.square(q_ref[bi].astype(F32)).reshape(N_HEADS, HEAD_DIM, tq), axis=1)
        q_sq = jnp.max(jnp.max(q_sq, axis=0, keepdims=True), axis=1, keepdims=True)
        k_sq = jnp.max(jnp.max(kn_ref[bi], axis=0, keepdims=True), axis=1, keepdims=True)
        return q_sq * k_sq
    worst = functools.reduce(jnp.maximum, [logit_bound_sq(bi) for bi in rows])
    bounded_f = jnp.where(worst <= LOGIT_BOUND * LOGIT_BOUND, 2.0, 0.0)

    code = jnp.max(tied_f + bounded_f)
    tie = jnp.logical_or(code == 1.0, code == 3.0)
    bounded = code >= 2.0

    @pl.when(jnp.logical_not(tie))
    def _():
        def body(kc, _):
            for bi in rows:
                sc_ref[bi, kc] = jnp.where(sc_ref[bi, kc] >= thrs[bi], 0.0, NEG)
            return 0
        lax.fori_loop(0, nch, body, 0)

    @pl.when(tie)
    def _():
        r = lax.broadcasted_iota(jnp.int32, (ch, ch), 0)
        c = lax.broadcasted_iota(jnp.int32, (ch, ch), 1)
        below = jnp.where(c < r, 1.0, 0.0).astype(BF16)
        n_gt = count([lambda s, thr=thr: s > thr for thr in thrs])
        needs = [kk - n for n in n_gt]

        def body(kc, seen):
            out = []
            for bi in rows:
                s = sc_ref[bi, kc]
                eq = jnp.where(s == thrs[bi], 1.0, 0.0)
                rank = jnp.dot(below, eq.astype(BF16), preferred_element_type=F32) + seen[bi]
                sel = (s > thrs[bi]) | ((s == thrs[bi]) & (rank < needs[bi]))
                sc_ref[bi, kc] = jnp.where(sel, 0.0, NEG)
                out.append(seen[bi] + jnp.sum(eq, axis=0, keepdims=True))
            return tuple(out)
        lax.fori_loop(0, nch, body, tuple(jnp.zeros((1, tq), F32) for _ in rows))

    zq = jnp.zeros((HEAD_DIM, tq), BF16)
    for bi in rows:
        for hh in range(N_HEADS):
            qh = q_ref[bi, hh * HEAD_DIM:(hh + 1) * HEAD_DIM, :]
            qpad_ref[bi, hh] = jnp.concatenate([qh, zq] if hh < GQA_GROUP else [zq, qh], axis=0)
    acc_ref[...] = jnp.zeros(acc_ref.shape, F32)
    m_ref[...] = jnp.zeros(m_ref.shape, F32)
    ones = jnp.ones((ONES_ROWS, ch), BF16)

    def logits(bi, kc, hh, kcs, bias):
        return jnp.dot(kcs, qpad_ref[bi, hh], preferred_element_type=F32) + bias

    @pl.when(jnp.logical_not(bounded))
    def _():
        def body(kc, ms):
            out = []
            for bi in rows:
                kcs, bias = k_ref[bi, chunk_rows(kc), :], sc_ref[bi, kc]
                out.append(tuple(jnp.maximum(ms[bi][hh], fold8(logits(bi, kc, hh, kcs, bias), jnp.max))
                                 for hh in range(N_HEADS)))
            return tuple(out)
        ms = lax.fori_loop(0, nch, body,
                           tuple(tuple(jnp.full((8, tq), NEG, F32) for _ in range(N_HEADS)) for _ in rows))
        for bi in rows:
            for hh in range(N_HEADS):
                m_ref[bi, hh] = jnp.max(ms[bi][hh], axis=0, keepdims=True)

    def prob_chunk(kc):
        for bi in rows:
            kcs, bias = k_ref[bi, chunk_rows(kc), :], sc_ref[bi, kc]
            for hh in range(N_HEADS):
                p_ref[bi, hh, kc] = jnp.exp2(logits(bi, kc, hh, kcs, bias) - m_ref[bi, hh]).astype(BF16)

    def pv_chunk(kc):
        for bi in rows:
            vt = vt_ref[bi, kc]
            for g in range(N_KV_HEADS):
                vaug = jnp.concatenate([vt[g * HEAD_DIM:(g + 1) * HEAD_DIM], ones], axis=0)
                for hh in range(g * GQA_GROUP, (g + 1) * GQA_GROUP):
                    acc_ref[bi, hh] += jnp.dot(vaug, p_ref[bi, hh, kc], preferred_element_type=F32)

    npair = nch // 2

    @pl.when(npair > 0)
    def _():
        prob_chunk(0)
        prob_chunk(1)

        def attend(t, _):
            pv_chunk(2 * t - 2)
            pv_chunk(2 * t - 1)
            prob_chunk(2 * t)
            prob_chunk(2 * t + 1)
            return 0
        lax.fori_loop(1, npair, attend, 0)
        pv_chunk(2 * npair - 2)
        pv_chunk(2 * npair - 1)

    @pl.when(nch % 2 == 1)
    def _():
        prob_chunk(nch - 1)
        pv_chunk(nch - 1)

    for bi in rows:
        outs = []
        for hh in range(N_HEADS):
            a = acc_ref[bi, hh]
            outs.append(a[0:HEAD_DIM] * (1.0 / a[HEAD_DIM:HEAD_DIM + 1]))
        y = jnp.concatenate(outs, axis=0)
        yn = y * lax.rsqrt(jnp.mean(y * y, axis=0, keepdims=True) + EPS) * g_ref[...]
        o_ref[bi] = yn.T.astype(BF16)


def _dsa_attn(qi3t, wt, ki3, qt, kk, vt, kn, g_attn_b, topk):
    bsz, s, _ = ki3.shape
    tq, ch = ATT_TQ, ATT_CH
    nb = ATT_NB if bsz % ATT_NB == 0 else 1
    nq = N_HEADS * HEAD_DIM
    nkv = N_KV_HEADS * HEAD_DIM
    return pl.pallas_call(
        functools.partial(_attn_kernel, topk=topk),
        grid=(bsz // nb, s // tq),
        in_specs=[
            pl.BlockSpec((nb, IDX_HEADS, 2 * IDX_DIM, tq), lambda b, j: (b, 0, 0, j)),
            pl.BlockSpec((nb, IDX_HEADS, tq), lambda b, j: (b, 0, j)),
            pl.BlockSpec((nb, s, IDX_K), lambda b, j: (b, 0, 0)),
            pl.BlockSpec((nb, nq, tq), lambda b, j: (b, 0, j)),
            pl.BlockSpec((nb, s, nkv), lambda b, j: (b, 0, 0)),
            pl.BlockSpec((nb, s // ch, nkv, ch), lambda b, j: (b, 0, 0, 0)),
            pl.BlockSpec((nb, N_KV_HEADS, s), lambda b, j: (b, 0, 0)),
            pl.BlockSpec((nq, tq), lambda b, j: (0, 0)),
        ],
        out_specs=pl.BlockSpec((nb, tq, nq), lambda b, j: (b, j, 0)),
        out_shape=jax.ShapeDtypeStruct((bsz, s, nq), BF16),
        scratch_shapes=[
            pltpu.VMEM((nb, s // ch, ch, tq), F32),
            pltpu.VMEM((nb, N_HEADS, s // ch, ch, tq), BF16),
            pltpu.VMEM((nb, IDX_HEADS, IDX_K, tq), BF16),
            pltpu.VMEM((nb, N_HEADS, 2 * HEAD_DIM, tq), BF16),
            pltpu.VMEM((nb, N_HEADS, 1, tq), F32),
            pltpu.VMEM((nb, N_HEADS, HEAD_DIM + ONES_ROWS, tq), F32),
        ],
        compiler_params=pltpu.CompilerParams(
            dimension_semantics=("arbitrary", "arbitrary"), vmem_limit_bytes=VMEM_LIMIT_BYTES),
        name="dsa_attn",
    )(qi3t, wt, ki3, qt, kk, vt, kn, g_attn_b)


def _out_ffn_kernel(x_ref, ya_ref, yb_ref, mod_ref, wo_ref, gpm_ref, gpf_ref, w1_ref, w2_ref, gpo_ref, o_ref):
    d_a = ya_ref.shape[2]
    tm = x_ref.shape[1]
    g1 = mod_ref[0, 2:3, :]
    sh2 = mod_ref[0, 3:4, :]
    sc2 = mod_ref[0, 4:5, :]
    g2 = mod_ref[0, 5:6, :]

    def mix_residual(rows):
        o = jnp.dot(ya_ref[0, rows, :], wo_ref[0:d_a, :], preferred_element_type=F32)
        o += jnp.dot(yb_ref[0, rows, :], wo_ref[d_a:, :], preferred_element_type=F32)
        return x_ref[0, rows, :] + g1 * _rms(o, gpm_ref[...])

    def mlp(x1):
        h2 = (_rms(x1, gpf_ref[...]) * (1.0 + sc2) + sh2).astype(BF16)
        f = jnp.zeros(x1.shape, F32)
        for c in range(w1_ref.shape[1] // FFN_TF):
            a = jnp.dot(h2, w1_ref[:, c * FFN_TF:(c + 1) * FFN_TF], preferred_element_type=F32)
            a = jnp.square(jnp.maximum(a, 0.0)).astype(BF16)
            f += jnp.dot(a, w2_ref[c * FFN_TF:(c + 1) * FFN_TF, :], preferred_element_type=F32)
        return f

    halves = [pl.ds(i * (tm // 2), tm // 2) for i in range(2)]
    x1 = [mix_residual(rows) for rows in halves]
    f = [mlp(x1h) for x1h in x1]
    for rows, x1h, fh in zip(halves, x1, f):
        o_ref[0, rows, :] = x1h + g2 * _rms(fh, gpo_ref[...])


def _out_ffn(x, ya, yb, mod3, w_out, g_pm, g_pf, w1, w2, g_po):
    bsz, s, d = x.shape
    tm = FFN_TM
    d_a = ya.shape[2]
    const2 = lambda b, i: (0, 0)
    resident = functools.partial(pl.BlockSpec, index_map=const2, pipeline_mode=pl.Buffered(1))
    return pl.pallas_call(
        _out_ffn_kernel,
        grid=(bsz, s // tm),
        in_specs=[
            pl.BlockSpec((1, tm, d), lambda b, i: (b, i, 0)),
            pl.BlockSpec((1, tm, d_a), lambda b, i: (b, i, 0)),
            pl.BlockSpec((1, tm, yb.shape[2]), lambda b, i: (b, i, 0)),
            pl.BlockSpec((1, 6, d), lambda b, i: (b, 0, 0)),
            resident(w_out.shape),
            pl.BlockSpec((1, d), const2),
            pl.BlockSpec((1, d), const2),
            resident(w1.shape),
            resident(w2.shape),
            pl.BlockSpec((1, d), const2),
        ],
        out_specs=pl.BlockSpec((1, tm, d), lambda b, i: (b, i, 0)),
        out_shape=jax.ShapeDtypeStruct((bsz, s, d), F32),
        compiler_params=pltpu.CompilerParams(
            dimension_semantics=("arbitrary", "arbitrary"), vmem_limit_bytes=VMEM_LIMIT_BYTES),
        name="out_ffn",
    )(x, ya, yb, mod3, w_out, g_pm, g_pf, w1, w2, g_po)


def _layer(x, mod3, cos_t, sin_t, g_pre_mix, w_in, g_sgu_v, w_spatial, b_spatial, g_out_sgu, g_out_attn,
           w_out, g_post_mix, g_pre_ffn, w_ff1, w_ff2, g_post_ffn):
    bsz, s, d = x.shape
    d_sgu = SGU_GROUPS * SGU_GROUP_DIM
    nq = N_HEADS * HEAD_DIM
    nkv = N_KV_HEADS * HEAD_DIM
    nqi = IDX_HEADS * IDX_DIM
    topk = min(TOPK_MAX, s // 4)

    o_q = 2 * d_sgu
    o_qi = o_q + nq + 2 * nkv
    w_uv = w_in[:, :o_q].astype(BF16)
    wt_main = w_in[:, o_q:o_qi].T.astype(BF16)
    n_idx = nqi + IDX_DIM + IDX_HEADS
    wt_idx = jnp.pad(w_in[:, o_qi:o_qi + n_idx].T, ((0, (-n_idx) % 16), (0, 0)))
    wt_idx_hi = wt_idx.astype(BF16)
    wt_idx_lo = (wt_idx - wt_idx_hi.astype(F32)).astype(BF16)
    b_sp = jnp.broadcast_to(b_spatial[:, :, None], (SGU_GROUPS, CHUNK, SGU_GROUP_DIM))

    ya, qt, kk, vt, qi3t, ki3, wt, kn = _in_proj(
        x, mod3, g_pre_mix.reshape(1, d), w_uv, wt_main, wt_idx_hi, wt_idx_lo, cos_t, sin_t,
        g_sgu_v.reshape(1, d_sgu), w_spatial, b_sp, g_out_sgu.reshape(1, d_sgu))

    g_attn_b = jnp.broadcast_to(g_out_attn[:, None], (nq, ATT_TQ))
    yb = _dsa_attn(qi3t, wt, ki3, qt, kk, vt, kn, g_attn_b, topk)

    return _out_ffn(x, ya, yb, mod3, w_out.astype(BF16), g_post_mix.reshape(1, d), g_pre_ffn.reshape(1, d),
                    w_ff1.astype(BF16), w_ff2.astype(BF16), g_post_ffn.reshape(1, d))


def kernel(x, c, positions, w_ada, b_ada, g_pre_mix, w_in, g_sgu_v, w_spatial, b_spatial, g_out_sgu, g_out_attn,
           w_out, g_post_mix, g_pre_ffn, w_ff1, w_ff2, g_post_ffn):
    bsz, s, d = x.shape
    inv_freq = ROPE_THETA ** (-jnp.arange(0, ROT_DIM, 2, dtype=F32) / ROT_DIM)
    ang = positions.astype(F32)[:, None, :] * inv_freq[None, :, None]
    cos_t, sin_t = jnp.cos(ang), jnp.sin(ang)
    for l in range(w_ada.shape[0]):
        mod3 = _ada_mod(c, w_ada[l], b_ada[l]).reshape(bsz, 6, d)
        x = _layer(x, mod3, cos_t, sin_t, g_pre_mix[l], w_in[l], g_sgu_v[l], w_spatial[l], b_spatial[l],
                   g_out_sgu[l], g_out_attn[l], w_out[l], g_post_mix[l], g_pre_ffn[l], w_ff1[l], w_ff2[l],
                   g_post_ffn[l])
    return x
```
